```python
import math
import jax, jax.numpy as jnp
from jax import lax
import numpy as np

D_MODEL = 1024
BATCH = 4
SEQ = 4096
DEPTH = 4

GRID_W = 64
N_MIXERS = 3
N_A = len(range(0, DEPTH, N_MIXERS))
N_B = len(range(1, DEPTH, N_MIXERS))
N_C = len(range(2, DEPTH, N_MIXERS))

NA_HEADS = 16
NA_HEAD_DIM = D_MODEL // NA_HEADS
NA_KH = 8
NA_KW = 16

MLA_HEADS = 16
MLA_NOPE = 64
MLA_ROPE = 32
MLA_V = D_MODEL // MLA_HEADS
MLA_Q_RANK = 256
MLA_KV_RANK = 256
ROPE_THETA = 10000.0
Q_BLOCK = 128

HG_EXPAND = 128
HG_HEADS = D_MODEL // HG_EXPAND
HG_F = HG_EXPAND
HG_V = D_MODEL // HG_HEADS
HG_CHUNK = 64

PEER_HEADS = 8
PEER_NKEYS = 128
PEER_EXPERTS = PEER_NKEYS * PEER_NKEYS
PEER_DK = 256
PEER_TOPK = 16
PEER_TOK_BLOCK = 128

NORM_EPS = 1e-5
DN_ALPHA = (2.0 * DEPTH) ** 0.25
DN_BETA = (8.0 * DEPTH) ** -0.25

kernel_name = 'hybrid_natten_mla_hgrn2_peer_encoder'

F32 = jnp.float32


def layer_norm(x, g, b):
    xf = x.astype(F32)
    mu = jnp.mean(xf, axis=-1, keepdims=True)
    var = jnp.mean(jnp.square(xf - mu), axis=-1, keepdims=True)
    return ((xf - mu) * lax.rsqrt(var + NORM_EPS) * g.astype(F32) + b.astype(F32)).astype(x.dtype)


def rms_norm(x, g):
    xf = x.astype(F32)
    y = xf * lax.rsqrt(jnp.mean(jnp.square(xf), axis=-1, keepdims=True) + NORM_EPS)
    return (y * g.astype(F32)).astype(x.dtype)


def neighborhood_attention(x, w_in, rel_bias, w_out):
    B, S, D = x.shape
    rows = S // GRID_W
    kh = min(NA_KH, rows)
    qkv = (x @ w_in).reshape(B, rows, GRID_W, 3, NA_HEADS, NA_HEAD_DIM)
    q = qkv[:, :, :, 0] * (NA_HEAD_DIM ** -0.5)
    k = qkv[:, :, :, 1]
    v = qkv[:, :, :, 2]
    cols = np.arange(GRID_W)
    c0 = np.clip(cols - NA_KW // 2, 0, GRID_W - NA_KW)
    col_idx = c0[:, None] + np.arange(NA_KW)[None, :]
    dc = col_idx - cols[:, None] + (NA_KW - 1)
    bias_c = rel_bias[:, :, dc]

    def row_step(r):
        r0 = jnp.clip(r - kh // 2, 0, rows - kh)
        k_rows = lax.dynamic_slice_in_dim(k, r0, kh, axis=1)
        v_rows = lax.dynamic_slice_in_dim(v, r0, kh, axis=1)
        k_win = k_rows[:, :, col_idx]
        v_win = v_rows[:, :, col_idx]
        q_row = lax.dynamic_index_in_dim(q, r, axis=1, keepdims=False)
        dr = r0 + jnp.arange(kh) - r + (NA_KH - 1)
        bias = jnp.take(bias_c, dr, axis=1).transpose(0, 2, 1, 3)
        s = jnp.einsum('bqhd,bkqjhd->bhqkj', q_row, k_win).astype(F32) + bias[None].astype(F32)
        p = jax.nn.softmax(s.reshape(B, NA_HEADS, GRID_W, kh * NA_KW), axis=-1)
        p = p.reshape(s.shape).astype(v.dtype)
        return jnp.einsum('bhqkj,bkqjhd->bqhd', p, v_win)

    o = lax.map(row_step, jnp.arange(rows))
    o = o.transpose(1, 0, 2, 3, 4).reshape(B, S, D)
    return o @ w_out


def apply_rope(x, cos, sin):
    half = x.shape[-1] // 2
    c = cos[None, :, None, :]
    s = sin[None, :, None, :]
    x1, x2 = x[..., :half], x[..., half:]
    return jnp.concatenate([x1 * c - x2 * s, x1 * s + x2 * c], axis=-1)


def mla_attention(x, w_in, q_norm, kv_norm, w_q_up, w_kv_up, w_out):
    B, S, D = x.shape
    dq = MLA_NOPE + MLA_ROPE
    h = x @ w_in
    cq = h[..., :MLA_Q_RANK]
    ckv = h[..., MLA_Q_RANK:MLA_Q_RANK + MLA_KV_RANK]
    k_pe = h[..., MLA_Q_RANK + MLA_KV_RANK:][:, :, None, :]
    q = (rms_norm(cq, q_norm) @ w_q_up).reshape(B, S, MLA_HEADS, dq)
    kv = (rms_norm(ckv, kv_norm) @ w_kv_up).reshape(B, S, MLA_HEADS, MLA_NOPE + MLA_V)
    half = MLA_ROPE // 2
    inv_freq = ROPE_THETA ** (-jnp.arange(half, dtype=F32) * 2.0 / MLA_ROPE)
    ang = jnp.arange(S, dtype=F32)[:, None] * inv_freq[None, :]
    cos = jnp.cos(ang).astype(x.dtype)
    sin = jnp.sin(ang).astype(x.dtype)
    q_pe = apply_rope(q[..., MLA_NOPE:], cos, sin)
    k_pe = apply_rope(k_pe, cos, sin)
    q = jnp.concatenate([q[..., :MLA_NOPE], q_pe], axis=-1) * (dq ** -0.5)
    k = jnp.concatenate([kv[..., :MLA_NOPE], jnp.broadcast_to(k_pe, (B, S, MLA_HEADS, MLA_ROPE))], axis=-1)
    v = kv[..., MLA_NOPE:]
    nb = S // Q_BLOCK
    qb = q.reshape(B, nb, Q_BLOCK, MLA_HEADS, dq).transpose(1, 0, 2, 3, 4)

    def block(qi):
        s = jnp.einsum('bqhd,bkhd->bhqk', qi, k).astype(F32)
        p = jax.nn.softmax(s, axis=-1).astype(v.dtype)
        return jnp.einsum('bhqk,bkhv->bqhv', p, v)

    o = lax.map(block, qb)
    o = o.transpose(1, 0, 2, 3, 4).reshape(B, S, MLA_HEADS * MLA_V)
    return o @ w_out


def hgrn2_chunk_scan(q, k, v, log_f):
    B, S, H, F = q.shape
    V = v.shape[-1]
    n = S // HG_CHUNK

    def to_chunks(a):
        return a.reshape(B, n, HG_CHUNK, H, a.shape[-1]).transpose(1, 0, 3, 2, 4)

    lower = jnp.tril(jnp.ones((HG_CHUNK, HG_CHUNK), dtype=bool))[None, None, :, :, None]

    def step(state, inp):
        qc, kc, vc, lc = inp
        b = jnp.cumsum(lc, axis=2)
        diff = b[:, :, :, None, :] - b[:, :, None, :, :]
        decay = jnp.exp(jnp.where(lower, diff, -jnp.inf))
        attn = jnp.einsum('bhtf,bhsf,bhtsf->bhts', qc, kc, decay)
        o = jnp.einsum('bhts,bhsv->bhtv', attn, vc) + jnp.einsum('bhtf,bhfv->bhtv', qc * jnp.exp(b), state)
        b_last = b[:, :, -1:, :]
        state = jnp.exp(b_last[:, :, 0, :])[..., None] * state + jnp.einsum('bhsf,bhsv->bhfv', kc * jnp.exp(b_last - b), vc)
        return state, o

    init = jnp.zeros((B, H, F, V), q.dtype)
    _, o = lax.scan(step, init, (to_chunks(q), to_chunks(k), to_chunks(v), to_chunks(log_f)))
    return o.transpose(1, 0, 3, 2, 4).reshape(B, S, H, V)


def hgrn2_mixer(x, w_in, lower_bound, norm_g, w_out):
    B, S, D = x.shape
    zq, zf_fwd, zf_bwd, zi, zg = jnp.split(x @ w_in, 5, axis=-1)

    def heads(a, d):
        return a.reshape(B, S, HG_HEADS, d).astype(F32)

    q = jax.nn.silu(heads(zq, HG_F))
    i = heads(zi, HG_V)
    lb = lower_bound.astype(F32).reshape(2, HG_HEADS, HG_F)

    def gates(z, lb_d):
        zf = heads(z, HG_F)
        f = lb_d + (1.0 - lb_d) * jax.nn.sigmoid(zf)
        return (1.0 - lb_d) * jax.nn.sigmoid(-zf), jnp.log(f)

    k_f, lf_f = gates(zf_fwd, lb[0])
    k_b, lf_b = gates(zf_bwd, lb[1])
    o_fwd = hgrn2_chunk_scan(q, k_f, i, lf_f)
    rev = lambda a: jnp.flip(a, axis=1)
    o_bwd = rev(hgrn2_chunk_scan(rev(q), rev(k_b), rev(i), rev(lf_b)))
    o = o_fwd + o_bwd
    o = o * lax.rsqrt(jnp.mean(jnp.square(o), axis=-1, keepdims=True) + NORM_EPS)
    o = o * norm_g.astype(F32).reshape(HG_HEADS, HG_V) * jax.nn.silu(heads(zg, HG_V))
    return o.reshape(B, S, D).astype(x.dtype) @ w_out


def peer_ffn(x, w_q, sub_keys, u, v):
    B, S, D = x.shape
    q = (x @ w_q).reshape(B, S, PEER_HEADS, 2, PEER_DK // 2)
    s = jnp.einsum('bshcd,cnd->bshcn', q, sub_keys).astype(F32)
    sv, si = lax.top_k(s, PEER_TOPK)
    cand_s = (sv[..., 0, :, None] + sv[..., 1, None, :]).reshape(B, S, PEER_HEADS, PEER_TOPK * PEER_TOPK)
    cand_i = (si[..., 0, :, None] * PEER_NKEYS + si[..., 1, None, :]).reshape(B, S, PEER_HEADS, PEER_TOPK * PEER_TOPK)
    best_s, pos = lax.top_k(cand_s, PEER_TOPK)
    e_idx = jnp.take_along_axis(cand_i, pos, axis=-1)
    gate = jax.nn.softmax(best_s, axis=-1).astype(x.dtype)
    n_tok = B * S
    nb = n_tok // PEER_TOK_BLOCK
    e_per = PEER_HEADS * PEER_TOPK
    xb = x.reshape(nb, PEER_TOK_BLOCK, D)
    eb = e_idx.reshape(nb, PEER_TOK_BLOCK, e_per)
    gb = gate.reshape(nb, PEER_TOK_BLOCK, e_per)

    def block(args):
        xt, et, gt = args
        hid = jax.nn.gelu(jnp.einsum('td,ted->te', xt, u[et]), approximate=False)
        return jnp.einsum('te,ted->td', gt * hid, v[et])

    y = lax.map(block, (xb, eb, gb))
    return y.reshape(B, S, D)


def setup_inputs(seed: int = 0) -> dict:
    key = jax.random.key(seed)
    ks = jax.random.split(key, 24)
    nrm = lambda k, shape, scale: jax.random.normal(k, shape, F32) * scale
    D = D_MODEL
    return {
        'x': nrm(ks[0], (BATCH, SEQ, D), 1.0),
        'na_w_in': nrm(ks[1], (N_A, D, 3 * D), D ** -0.5),
        'na_rel_bias': nrm(ks[2], (N_A, NA_HEADS, 2 * NA_KH - 1, 2 * NA_KW - 1), 0.1),
        'na_w_out': nrm(ks[3], (N_A, D, D), DN_BETA * D ** -0.5),
        'mla_w_in': nrm(ks[4], (N_B, D, MLA_Q_RANK + MLA_KV_RANK + MLA_ROPE), D ** -0.5),
        'mla_q_norm': 1.0 + nrm(ks[5], (N_B, MLA_Q_RANK), 0.01),
        'mla_kv_norm': 1.0 + nrm(ks[6], (N_B, MLA_KV_RANK), 0.01),
        'mla_w_q_up': nrm(ks[7], (N_B, MLA_Q_RANK, MLA_HEADS * (MLA_NOPE + MLA_ROPE)), MLA_Q_RANK ** -0.5),
        'mla_w_kv_up': nrm(ks[8], (N_B, MLA_KV_RANK, MLA_HEADS * (MLA_NOPE + MLA_V)), MLA_KV_RANK ** -0.5),
        'mla_w_out': nrm(ks[9], (N_B, MLA_HEADS * MLA_V, D), DN_BETA * (MLA_HEADS * MLA_V) ** -0.5),
        'hg_w_in': nrm(ks[10], (N_C, D, 5 * D), D ** -0.5),
        'hg_lower_bound': nrm(ks[11], (DEPTH, 2, D), 0.1),
        'hg_norm': 1.0 + nrm(ks[12], (N_C, D), 0.01),
        'hg_w_out': nrm(ks[13], (N_C, D, D), DN_BETA * D ** -0.5),
        'peer_w_q': nrm(ks[14], (DEPTH, D, PEER_HEADS * PEER_DK), D ** -0.5),
        'peer_sub_keys': nrm(ks[15], (DEPTH, 2, PEER_NKEYS, PEER_DK // 2), (PEER_DK // 2) ** -0.5),
        'peer_u': nrm(ks[16], (DEPTH, PEER_EXPERTS, D), D ** -0.5),
        'peer_v': nrm(ks[17], (DEPTH, PEER_EXPERTS, D), DN_BETA * PEER_HEADS ** -0.5),
        'ln_mix_g': 1.0 + nrm(ks[18], (DEPTH, D), 0.01),
        'ln_mix_b': nrm(ks[19], (DEPTH, D), 0.01),
        'ln_ffn_g': 1.0 + nrm(ks[20], (DEPTH, D), 0.01),
        'ln_ffn_b': nrm(ks[21], (DEPTH, D), 0.01),
    }


def reference(x, na_w_in, na_rel_bias, na_w_out, mla_w_in, mla_q_norm, mla_kv_norm, mla_w_q_up, mla_w_kv_up, mla_w_out, hg_w_in, hg_lower_bound, hg_norm, hg_w_out, peer_w_q, peer_sub_keys, peer_u, peer_v, ln_mix_g, ln_mix_b, ln_ffn_g, ln_ffn_b):
    lb_w = jax.nn.softmax(hg_lower_bound.astype(F32), axis=0)
    lb_all = jnp.cumsum(lb_w, axis=0) - lb_w[0:1]
    h = x
    for layer in range(DEPTH):
        kind = layer % N_MIXERS
        j = layer // N_MIXERS
        if kind == 0:
            mix = neighborhood_attention(h, na_w_in[j], na_rel_bias[j], na_w_out[j])
        elif kind == 1:
            mix = mla_attention(h, mla_w_in[j], mla_q_norm[j], mla_kv_norm[j], mla_w_q_up[j], mla_w_kv_up[j], mla_w_out[j])
        else:
            mix = hgrn2_mixer(h, hg_w_in[j], lb_all[layer], hg_norm[j], hg_w_out[j])
        h = layer_norm(DN_ALPHA * h + mix, ln_mix_g[layer], ln_mix_b[layer])
        ffn = peer_ffn(h, peer_w_q[layer], peer_sub_keys[layer], peer_u[layer], peer_v[layer])
        h = layer_norm(DN_ALPHA * h + ffn, ln_ffn_g[layer], ln_ffn_b[layer])
    return h
```

```python
import functools
import math

import jax
import jax.numpy as jnp
import numpy as np
from jax import lax
from jax.experimental import pallas as pl
from jax.experimental.pallas import tpu as pltpu

F32 = jnp.float32
BF16 = jnp.bfloat16

D_MODEL = 1024
DEPTH = 4
GRID_W = 64
N_MIXERS = 3

NA_HEADS = 16
NA_HEAD_DIM = 64
NA_KH = 8
NA_KW = 16

MLA_HEADS = 16
MLA_NOPE = 64
MLA_ROPE = 32
MLA_V = 64
MLA_Q_RANK = 256
MLA_KV_RANK = 256
ROPE_THETA = 10000.0

HG_HEADS = 8
HG_F = 128
HG_CHUNK = 64

PEER_HEADS = 8
PEER_NKEYS = 128
PEER_TOPK = 16
PEER_EXPERT_BLOCK = 1024

NORM_EPS = 1e-5
DN_ALPHA = (2.0 * DEPTH) ** 0.25

LANES = 128
VMEM_LIMIT = 48 * 1024 * 1024
MASK_NEG = -1e30
RSQRT2 = 0.7071067811865476


def _params(*sem):
    return pltpu.CompilerParams(dimension_semantics=sem, vmem_limit_bytes=VMEM_LIMIT)


def _layer_norm_rows(z, g, b):
    mu = jnp.mean(z, axis=-1, keepdims=True)
    zc = z - mu
    var = jnp.mean(zc * zc, axis=-1, keepdims=True)
    return zc * lax.rsqrt(var + NORM_EPS) * g + b


def _dot(a, b):
    return jnp.dot(a, b, preferred_element_type=F32)


def _dot_nt(a, b):
    return lax.dot_general(a, b, (((1,), (1,)), ((), ())), preferred_element_type=F32)


def _dot_tn(a, b):
    return lax.dot_general(a, b, (((0,), (0,)), ((), ())), preferred_element_type=F32)


def _mm_kernel(a_ref, w_ref, o_ref):
    o_ref[...] = _dot(a_ref[...].astype(BF16), w_ref[...]).astype(o_ref.dtype)


def matmul(a, w, out_dtype, tm=512, tn=1024):
    m, k = a.shape
    n = w.shape[1]
    tn = min(tn, n)
    return pl.pallas_call(
        _mm_kernel,
        grid=(m // tm, n // tn),
        in_specs=[pl.BlockSpec((tm, k), lambda i, j: (i, 0)),
                  pl.BlockSpec((k, tn), lambda i, j: (0, j))],
        out_specs=pl.BlockSpec((tm, tn), lambda i, j: (i, j)),
        out_shape=jax.ShapeDtypeStruct((m, n), out_dtype),
        compiler_params=_params("parallel", "parallel"),
        name="proj",
    )(a, w)


def _mm_res_ln_kernel(a_ref, w_ref, res_ref, g_ref, b_ref, o_ref, ot_ref):
    y = _dot(a_ref[...], w_ref[...])
    out = _layer_norm_rows(DN_ALPHA * res_ref[...] + y, g_ref[...], b_ref[...])
    o_ref[...] = out
    ot_ref[...] = out.T.astype(BF16)


def matmul_res_ln(a, w, res, g, b, tm=256):
    m, k = a.shape
    n = w.shape[1]
    return pl.pallas_call(
        _mm_res_ln_kernel,
        grid=(m // tm,),
        in_specs=[pl.BlockSpec((tm, k), lambda i: (i, 0)),
                  pl.BlockSpec((k, n), lambda i: (0, 0)),
                  pl.BlockSpec((tm, n), lambda i: (i, 0)),
                  pl.BlockSpec((1, n), lambda i: (0, 0)),
                  pl.BlockSpec((1, n), lambda i: (0, 0))],
        out_specs=[pl.BlockSpec((tm, n), lambda i: (i, 0)),
                   pl.BlockSpec((n, tm), lambda i: (0, i))],
        out_shape=[jax.ShapeDtypeStruct((m, n), F32),
                   jax.ShapeDtypeStruct((n, m), BF16)],
        compiler_params=_params("parallel"),
        name="out_proj_ln",
    )(a, w, res, g.reshape(1, n), b.reshape(1, n))


def _na_row_start(r, rows):
    return jnp.clip(r - NA_KH // 2, 0, rows - NA_KH)


def _na_kernel(q_ref, k_ref, v_ref, bias_ref, o_ref):
    n_keys = NA_KH * GRID_W
    lane = lax.broadcasted_iota(jnp.int32, (GRID_W, LANES), 1)
    low = lane < NA_HEAD_DIM
    for p in range(NA_HEADS // 2):
        cols = slice(p * LANES, (p + 1) * LANES)
        qp = q_ref[0, 0, :, cols]
        kp = k_ref[0, :, :, cols].reshape(n_keys, LANES)
        vp = v_ref[0, :, :, cols].reshape(n_keys, LANES)
        halves = []
        for half in range(2):
            qh = jnp.where(low if half == 0 else jnp.logical_not(low), qp, jnp.zeros_like(qp))
            s = _dot_nt(qh, kp) * (NA_HEAD_DIM ** -0.5) + bias_ref[0, 2 * p + half]
            m = jnp.max(s, axis=-1, keepdims=True)
            e = jnp.exp(s - m)
            l = jnp.sum(e, axis=-1, keepdims=True)
            halves.append(_dot(e.astype(BF16), vp) / l)
        o_ref[:, cols] = jnp.where(low, halves[0], halves[1]).astype(o_ref.dtype)


def _na_bias_table(rel_bias):
    cols = np.arange(GRID_W)
    c0 = np.clip(cols - NA_KW // 2, 0, GRID_W - NA_KW)
    kc = np.arange(GRID_W)
    inside = (kc[None, :] >= c0[:, None]) & (kc[None, :] < c0[:, None] + NA_KW)
    dc = np.clip(kc[None, :] - cols[:, None] + (NA_KW - 1), 0, 2 * NA_KW - 2)
    b2 = jnp.where(inside[None, None], rel_bias[:, :, dc].astype(F32), MASK_NEG)
    tabs = []
    for d0 in range(NA_KH):
        t = b2[:, d0:d0 + NA_KH]
        tabs.append(t.transpose(0, 2, 1, 3).reshape(NA_HEADS, GRID_W, NA_KH * GRID_W))
    return jnp.stack(tabs)


def neighborhood_attention(qkv, rel_bias, batch, rows):
    d = D_MODEL
    qkv4 = qkv.reshape(batch, rows, GRID_W, 3 * d)
    bias = _na_bias_table(rel_bias)
    el = pl.Element

    def kv_map(col):
        return lambda b, r: (b, _na_row_start(r, rows), 0, col)

    out = pl.pallas_call(
        _na_kernel,
        grid=(batch, rows),
        in_specs=[pl.BlockSpec((el(1), el(1), el(GRID_W), el(d)), lambda b, r: (b, r, 0, 0)),
                  pl.BlockSpec((el(1), el(NA_KH), el(GRID_W), el(d)), kv_map(d)),
                  pl.BlockSpec((el(1), el(NA_KH), el(GRID_W), el(d)), kv_map(2 * d)),
                  pl.BlockSpec((1, NA_HEADS, GRID_W, NA_KH * GRID_W),
                               lambda b, r: (_na_row_start(r, rows) - r + NA_KH - 1, 0, 0, 0))],
        out_specs=pl.BlockSpec((None, None, GRID_W, d), lambda b, r: (b, r, 0, 0)),
        out_shape=jax.ShapeDtypeStruct((batch, rows, GRID_W, d), BF16),
        compiler_params=_params("parallel", "arbitrary"),
        name="na_attn",
    )(qkv4, qkv4, qkv4, bias)
    return out.reshape(batch * rows * GRID_W, d)


def _rms_rows(x, g):
    return x * lax.rsqrt(jnp.mean(x * x, axis=-1, keepdims=True) + NORM_EPS) * g


def _mla_proj_kernel(x_ref, win_ref, qn_ref, kvn_ref, wqa_ref, wqb_ref, wkv_ref, cos_ref, sin_ref,
                     q_ref, k_ref, v_ref):
    hd = MLA_HEADS * LANES
    hin = _dot(x_ref[...].astype(BF16), win_ref[...])
    cq = _rms_rows(hin[:, :MLA_Q_RANK], qn_ref[...]).astype(BF16)
    ckv = _rms_rows(hin[:, MLA_Q_RANK:MLA_Q_RANK + MLA_KV_RANK], kvn_ref[...]).astype(BF16)
    cos = cos_ref[...]
    sin = sin_ref[...]
    cos_t = jnp.tile(cos, (1, MLA_HEADS))
    sin_t = jnp.tile(sin, (1, MLA_HEADS))
    q = _dot(cq, wqa_ref[...]) * cos_t + _dot(cq, wqb_ref[...]) * sin_t
    q_ref[...] = (q * ((MLA_NOPE + MLA_ROPE) ** -0.5)).astype(q_ref.dtype)
    kv = _dot(ckv, wkv_ref[...])
    base = MLA_Q_RANK + MLA_KV_RANK
    kpe = hin[:, base:base + LANES] * cos + hin[:, base + LANES:base + 2 * LANES] * sin
    k_ref[...] = (kv[:, :hd] + jnp.tile(kpe, (1, MLA_HEADS))).astype(k_ref.dtype)
    v_ref[...] = kv[:, hd:].astype(v_ref.dtype)


def _mla_weights(w_in, w_q_up, w_kv_up):
    r = MLA_ROPE
    half = r // 2
    dq = MLA_NOPE + r
    nh = MLA_HEADS
    base = MLA_Q_RANK + MLA_KV_RANK
    kpe = w_in[:, base:base + r]
    zpad = lambda w, lo, hi: jnp.pad(w, ((0, 0), (lo, hi)))
    swap = lambda w: jnp.concatenate([-w[..., half:], w[..., :half]], axis=-1)
    kpe_a = zpad(kpe, MLA_NOPE, LANES - MLA_NOPE - r)
    kpe_b = zpad(swap(kpe), MLA_NOPE, LANES - MLA_NOPE - r)
    win = jnp.concatenate([w_in[:, :base], kpe_a, kpe_b], axis=1)
    wq = w_q_up.reshape(MLA_Q_RANK, nh, dq)
    pad3 = lambda w, lo, hi: jnp.pad(w, ((0, 0), (0, 0), (lo, hi)))
    wqa = pad3(wq, 0, LANES - dq).reshape(MLA_Q_RANK, nh * LANES)
    wqb = pad3(swap(wq[:, :, MLA_NOPE:]), MLA_NOPE, LANES - dq).reshape(MLA_Q_RANK, nh * LANES)
    wkv = w_kv_up.reshape(MLA_KV_RANK, nh, MLA_NOPE + MLA_V)
    wk = pad3(wkv[:, :, :MLA_NOPE], 0, LANES - MLA_NOPE).reshape(MLA_KV_RANK, nh * LANES)
    wv = wkv[:, :, MLA_NOPE:].reshape(MLA_KV_RANK, nh * MLA_V)
    return (win.astype(BF16), wqa.astype(BF16), wqb.astype(BF16),
            jnp.concatenate([wk, wv], axis=1).astype(BF16))


def _rope_tables(seq):
    half = MLA_ROPE // 2
    inv_freq = ROPE_THETA ** (-jnp.arange(half, dtype=F32) * 2.0 / MLA_ROPE)
    ang = jnp.arange(seq, dtype=F32)[:, None] * inv_freq[None, :]
    cos = jnp.cos(ang)
    sin = jnp.sin(ang)
    tail = LANES - MLA_NOPE - MLA_ROPE
    cos_p = jnp.concatenate([jnp.ones((seq, MLA_NOPE), F32), cos, cos, jnp.zeros((seq, tail), F32)], axis=1)
    sin_p = jnp.concatenate([jnp.zeros((seq, MLA_NOPE), F32), sin, sin, jnp.zeros((seq, tail), F32)], axis=1)
    return cos_p, sin_p


def mla_projections(h, w_in, q_norm, kv_norm, w_q_up, w_kv_up, seq, tm=256):
    n, d = h.shape
    win, wqa, wqb, wkv = _mla_weights(w_in, w_q_up, w_kv_up)
    cos_p, sin_p = _rope_tables(seq)
    hd = MLA_HEADS * LANES
    per_seq = seq // tm
    full = lambda a: pl.BlockSpec(a.shape, lambda i: (0,) * a.ndim)
    qn = q_norm.reshape(1, -1).astype(F32)
    kvn = kv_norm.reshape(1, -1).astype(F32)
    return pl.pallas_call(
        _mla_proj_kernel,
        grid=(n // tm,),
        in_specs=[pl.BlockSpec((tm, d), lambda i: (i, 0)), full(win), full(qn), full(kvn),
                  full(wqa), full(wqb), full(wkv),
                  pl.BlockSpec((tm, LANES), lambda i: (i % per_seq, 0)),
                  pl.BlockSpec((tm, LANES), lambda i: (i % per_seq, 0))],
        out_specs=[pl.BlockSpec((tm, hd), lambda i: (i, 0)),
                   pl.BlockSpec((tm, hd), lambda i: (i, 0)),
                   pl.BlockSpec((tm, MLA_HEADS * MLA_V), lambda i: (i, 0))],
        out_shape=[jax.ShapeDtypeStruct((n, hd), BF16),
                   jax.ShapeDtypeStruct((n, hd), BF16),
                   jax.ShapeDtypeStruct((n, MLA_HEADS * MLA_V), BF16)],
        compiler_params=_params("parallel"),
        name="mla_proj",
    )(h, win, qn, kvn, wqa, wqb, wkv, cos_p, sin_p)


def _mla_attn_kernel(q_ref, k_ref, v_ref, o_ref):
    v = v_ref[...]
    lane = lax.broadcasted_iota(jnp.int32, o_ref.shape, 1)
    halves = []
    for half in range(2):
        cols = slice(half * LANES, (half + 1) * LANES)
        s = _dot_nt(q_ref[:, cols], k_ref[:, cols])
        m = jnp.max(s, axis=-1, keepdims=True)
        e = jnp.exp(s - m)
        l = jnp.sum(e, axis=-1, keepdims=True)
        halves.append(_dot(e.astype(BF16), v) / l)
    o_ref[...] = jnp.where(lane < MLA_V, halves[0], halves[1]).astype(o_ref.dtype)


def mla_attention(q, k, v, batch, seq, tq=256):
    hd = MLA_HEADS * LANES
    q3 = q.reshape(batch, seq, hd)
    k3 = k.reshape(batch, seq, hd)
    v3 = v.reshape(batch, seq, MLA_HEADS * MLA_V)
    out = pl.pallas_call(
        _mla_attn_kernel,
        grid=(batch, MLA_HEADS // 2, seq // tq),
        in_specs=[pl.BlockSpec((None, tq, 2 * LANES), lambda b, p, i: (b, i, p)),
                  pl.BlockSpec((None, seq, 2 * LANES), lambda b, p, i: (b, 0, p)),
                  pl.BlockSpec((None, seq, LANES), lambda b, p, i: (b, 0, p))],
        out_specs=pl.BlockSpec((None, tq, LANES), lambda b, p, i: (b, i, p)),
        out_shape=jax.ShapeDtypeStruct((batch, seq, MLA_HEADS * MLA_V), BF16),
        compiler_params=_params("parallel", "parallel", "arbitrary"),
        name="mla_attn",
    )(q3, k3, v3)
    return out.reshape(batch * seq, MLA_HEADS * MLA_V)


HG_LEVELS = int(math.log2(HG_CHUNK))


def _hg_constants(reverse):
    c = HG_CHUNK
    t = np.arange(c)[:, None]
    u = np.arange(c)[None, :]
    if not reverse:
        incl = u <= t
        rest = u > t
    else:
        incl = u >= t
        rest = u < t
    mats = [incl, rest]
    masks = []
    roles = []
    for lvl in range(1, HG_LEVELS + 1):
        size = 1 << lvl
        start = (t // size) * size
        mid = start + size // 2
        upper = t >= mid
        if not reverse:
            q_side = (u >= mid) & (u <= t)
            k_side = (u > t) & (u <= mid - 1)
            is_query = upper
        else:
            q_side = (u >= t) & (u < mid)
            k_side = (u >= mid) & (u < t)
            is_query = ~upper
        mats.append(np.where(is_query, q_side, k_side))
        same = (t // size) == (u // size)
        key_row = (~is_query).T
        masks.append(same & is_query & np.broadcast_to(key_row, (c, c)))
        roles.append(np.broadcast_to(is_query, (c, LANES)))
    w = np.concatenate(mats, axis=0).astype(np.float32)
    return (jnp.asarray(w, BF16), jnp.asarray(np.stack(masks).astype(np.float32)),
            jnp.asarray(np.stack(roles).astype(np.float32)))


def _hg_chunk(q, zf, v_b, lb, w_ref, m_ref, r_ref, st_ref, total_row):
    c = HG_CHUNK
    sg = jax.nn.sigmoid(zf)
    k = (1.0 - lb) * jax.nn.sigmoid(-zf)
    lf = jnp.log(lb + (1.0 - lb) * sg)
    hi = lf.astype(BF16)
    r1 = lf - hi.astype(F32)
    mid = r1.astype(BF16)
    lo = (r1 - mid.astype(F32)).astype(BF16)
    w = w_ref[...]
    ex = _dot(w, hi) + _dot(w, mid) + _dot(w, lo)
    b_incl = ex[0:c]
    total = ex[total_row:total_row + 1]
    qd = q * jnp.exp(b_incl)
    kd = k * jnp.exp(ex[c:2 * c])
    row = lax.broadcasted_iota(jnp.int32, (c, c), 0)
    col = lax.broadcasted_iota(jnp.int32, (c, c), 1)
    a = jnp.where(row == col, jnp.sum(q * k, axis=-1, keepdims=True), 0.0)
    for lvl in range(HG_LEVELS):
        x = jnp.where(r_ref[lvl] > 0.5, q, k) * jnp.exp(ex[(2 + lvl) * c:(3 + lvl) * c])
        xb = x.astype(BF16)
        a = a + m_ref[lvl] * _dot_nt(xb, xb)
    st = st_ref[...]
    o = _dot(a.astype(BF16), v_b) + _dot_nt(qd.astype(BF16), st.astype(BF16))
    st_ref[...] = st * jnp.exp(total) + _dot_tn(v_b, kd.astype(BF16))
    return o


def _hg_kernel(zq_ref, zff_ref, zfb_ref, zi_ref, zg_ref, lb_ref, g_ref,
               wf_ref, mf_ref, rf_ref, wb_ref, mb_ref, rb_ref,
               o_ref, acc_ref, stf_ref, stb_ref):
    c = HG_CHUNK
    seq = zq_ref.shape[0]
    n = seq // c
    acc_ref[...] = jnp.zeros_like(acc_ref)
    stf_ref[...] = jnp.zeros_like(stf_ref)
    stb_ref[...] = jnp.zeros_like(stb_ref)
    lb_f = lb_ref[0:1, :]
    lb_b = lb_ref[1:2, :]

    def gated(ref, rows):
        z = ref[rows, :]
        return z * jax.nn.sigmoid(z)

    def body(i, carry):
        rf = pl.ds(pl.multiple_of(i * c, c), c)
        rb = pl.ds(pl.multiple_of((n - 1 - i) * c, c), c)
        of = _hg_chunk(gated(zq_ref, rf), zff_ref[rf, :], zi_ref[rf, :].astype(BF16), lb_f,
                       wf_ref, mf_ref, rf_ref, stf_ref, c - 1)
        acc_ref[rf, :] += of
        ob = _hg_chunk(gated(zq_ref, rb), zfb_ref[rb, :], zi_ref[rb, :].astype(BF16), lb_b,
                       wb_ref, mb_ref, rb_ref, stb_ref, 0)
        acc_ref[rb, :] += ob
        return carry

    lax.fori_loop(0, n, body, 0)

    def finish(i, carry):
        rows = pl.ds(pl.multiple_of(i * c, c), c)
        o = acc_ref[rows, :]
        o = o * lax.rsqrt(jnp.mean(o * o, axis=-1, keepdims=True) + NORM_EPS)
        o_ref[rows, :] = (o * g_ref[...] * gated(zg_ref, rows)).astype(o_ref.dtype)
        return carry

    lax.fori_loop(0, n, finish, 0)


def hgrn2_scan(z, lb, norm_g, batch, seq):
    d = D_MODEL
    z3 = z.reshape(batch, seq, 5 * d)
    wf, mf, rf = _hg_constants(False)
    wb, mb, rb = _hg_constants(True)
    zspec = lambda j: pl.BlockSpec((None, seq, HG_F), lambda b, h: (b, 0, j * HG_HEADS + h))
    full = lambda a: pl.BlockSpec(a.shape, lambda b, h: (0,) * a.ndim)
    out = pl.pallas_call(
        _hg_kernel,
        grid=(batch, HG_HEADS),
        in_specs=[zspec(0), zspec(1), zspec(2), zspec(3), zspec(4),
                  pl.BlockSpec((2, HG_F), lambda b, h: (0, h)),
                  pl.BlockSpec((1, HG_F), lambda b, h: (0, h)),
                  full(wf), full(mf), full(rf), full(wb), full(mb), full(rb)],
        out_specs=pl.BlockSpec((None, seq, HG_F), lambda b, h: (b, 0, h)),
        out_shape=jax.ShapeDtypeStruct((batch, seq, d), BF16),
        scratch_shapes=[pltpu.VMEM((seq, HG_F), F32),
                        pltpu.VMEM((HG_F, HG_F), F32),
                        pltpu.VMEM((HG_F, HG_F), F32)],
        compiler_params=_params("parallel", "parallel"),
        name="hgrn2_scan",
    )(z3, z3, z3, z3, z3, lb.astype(F32), norm_g.reshape(1, d).astype(F32),
      wf, mf, rf, wb, mb, rb)
    return out.reshape(batch * seq, d)


def _top_values(s, count):
    vals = []
    cur = s
    for _ in range(count):
        m = jnp.max(cur, axis=0, keepdims=True)
        vals.append(m)
        cur = jnp.where(cur >= m, -jnp.inf, cur)
    return jnp.concatenate(vals, axis=0)


def _peer_route_kernel(ht_ref, wq_ref, keys_ref, thr_ref, a0_ref, s1_ref, b1_ref):
    kk = PEER_TOPK
    ht = ht_ref[...]
    for h in range(PEER_HEADS):
        scores = []
        tops = []
        for c in range(2):
            g = 2 * h + c
            qt = _dot(wq_ref[g * LANES:(g + 1) * LANES, :], ht)
            s = _dot(keys_ref[c], qt.astype(BF16))
            scores.append(s)
            tops.append(_top_values(s, kk))
        s0, s1 = scores
        sv0, sv1 = tops
        cands = [sv0[a:a + 1] + sv1[0:kk // (a + 1)] for a in range(kk)]
        n_cand = sum(kk // (a + 1) for a in range(kk))
        pad = (-n_cand) % 8
        if pad:
            cands.append(jnp.full((pad, ht.shape[1]), -jnp.inf, F32))
        tau = _top_values(jnp.concatenate(cands, axis=0), kk)[kk - 1:kk]
        e0 = jnp.exp(sv0 - sv0[0:1])
        e1 = jnp.exp(sv1 - sv1[0:1])
        z = jnp.zeros_like(tau)
        for a in range(kk):
            nb = kk // (a + 1)
            sel = (sv0[a:a + 1] + sv1[0:nb]) >= tau
            z = z + e0[a:a + 1] * jnp.sum(jnp.where(sel, e1[0:nb], 0.0), axis=0, keepdims=True)
        thr = jnp.full(s0.shape, jnp.inf, F32)
        for b in range(kk):
            vb = sv1[b:b + 1]
            thr = jnp.minimum(thr, jnp.where(s0 + vb >= tau, vb, jnp.inf))
        thr_ref[h] = thr
        a0_ref[h] = jnp.exp(s0 - sv0[0:1])
        s1_ref[h] = s1
        b1_ref[h] = jnp.exp(s1 - sv1[0:1]) / z


def peer_route(ht, wq_t, keys, tt=256):
    d, n = ht.shape
    shp = jax.ShapeDtypeStruct((PEER_HEADS, PEER_NKEYS, n), F32)
    ospec = pl.BlockSpec((PEER_HEADS, PEER_NKEYS, tt), lambda i: (0, 0, i))
    return pl.pallas_call(
        _peer_route_kernel,
        grid=(n // tt,),
        in_specs=[pl.BlockSpec((d, tt), lambda i: (0, i)),
                  pl.BlockSpec(wq_t.shape, lambda i: (0, 0)),
                  pl.BlockSpec(keys.shape, lambda i: (0, 0, 0))],
        out_specs=[ospec, ospec, ospec, ospec],
        out_shape=[shp, shp, shp, shp],
        compiler_params=_params("parallel"),
        name="peer_route",
    )(ht, wq_t, keys)


def _peer_ffn_kernel(ht_ref, u_ref, vt_ref, thr_ref, a0_ref, s1_ref, b1_ref, res_ref, g_ref, b_ref,
                     o_ref, acc_ref, w_ref):
    e = pl.program_id(1)
    tt = ht_ref.shape[1]
    rows_per_block = PEER_EXPERT_BLOCK // PEER_NKEYS

    @pl.when(e == 0)
    def _():
        acc_ref[...] = jnp.zeros_like(acc_ref)

    w_ref[...] = _dot(u_ref[...], ht_ref[...])

    def row_body(ii, carry):
        rows = pl.ds(pl.multiple_of(ii * PEER_NKEYS, PEER_NKEYS), PEER_NKEYS)
        for tc in range(tt // LANES):
            cols = slice(tc * LANES, (tc + 1) * LANES)
            gate = jnp.zeros((PEER_NKEYS, LANES), F32)
            for h in range(PEER_HEADS):
                thr = thr_ref[h, ii, :, cols]
                a0 = a0_ref[h, ii, :, cols]
                gate = gate + jnp.where(s1_ref[h, :, cols] >= thr, a0 * b1_ref[h, :, cols], 0.0)
            hid = w_ref[rows, cols]
            w_ref[rows, cols] = gate * (0.5 * hid * (1.0 + lax.erf(hid * RSQRT2)))
        return carry

    lax.fori_loop(0, rows_per_block, row_body, 0)
    acc_ref[...] += _dot(vt_ref[...], w_ref[...].astype(BF16))

    @pl.when(e == pl.num_programs(1) - 1)
    def _():
        y = acc_ref[...].T
        o_ref[...] = _layer_norm_rows(DN_ALPHA * res_ref[...] + y, g_ref[...], b_ref[...])


def peer_ffn(h, ht, route, u, v_t, g, b, tt=256):
    n, d = h.shape
    n_exp = u.shape[0]
    eb = PEER_EXPERT_BLOCK
    rspec = pl.BlockSpec((PEER_HEADS, PEER_NKEYS, tt), lambda i, e: (0, 0, i))
    rows_per_block = eb // PEER_NKEYS
    rowspec = pl.BlockSpec((PEER_HEADS, rows_per_block, 1, tt), lambda i, e: (0, e, 0, i))
    thr, a0, s1, b1 = route
    as_rows = lambda a: a.reshape(PEER_HEADS, PEER_NKEYS, 1, n)
    return pl.pallas_call(
        _peer_ffn_kernel,
        grid=(n // tt, n_exp // eb),
        in_specs=[pl.BlockSpec((d, tt), lambda i, e: (0, i)),
                  pl.BlockSpec((eb, d), lambda i, e: (e, 0)),
                  pl.BlockSpec((d, eb), lambda i, e: (0, e)),
                  rowspec, rowspec, rspec, rspec,
                  pl.BlockSpec((tt, d), lambda i, e: (i, 0)),
                  pl.BlockSpec((1, d), lambda i, e: (0, 0)),
                  pl.BlockSpec((1, d), lambda i, e: (0, 0))],
        out_specs=pl.BlockSpec((tt, d), lambda i, e: (i, 0)),
        out_shape=jax.ShapeDtypeStruct((n, d), F32),
        scratch_shapes=[pltpu.VMEM((d, tt), F32), pltpu.VMEM((eb, tt), F32)],
        compiler_params=_params("parallel", "arbitrary"),
        name="peer_ffn",
    )(ht, u, v_t, as_rows(thr), as_rows(a0), s1, b1, h, g.reshape(1, d), b.reshape(1, d))


def kernel(x, na_w_in, na_rel_bias, na_w_out, mla_w_in, mla_q_norm, mla_kv_norm, mla_w_q_up, mla_w_kv_up, mla_w_out, hg_w_in, hg_lower_bound, hg_norm, hg_w_out, peer_w_q, peer_sub_keys, peer_u, peer_v, ln_mix_g, ln_mix_b, ln_ffn_g, ln_ffn_b):
    batch, seq, d = x.shape
    rows = seq // GRID_W
    lb_w = jax.nn.softmax(hg_lower_bound.astype(F32), axis=0)
    lb_all = jnp.cumsum(lb_w, axis=0) - lb_w[0:1]
    h = x.reshape(batch * seq, d)
    for layer in range(DEPTH):
        kind = layer % N_MIXERS
        j = layer // N_MIXERS
        if kind == 0:
            qkv = matmul(h, na_w_in[j].astype(BF16), BF16)
            mix_in = neighborhood_attention(qkv, na_rel_bias[j], batch, rows)
            w_out = na_w_out[j]
        elif kind == 1:
            q, k, v = mla_projections(h, mla_w_in[j], mla_q_norm[j], mla_kv_norm[j],
                                      mla_w_q_up[j], mla_w_kv_up[j], seq)
            mix_in = mla_attention(q, k, v, batch, seq)
            w_out = mla_w_out[j]
        else:
            z = matmul(h, hg_w_in[j].astype(BF16), F32)
            mix_in = hgrn2_scan(z, lb_all[layer], hg_norm[j], batch, seq)
            w_out = hg_w_out[j]
        h, ht = matmul_res_ln(mix_in, w_out.astype(BF16), h, ln_mix_g[layer], ln_mix_b[layer])
        route = peer_route(ht, peer_w_q[layer].T.astype(BF16), peer_sub_keys[layer].astype(BF16))
        h = peer_ffn(h, ht, route, peer_u[layer].astype(BF16), peer_v[layer].T.astype(BF16),
                     ln_ffn_g[layer], ln_ffn_b[layer])
    return h.reshape(batch, seq, d)
```

```python
import functools
import math

import jax
import jax.numpy as jnp
import numpy as np
from jax import lax
from jax.experimental import pallas as pl
from jax.experimental.pallas import tpu as pltpu

F32 = jnp.float32
BF16 = jnp.bfloat16

D_MODEL = 1024
DEPTH = 4
GRID_W = 64
N_MIXERS = 3

NA_HEADS = 16
NA_HEAD_DIM = 64
NA_KH = 8
NA_KW = 16

MLA_HEADS = 16
MLA_NOPE = 64
MLA_ROPE = 32
MLA_V = 64
MLA_Q_RANK = 256
MLA_KV_RANK = 256
ROPE_THETA = 10000.0

HG_HEADS = 8
HG_F = 128
HG_CHUNK = 64

PEER_HEADS = 8
PEER_NKEYS = 128
PEER_TOPK = 16
PEER_EXPERT_BLOCK = 1024

NORM_EPS = 1e-5
DN_ALPHA = (2.0 * DEPTH) ** 0.25

LANES = 128
BF16_SUBLANES = 16
VMEM_LIMIT = 48 * 1024 * 1024
MASK_NEG = -1e30
RSQRT2 = 0.7071067811865476


def _params(*sem):
    return pltpu.CompilerParams(dimension_semantics=sem, vmem_limit_bytes=VMEM_LIMIT)


def _layer_norm_rows(z, g, b):
    mu = jnp.mean(z, axis=-1, keepdims=True)
    zc = z - mu
    var = jnp.mean(zc * zc, axis=-1, keepdims=True)
    return zc * lax.rsqrt(var + NORM_EPS) * g + b


def _dot(a, b):
    return jnp.dot(a, b, preferred_element_type=F32)


def _dot_nt(a, b):
    return lax.dot_general(a, b, (((1,), (1,)), ((), ())), preferred_element_type=F32)


def _dot_tn(a, b):
    return lax.dot_general(a, b, (((0,), (0,)), ((), ())), preferred_element_type=F32)


def _mm_kernel(a_ref, w_ref, o_ref):
    o_ref[...] = _dot(a_ref[...].astype(BF16), w_ref[...]).astype(o_ref.dtype)


def matmul(a, w, out_dtype, tm=512, tn=1024):
    m, k = a.shape
    n = w.shape[1]
    tn = min(tn, n)
    return pl.pallas_call(
        _mm_kernel,
        grid=(m // tm, n // tn),
        in_specs=[pl.BlockSpec((tm, k), lambda i, j: (i, 0)),
                  pl.BlockSpec((k, tn), lambda i, j: (0, j))],
        out_specs=pl.BlockSpec((tm, tn), lambda i, j: (i, j)),
        out_shape=jax.ShapeDtypeStruct((m, n), out_dtype),
        compiler_params=_params("parallel", "parallel"),
        name="proj",
    )(a, w)


def _mm_res_ln_kernel(a_ref, w_ref, res_ref, g_ref, b_ref, o_ref, ot_ref):
    y = _dot(a_ref[...], w_ref[...])
    out = _layer_norm_rows(DN_ALPHA * res_ref[...] + y, g_ref[...], b_ref[...])
    o_ref[...] = out
    ot_ref[...] = out.T.astype(BF16)


def matmul_res_ln(a, w, res, g, b, tm=256):
    m, k = a.shape
    n = w.shape[1]
    return pl.pallas_call(
        _mm_res_ln_kernel,
        grid=(m // tm,),
        in_specs=[pl.BlockSpec((tm, k), lambda i: (i, 0)),
                  pl.BlockSpec((k, n), lambda i: (0, 0)),
                  pl.BlockSpec((tm, n), lambda i: (i, 0)),
                  pl.BlockSpec((1, n), lambda i: (0, 0)),
                  pl.BlockSpec((1, n), lambda i: (0, 0))],
        out_specs=[pl.BlockSpec((tm, n), lambda i: (i, 0)),
                   pl.BlockSpec((n, tm), lambda i: (0, i))],
        out_shape=[jax.ShapeDtypeStruct((m, n), F32),
                   jax.ShapeDtypeStruct((n, m), BF16)],
        compiler_params=_params("parallel"),
        name="out_proj_ln",
    )(a, w, res, g.reshape(1, n), b.reshape(1, n))


def _na_row_start(r, rows):
    return jnp.clip(r - NA_KH // 2, 0, rows - NA_KH)


def _na_kernel(q_ref, k_ref, v_ref, bias_ref, o_ref):
    n_keys = NA_KH * GRID_W
    lane = lax.broadcasted_iota(jnp.int32, (GRID_W, LANES), 1)
    low = lane < NA_HEAD_DIM
    for p in range(NA_HEADS // 2):
        cols = slice(p * LANES, (p + 1) * LANES)
        qp = q_ref[0, 0, :, cols]
        kp = k_ref[0, :, :, cols].reshape(n_keys, LANES)
        vp = v_ref[0, :, :, cols].reshape(n_keys, LANES)
        halves = []
        for half in range(2):
            qh = jnp.where(low if half == 0 else jnp.logical_not(low), qp, jnp.zeros_like(qp))
            s = _dot_nt(qh, kp) * (NA_HEAD_DIM ** -0.5) + bias_ref[0, 2 * p + half]
            m = jnp.max(s, axis=-1, keepdims=True)
            e = jnp.exp(s - m)
            l = jnp.sum(e, axis=-1, keepdims=True)
            halves.append(_dot(e.astype(BF16), vp) / l)
        o_ref[:, cols] = jnp.where(low, halves[0], halves[1]).astype(o_ref.dtype)


def _na_bias_table(rel_bias):
    cols = np.arange(GRID_W)
    c0 = np.clip(cols - NA_KW // 2, 0, GRID_W - NA_KW)
    kc = np.arange(GRID_W)
    inside = (kc[None, :] >= c0[:, None]) & (kc[None, :] < c0[:, None] + NA_KW)
    dc = np.clip(kc[None, :] - cols[:, None] + (NA_KW - 1), 0, 2 * NA_KW - 2)
    b2 = jnp.where(inside[None, None], rel_bias[:, :, dc].astype(F32), MASK_NEG)
    tabs = []
    for d0 in range(NA_KH):
        t = b2[:, d0:d0 + NA_KH]
        tabs.append(t.transpose(0, 2, 1, 3).reshape(NA_HEADS, GRID_W, NA_KH * GRID_W))
    return jnp.stack(tabs)


def neighborhood_attention(qkv, rel_bias, batch, rows):
    d = D_MODEL
    qkv4 = qkv.reshape(batch, rows, GRID_W, 3 * d)
    bias = _na_bias_table(rel_bias)
    el = pl.Element

    def kv_map(col):
        return lambda b, r: (b, _na_row_start(r, rows), 0, col)

    out = pl.pallas_call(
        _na_kernel,
        grid=(batch, rows),
        in_specs=[pl.BlockSpec((el(1), el(1), el(GRID_W), el(d)), lambda b, r: (b, r, 0, 0)),
                  pl.BlockSpec((el(1), el(NA_KH), el(GRID_W), el(d)), kv_map(d)),
                  pl.BlockSpec((el(1), el(NA_KH), el(GRID_W), el(d)), kv_map(2 * d)),
                  pl.BlockSpec((1, NA_HEADS, GRID_W, NA_KH * GRID_W),
                               lambda b, r: (_na_row_start(r, rows) - r + NA_KH - 1, 0, 0, 0))],
        out_specs=pl.BlockSpec((None, None, GRID_W, d), lambda b, r: (b, r, 0, 0)),
        out_shape=jax.ShapeDtypeStruct((batch, rows, GRID_W, d), BF16),
        compiler_params=_params("parallel", "arbitrary"),
        name="na_attn",
    )(qkv4, qkv4, qkv4, bias)
    return out.reshape(batch * rows * GRID_W, d)


def _rms_rows(x, g):
    return x * lax.rsqrt(jnp.mean(x * x, axis=-1, keepdims=True) + NORM_EPS) * g


def _mla_proj_kernel(x_ref, win_ref, qn_ref, kvn_ref, wqa_ref, wqb_ref, wkv_ref, cos_ref, sin_ref,
                     q_ref, k_ref, v_ref):
    hd = MLA_HEADS * LANES
    hin = _dot(x_ref[...].astype(BF16), win_ref[...])
    cq = _rms_rows(hin[:, :MLA_Q_RANK], qn_ref[...]).astype(BF16)
    ckv = _rms_rows(hin[:, MLA_Q_RANK:MLA_Q_RANK + MLA_KV_RANK], kvn_ref[...]).astype(BF16)
    cos = cos_ref[...]
    sin = sin_ref[...]
    cos_t = jnp.tile(cos, (1, MLA_HEADS))
    sin_t = jnp.tile(sin, (1, MLA_HEADS))
    q = _dot(cq, wqa_ref[...]) * cos_t + _dot(cq, wqb_ref[...]) * sin_t
    q_ref[...] = (q * ((MLA_NOPE + MLA_ROPE) ** -0.5)).astype(q_ref.dtype)
    kv = _dot(ckv, wkv_ref[...])
    base = MLA_Q_RANK + MLA_KV_RANK
    kpe = hin[:, base:base + LANES] * cos + hin[:, base + LANES:base + 2 * LANES] * sin
    k_ref[...] = (kv[:, :hd] + jnp.tile(kpe, (1, MLA_HEADS))).astype(k_ref.dtype)
    v_ref[...] = kv[:, hd:].astype(v_ref.dtype)


def _mla_weights(w_in, w_q_up, w_kv_up):
    r = MLA_ROPE
    half = r // 2
    dq = MLA_NOPE + r
    nh = MLA_HEADS
    base = MLA_Q_RANK + MLA_KV_RANK
    kpe = w_in[:, base:base + r]
    zpad = lambda w, lo, hi: jnp.pad(w, ((0, 0), (lo, hi)))
    swap = lambda w: jnp.concatenate([-w[..., half:], w[..., :half]], axis=-1)
    kpe_a = zpad(kpe, MLA_NOPE, LANES - MLA_NOPE - r)
    kpe_b = zpad(swap(kpe), MLA_NOPE, LANES - MLA_NOPE - r)
    win = jnp.concatenate([w_in[:, :base], kpe_a, kpe_b], axis=1)
    wq = w_q_up.reshape(MLA_Q_RANK, nh, dq)
    pad3 = lambda w, lo, hi: jnp.pad(w, ((0, 0), (0, 0), (lo, hi)))
    wqa = pad3(wq, 0, LANES - dq).reshape(MLA_Q_RANK, nh * LANES)
    wqb = pad3(swap(wq[:, :, MLA_NOPE:]), MLA_NOPE, LANES - dq).reshape(MLA_Q_RANK, nh * LANES)
    wkv = w_kv_up.reshape(MLA_KV_RANK, nh, MLA_NOPE + MLA_V)
    wk = pad3(wkv[:, :, :MLA_NOPE], 0, LANES - MLA_NOPE).reshape(MLA_KV_RANK, nh * LANES)
    wv = wkv[:, :, MLA_NOPE:].reshape(MLA_KV_RANK, nh * MLA_V)
    return (win.astype(BF16), wqa.astype(BF16), wqb.astype(BF16),
            jnp.concatenate([wk, wv], axis=1).astype(BF16))


def _rope_tables(seq):
    half = MLA_ROPE // 2
    inv_freq = ROPE_THETA ** (-jnp.arange(half, dtype=F32) * 2.0 / MLA_ROPE)
    ang = jnp.arange(seq, dtype=F32)[:, None] * inv_freq[None, :]
    cos = jnp.cos(ang)
    sin = jnp.sin(ang)
    tail = LANES - MLA_NOPE - MLA_ROPE
    cos_p = jnp.concatenate([jnp.ones((seq, MLA_NOPE), F32), cos, cos, jnp.zeros((seq, tail), F32)], axis=1)
    sin_p = jnp.concatenate([jnp.zeros((seq, MLA_NOPE), F32), sin, sin, jnp.zeros((seq, tail), F32)], axis=1)
    return cos_p, sin_p


def mla_projections(h, w_in, q_norm, kv_norm, w_q_up, w_kv_up, seq, tm=256):
    n, d = h.shape
    win, wqa, wqb, wkv = _mla_weights(w_in, w_q_up, w_kv_up)
    cos_p, sin_p = _rope_tables(seq)
    hd = MLA_HEADS * LANES
    per_seq = seq // tm
    full = lambda a: pl.BlockSpec(a.shape, lambda i: (0,) * a.ndim)
    qn = q_norm.reshape(1, -1).astype(F32)
    kvn = kv_norm.reshape(1, -1).astype(F32)
    return pl.pallas_call(
        _mla_proj_kernel,
        grid=(n // tm,),
        in_specs=[pl.BlockSpec((tm, d), lambda i: (i, 0)), full(win), full(qn), full(kvn),
                  full(wqa), full(wqb), full(wkv),
                  pl.BlockSpec((tm, LANES), lambda i: (i % per_seq, 0)),
                  pl.BlockSpec((tm, LANES), lambda i: (i % per_seq, 0))],
        out_specs=[pl.BlockSpec((tm, hd), lambda i: (i, 0)),
                   pl.BlockSpec((tm, hd), lambda i: (i, 0)),
                   pl.BlockSpec((tm, MLA_HEADS * MLA_V), lambda i: (i, 0))],
        out_shape=[jax.ShapeDtypeStruct((n, hd), BF16),
                   jax.ShapeDtypeStruct((n, hd), BF16),
                   jax.ShapeDtypeStruct((n, MLA_HEADS * MLA_V), BF16)],
        compiler_params=_params("parallel"),
        name="mla_proj",
    )(h, win, qn, kvn, wqa, wqb, wkv, cos_p, sin_p)


def _mla_attn_kernel(q_ref, k_ref, v_ref, o_ref):
    v = v_ref[...]
    lane = lax.broadcasted_iota(jnp.int32, o_ref.shape, 1)
    halves = []
    for half in range(2):
        cols = slice(half * LANES, (half + 1) * LANES)
        s = _dot_nt(q_ref[:, cols], k_ref[:, cols])
        m = jnp.max(s, axis=-1, keepdims=True)
        e = jnp.exp(s - m)
        l = jnp.sum(e, axis=-1, keepdims=True)
        halves.append(_dot(e.astype(BF16), v) / l)
    o_ref[...] = jnp.where(lane < MLA_V, halves[0], halves[1]).astype(o_ref.dtype)


def mla_attention(q, k, v, batch, seq, tq=256):
    hd = MLA_HEADS * LANES
    q3 = q.reshape(batch, seq, hd)
    k3 = k.reshape(batch, seq, hd)
    v3 = v.reshape(batch, seq, MLA_HEADS * MLA_V)
    out = pl.pallas_call(
        _mla_attn_kernel,
        grid=(batch, MLA_HEADS // 2, seq // tq),
        in_specs=[pl.BlockSpec((None, tq, 2 * LANES), lambda b, p, i: (b, i, p)),
                  pl.BlockSpec((None, seq, 2 * LANES), lambda b, p, i: (b, 0, p)),
                  pl.BlockSpec((None, seq, LANES), lambda b, p, i: (b, 0, p))],
        out_specs=pl.BlockSpec((None, tq, LANES), lambda b, p, i: (b, i, p)),
        out_shape=jax.ShapeDtypeStruct((batch, seq, MLA_HEADS * MLA_V), BF16),
        compiler_params=_params("parallel", "parallel", "arbitrary"),
        name="mla_attn",
    )(q3, k3, v3)
    return out.reshape(batch * seq, MLA_HEADS * MLA_V)


HG_LEVELS = int(math.log2(HG_CHUNK))


def _hg_constants(reverse):
    c = HG_CHUNK
    t = np.arange(c)[:, None]
    u = np.arange(c)[None, :]
    if not reverse:
        incl = u <= t
        rest = u > t
    else:
        incl = u >= t
        rest = u < t
    mats = [incl, rest]
    masks = []
    roles = []
    for lvl in range(1, HG_LEVELS + 1):
        size = 1 << lvl
        start = (t // size) * size
        mid = start + size // 2
        upper = t >= mid
        if not reverse:
            q_side = (u >= mid) & (u <= t)
            k_side = (u > t) & (u <= mid - 1)
            is_query = upper
        else:
            q_side = (u >= t) & (u < mid)
            k_side = (u >= mid) & (u < t)
            is_query = ~upper
        mats.append(np.where(is_query, q_side, k_side))
        same = (t // size) == (u // size)
        key_row = (~is_query).T
        masks.append(same & is_query & np.broadcast_to(key_row, (c, c)))
        roles.append(np.broadcast_to(is_query, (c, LANES)))
    w = np.concatenate(mats, axis=0).astype(np.float32)
    return (jnp.asarray(w, BF16), jnp.asarray(np.stack(masks).astype(np.float32)),
            jnp.asarray(np.stack(roles).astype(np.float32)))


def _hg_chunk(q, zf, v_b, lb, w_ref, m_ref, r_ref, st_ref, total_row):
    c = HG_CHUNK
    sg = jax.nn.sigmoid(zf)
    k = (1.0 - lb) * jax.nn.sigmoid(-zf)
    lf = jnp.log(lb + (1.0 - lb) * sg)
    hi = lf.astype(BF16)
    r1 = lf - hi.astype(F32)
    mid = r1.astype(BF16)
    lo = (r1 - mid.astype(F32)).astype(BF16)
    w = w_ref[...]
    ex = _dot(w, hi) + _dot(w, mid) + _dot(w, lo)
    b_incl = ex[0:c]
    total = ex[total_row:total_row + 1]
    qd = q * jnp.exp(b_incl)
    kd = k * jnp.exp(ex[c:2 * c])
    row = lax.broadcasted_iota(jnp.int32, (c, c), 0)
    col = lax.broadcasted_iota(jnp.int32, (c, c), 1)
    a = jnp.where(row == col, jnp.sum(q * k, axis=-1, keepdims=True), 0.0)
    for lvl in range(HG_LEVELS):
        x = jnp.where(r_ref[lvl] > 0.5, q, k) * jnp.exp(ex[(2 + lvl) * c:(3 + lvl) * c])
        xb = x.astype(BF16)
        a = a + m_ref[lvl] * _dot_nt(xb, xb)
    st = st_ref[...]
    o = _dot(a.astype(BF16), v_b) + _dot_nt(qd.astype(BF16), st.astype(BF16))
    st_ref[...] = st * jnp.exp(total) + _dot_tn(v_b, kd.astype(BF16))
    return o


def _hg_kernel(zq_ref, zff_ref, zfb_ref, zi_ref, zg_ref, lb_ref, g_ref,
               wf_ref, mf_ref, rf_ref, wb_ref, mb_ref, rb_ref,
               o_ref, acc_ref, stf_ref, stb_ref):
    c = HG_CHUNK
    seq = zq_ref.shape[0]
    n = seq // c
    acc_ref[...] = jnp.zeros_like(acc_ref)
    stf_ref[...] = jnp.zeros_like(stf_ref)
    stb_ref[...] = jnp.zeros_like(stb_ref)
    lb_f = lb_ref[0:1, :]
    lb_b = lb_ref[1:2, :]

    def gated(ref, rows):
        z = ref[rows, :]
        return z * jax.nn.sigmoid(z)

    def body(i, carry):
        rf = pl.ds(pl.multiple_of(i * c, c), c)
        rb = pl.ds(pl.multiple_of((n - 1 - i) * c, c), c)
        of = _hg_chunk(gated(zq_ref, rf), zff_ref[rf, :], zi_ref[rf, :].astype(BF16), lb_f,
                       wf_ref, mf_ref, rf_ref, stf_ref, c - 1)
        acc_ref[rf, :] += of
        ob = _hg_chunk(gated(zq_ref, rb), zfb_ref[rb, :], zi_ref[rb, :].astype(BF16), lb_b,
                       wb_ref, mb_ref, rb_ref, stb_ref, 0)
        acc_ref[rb, :] += ob
        return carry

    lax.fori_loop(0, n, body, 0)

    def finish(i, carry):
        rows = pl.ds(pl.multiple_of(i * c, c), c)
        o = acc_ref[rows, :]
        o = o * lax.rsqrt(jnp.mean(o * o, axis=-1, keepdims=True) + NORM_EPS)
        o_ref[rows, :] = (o * g_ref[...] * gated(zg_ref, rows)).astype(o_ref.dtype)
        return carry

    lax.fori_loop(0, n, finish, 0)


def hgrn2_scan(z, lb, norm_g, batch, seq):
    d = D_MODEL
    z3 = z.reshape(batch, seq, 5 * d)
    wf, mf, rf = _hg_constants(False)
    wb, mb, rb = _hg_constants(True)
    zspec = lambda j: pl.BlockSpec((None, seq, HG_F), lambda b, h: (b, 0, j * HG_HEADS + h))
    full = lambda a: pl.BlockSpec(a.shape, lambda b, h: (0,) * a.ndim)
    out = pl.pallas_call(
        _hg_kernel,
        grid=(batch, HG_HEADS),
        in_specs=[zspec(0), zspec(1), zspec(2), zspec(3), zspec(4),
                  pl.BlockSpec((2, HG_F), lambda b, h: (0, h)),
                  pl.BlockSpec((1, HG_F), lambda b, h: (0, h)),
                  full(wf), full(mf), full(rf), full(wb), full(mb), full(rb)],
        out_specs=pl.BlockSpec((None, seq, HG_F), lambda b, h: (b, 0, h)),
        out_shape=jax.ShapeDtypeStruct((batch, seq, d), BF16),
        scratch_shapes=[pltpu.VMEM((seq, HG_F), F32),
                        pltpu.VMEM((HG_F, HG_F), F32),
                        pltpu.VMEM((HG_F, HG_F), F32)],
        compiler_params=_params("parallel", "parallel"),
        name="hgrn2_scan",
    )(z3, z3, z3, z3, z3, lb.astype(F32), norm_g.reshape(1, d).astype(F32),
      wf, mf, rf, wb, mb, rb)
    return out.reshape(batch * seq, d)


def _top_values(s, count, with_rank=False):
    vals = []
    cur = s
    rank = jnp.full(s.shape, float(count), F32)
    for it in range(count):
        m = jnp.max(cur, axis=0, keepdims=True)
        vals.append(m)
        hit = cur >= m
        if with_rank:
            rank = jnp.where(hit, float(it), rank)
        cur = jnp.where(hit, -jnp.inf, cur)
    vals = jnp.concatenate(vals, axis=0)
    return (vals, rank) if with_rank else vals


def _peer_route_kernel(ht_ref, wq_ref, keys_ref, cnt_ref, a0_ref, r1_ref, b1_ref):
    kk = PEER_TOPK
    ht = ht_ref[...]
    for h in range(PEER_HEADS):
        scores = []
        for c in range(2):
            g = 2 * h + c
            qt = _dot(wq_ref[g * LANES:(g + 1) * LANES, :], ht)
            scores.append(_dot(keys_ref[c], qt.astype(BF16)))
        s0, s1 = scores
        sv0 = _top_values(s0, kk)
        sv1, r1 = _top_values(s1, kk, with_rank=True)
        cands = [sv0[a:a + 1] + sv1[0:kk // (a + 1)] for a in range(kk)]
        n_cand = sum(kk // (a + 1) for a in range(kk))
        pad = (-n_cand) % 8
        if pad:
            cands.append(jnp.full((pad, ht.shape[1]), -jnp.inf, F32))
        tau = _top_values(jnp.concatenate(cands, axis=0), kk)[kk - 1:kk]
        e0 = jnp.exp(sv0 - sv0[0:1])
        e1 = jnp.exp(sv1 - sv1[0:1])
        z = jnp.zeros_like(tau)
        for a in range(kk):
            nb = kk // (a + 1)
            sel = (sv0[a:a + 1] + sv1[0:nb]) >= tau
            z = z + e0[a:a + 1] * jnp.sum(jnp.where(sel, e1[0:nb], 0.0), axis=0, keepdims=True)
        cnt = jnp.zeros(s0.shape, F32)
        for b in range(kk):
            cnt = cnt + jnp.where(s0 + sv1[b:b + 1] >= tau, 1.0, 0.0)
        cnt_ref[h] = cnt
        a0_ref[h] = jnp.exp(s0 - sv0[0:1])
        r1_ref[h] = r1.astype(BF16)
        b1_ref[h] = (jnp.exp(s1 - sv1[0:1]) / z).astype(BF16)


def peer_route(ht, wq_t, keys, tt=256):
    d, n = ht.shape
    shape = (PEER_HEADS, PEER_NKEYS, n)
    ospec = pl.BlockSpec((PEER_HEADS, PEER_NKEYS, tt), lambda i: (0, 0, i))
    return pl.pallas_call(
        _peer_route_kernel,
        grid=(n // tt,),
        in_specs=[pl.BlockSpec((d, tt), lambda i: (0, i)),
                  pl.BlockSpec(wq_t.shape, lambda i: (0, 0)),
                  pl.BlockSpec(keys.shape, lambda i: (0, 0, 0))],
        out_specs=[ospec, ospec, ospec, ospec],
        out_shape=[jax.ShapeDtypeStruct(shape, F32), jax.ShapeDtypeStruct(shape, F32),
                   jax.ShapeDtypeStruct(shape, BF16), jax.ShapeDtypeStruct(shape, BF16)],
        compiler_params=_params("parallel"),
        name="peer_route",
    )(ht, wq_t, keys)


def _peer_gate_block(hid_ref, w_ref, cnt_ref, a0_ref, r1_ref, b1_ref):
    tt = hid_ref.shape[1]
    zero = jnp.zeros((PEER_NKEYS, LANES), BF16)

    def row_tile(ref, h, ii, cols):
        row = jnp.broadcast_to(ref[h, ii, :, cols], (BF16_SUBLANES, LANES)).astype(BF16)
        return pltpu.repeat(row, PEER_NKEYS // BF16_SUBLANES, axis=0)

    for ii in range(PEER_EXPERT_BLOCK // PEER_NKEYS):
        rows = slice(ii * PEER_NKEYS, (ii + 1) * PEER_NKEYS)
        for tc in range(tt // LANES):
            cols = slice(tc * LANES, (tc + 1) * LANES)
            gate = zero
            for h in range(PEER_HEADS):
                cnt = row_tile(cnt_ref, h, ii, cols)
                a0 = row_tile(a0_ref, h, ii, cols)
                gate = gate + jnp.where(r1_ref[h, :, cols] < cnt, a0 * b1_ref[h, :, cols], zero)
            hid = hid_ref[rows, cols]
            act = (0.5 * hid * (1.0 + lax.erf(hid * RSQRT2))).astype(BF16)
            w_ref[rows, cols] = gate * act


def _peer_ffn_kernel(ht_ref, ua_ref, ub_ref, vtp_ref, vtq_ref, cntp_ref, a0p_ref, cnta_ref, a0a_ref,
                     r1_in_ref, b1_in_ref, res_ref, g_ref, b_ref,
                     o_ref, acc_ref, hidp_ref, hida_ref, wp_ref, wq_ref, r1_ref, b1_ref):
    g = pl.program_id(1)

    @pl.when(g == 0)
    def _():
        acc_ref[...] = jnp.zeros_like(acc_ref)
        hidp_ref[...] = jnp.zeros_like(hidp_ref)
        wp_ref[...] = jnp.zeros_like(wp_ref)
        r1_ref[...] = r1_in_ref[...]
        b1_ref[...] = b1_in_ref[...]

    ht = ht_ref[...]
    acc_ref[...] += _dot(vtp_ref[...], wp_ref[...])
    _peer_gate_block(hidp_ref, wq_ref, cntp_ref, a0p_ref, r1_ref, b1_ref)
    hida_ref[...] = _dot(ua_ref[...], ht)

    acc_ref[...] += _dot(vtq_ref[...], wq_ref[...])
    _peer_gate_block(hida_ref, wp_ref, cnta_ref, a0a_ref, r1_ref, b1_ref)
    hidp_ref[...] = _dot(ub_ref[...], ht)

    @pl.when(g == pl.num_programs(1) - 1)
    def _():
        y = acc_ref[...].T
        o_ref[...] = _layer_norm_rows(DN_ALPHA * res_ref[...] + y, g_ref[...], b_ref[...])


def peer_ffn(h, ht, route, u, v_t, g, b, tt=256):
    n, d = h.shape
    eb = PEER_EXPERT_BLOCK
    nb = u.shape[0] // eb
    last = nb - 1
    rspec = pl.BlockSpec((PEER_HEADS, PEER_NKEYS, tt), lambda i, s: (0, 0, i))
    rows_per_block = eb // PEER_NKEYS
    rowspec = lambda blk: pl.BlockSpec((PEER_HEADS, rows_per_block, 1, tt),
                                       lambda i, s: (0, blk(s), 0, i))
    prev_b = lambda s: jnp.maximum(2 * s - 1, 0)
    this_a = lambda s: jnp.minimum(2 * s, last)
    cnt, a0, r1, b1 = route
    cnt = cnt.reshape(PEER_HEADS, PEER_NKEYS, 1, n)
    a0 = a0.reshape(PEER_HEADS, PEER_NKEYS, 1, n)
    return pl.pallas_call(
        _peer_ffn_kernel,
        grid=(n // tt, nb // 2 + 1),
        in_specs=[pl.BlockSpec((d, tt), lambda i, s: (0, i)),
                  pl.BlockSpec((eb, d), lambda i, s: (this_a(s), 0)),
                  pl.BlockSpec((eb, d), lambda i, s: (jnp.minimum(2 * s + 1, last), 0)),
                  pl.BlockSpec((d, eb), lambda i, s: (0, jnp.maximum(2 * s - 2, 0))),
                  pl.BlockSpec((d, eb), lambda i, s: (0, prev_b(s))),
                  rowspec(prev_b), rowspec(prev_b), rowspec(this_a), rowspec(this_a),
                  rspec, rspec,
                  pl.BlockSpec((tt, d), lambda i, s: (i, 0)),
                  pl.BlockSpec((1, d), lambda i, s: (0, 0)),
                  pl.BlockSpec((1, d), lambda i, s: (0, 0))],
        out_specs=pl.BlockSpec((tt, d), lambda i, s: (i, 0)),
        out_shape=jax.ShapeDtypeStruct((n, d), F32),
        scratch_shapes=[pltpu.VMEM((d, tt), F32),
                        pltpu.VMEM((eb, tt), F32), pltpu.VMEM((eb, tt), F32),
                        pltpu.VMEM((eb, tt), BF16), pltpu.VMEM((eb, tt), BF16),
                        pltpu.VMEM((PEER_HEADS, PEER_NKEYS, tt), BF16),
                        pltpu.VMEM((PEER_HEADS, PEER_NKEYS, tt), BF16)],
        compiler_params=_params("parallel", "arbitrary"),
        name="peer_ffn",
    )(ht, u, u, v_t, v_t, cnt, a0, cnt, a0, r1, b1, h, g.reshape(1, d), b.reshape(1, d))


def kernel(x, na_w_in, na_rel_bias, na_w_out, mla_w_in, mla_q_norm, mla_kv_norm, mla_w_q_up, mla_w_kv_up, mla_w_out, hg_w_in, hg_lower_bound, hg_norm, hg_w_out, peer_w_q, peer_sub_keys, peer_u, peer_v, ln_mix_g, ln_mix_b, ln_ffn_g, ln_ffn_b):
    batch, seq, d = x.shape
    rows = seq // GRID_W
    lb_w = jax.nn.softmax(hg_lower_bound.astype(F32), axis=0)
    lb_all = jnp.cumsum(lb_w, axis=0) - lb_w[0:1]
    h = x.reshape(batch * seq, d)
    for layer in range(DEPTH):
        kind = layer % N_MIXERS
        j = layer // N_MIXERS
        if kind == 0:
            qkv = matmul(h, na_w_in[j].astype(BF16), BF16)
            mix_in = neighborhood_attention(qkv, na_rel_bias[j], batch, rows)
            w_out = na_w_out[j]
        elif kind == 1:
            q, k, v = mla_projections(h, mla_w_in[j], mla_q_norm[j], mla_kv_norm[j],
                                      mla_w_q_up[j], mla_w_kv_up[j], seq)
            mix_in = mla_attention(q, k, v, batch, seq)
            w_out = mla_w_out[j]
        else:
            z = matmul(h, hg_w_in[j].astype(BF16), F32)
            mix_in = hgrn2_scan(z, lb_all[layer], hg_norm[j], batch, seq)
            w_out = hg_w_out[j]
        h, ht = matmul_res_ln(mix_in, w_out.astype(BF16), h, ln_mix_g[layer], ln_mix_b[layer])
        route = peer_route(ht, peer_w_q[layer].T.astype(BF16), peer_sub_keys[layer].astype(BF16))
        h = peer_ffn(h, ht, route, peer_u[layer].astype(BF16), peer_v[layer].T.astype(BF16),
                     ln_ffn_g[layer], ln_ffn_b[layer])
    return h.reshape(batch, seq, d)
```

```python
import functools
import math

import jax
import jax.numpy as jnp
import numpy as np
from jax import lax
from jax.experimental import pallas as pl
from jax.experimental.pallas import tpu as pltpu

F32 = jnp.float32
BF16 = jnp.bfloat16

D_MODEL = 1024
DEPTH = 4
GRID_W = 64
N_MIXERS = 3

NA_HEADS = 16
NA_HEAD_DIM = 64
NA_KH = 8
NA_KW = 16

MLA_HEADS = 16
MLA_NOPE = 64
MLA_ROPE = 32
MLA_V = 64
MLA_Q_RANK = 256
MLA_KV_RANK = 256
ROPE_THETA = 10000.0

HG_HEADS = 8
HG_F = 128
HG_CHUNK = 128

PEER_HEADS = 8
PEER_NKEYS = 128
PEER_TOPK = 16
PEER_EXPERT_BLOCK = 1024

NORM_EPS = 1e-5
DN_ALPHA = (2.0 * DEPTH) ** 0.25

LANES = 128
BF16_SUBLANES = 16
VMEM_LIMIT = 48 * 1024 * 1024
MASK_NEG = -1e30
RSQRT2 = 0.7071067811865476


def _params(*sem):
    return pltpu.CompilerParams(dimension_semantics=sem, vmem_limit_bytes=VMEM_LIMIT)


def _layer_norm_rows(z, g, b):
    mu = jnp.mean(z, axis=-1, keepdims=True)
    zc = z - mu
    var = jnp.mean(zc * zc, axis=-1, keepdims=True)
    return zc * lax.rsqrt(var + NORM_EPS) * g + b


def _dot(a, b):
    return jnp.dot(a, b, preferred_element_type=F32)


def _dot_nt(a, b):
    return lax.dot_general(a, b, (((1,), (1,)), ((), ())), preferred_element_type=F32)


def _dot_tn(a, b):
    return lax.dot_general(a, b, (((0,), (0,)), ((), ())), preferred_element_type=F32)


def _mm_kernel(a_ref, w_ref, o_ref):
    o_ref[...] = _dot(a_ref[...].astype(BF16), w_ref[...]).astype(o_ref.dtype)


def matmul(a, w, out_dtype, tm=512, tn=1024):
    m, k = a.shape
    n = w.shape[1]
    tn = min(tn, n)
    return pl.pallas_call(
        _mm_kernel,
        grid=(m // tm, n // tn),
        in_specs=[pl.BlockSpec((tm, k), lambda i, j: (i, 0)),
                  pl.BlockSpec((k, tn), lambda i, j: (0, j))],
        out_specs=pl.BlockSpec((tm, tn), lambda i, j: (i, j)),
        out_shape=jax.ShapeDtypeStruct((m, n), out_dtype),
        compiler_params=_params("parallel", "parallel"),
        name="proj",
    )(a, w)


def _mm_res_ln_kernel(a_ref, w_ref, res_ref, g_ref, b_ref, o_ref, ot_ref):
    y = _dot(a_ref[...], w_ref[...])
    out = _layer_norm_rows(DN_ALPHA * res_ref[...] + y, g_ref[...], b_ref[...])
    o_ref[...] = out
    ot_ref[...] = out.T.astype(BF16)


def matmul_res_ln(a, w, res, g, b, tm=256):
    m, k = a.shape
    n = w.shape[1]
    return pl.pallas_call(
        _mm_res_ln_kernel,
        grid=(m // tm,),
        in_specs=[pl.BlockSpec((tm, k), lambda i: (i, 0)),
                  pl.BlockSpec((k, n), lambda i: (0, 0)),
                  pl.BlockSpec((tm, n), lambda i: (i, 0)),
                  pl.BlockSpec((1, n), lambda i: (0, 0)),
                  pl.BlockSpec((1, n), lambda i: (0, 0))],
        out_specs=[pl.BlockSpec((tm, n), lambda i: (i, 0)),
                   pl.BlockSpec((n, tm), lambda i: (0, i))],
        out_shape=[jax.ShapeDtypeStruct((m, n), F32),
                   jax.ShapeDtypeStruct((n, m), BF16)],
        compiler_params=_params("parallel"),
        name="out_proj_ln",
    )(a, w, res, g.reshape(1, n), b.reshape(1, n))


def _na_row_start(r, rows):
    return jnp.clip(r - NA_KH // 2, 0, rows - NA_KH)


def _na_kernel(q_ref, k_ref, v_ref, bias_ref, o_ref):
    n_keys = NA_KH * GRID_W
    lane = lax.broadcasted_iota(jnp.int32, (GRID_W, LANES), 1)
    low = lane < NA_HEAD_DIM
    for p in range(NA_HEADS // 2):
        cols = slice(p * LANES, (p + 1) * LANES)
        qp = q_ref[0, 0, :, cols]
        kp = k_ref[0, :, :, cols].reshape(n_keys, LANES)
        vp = v_ref[0, :, :, cols].reshape(n_keys, LANES)
        halves = []
        for half in range(2):
            qh = jnp.where(low if half == 0 else jnp.logical_not(low), qp, jnp.zeros_like(qp))
            s = _dot_nt(qh, kp) * (NA_HEAD_DIM ** -0.5) + bias_ref[0, 2 * p + half]
            m = jnp.max(s, axis=-1, keepdims=True)
            e = jnp.exp(s - m)
            l = jnp.sum(e, axis=-1, keepdims=True)
            halves.append(_dot(e.astype(BF16), vp) / l)
        o_ref[:, cols] = jnp.where(low, halves[0], halves[1]).astype(o_ref.dtype)


def _na_bias_table(rel_bias):
    cols = np.arange(GRID_W)
    c0 = np.clip(cols - NA_KW // 2, 0, GRID_W - NA_KW)
    kc = np.arange(GRID_W)
    inside = (kc[None, :] >= c0[:, None]) & (kc[None, :] < c0[:, None] + NA_KW)
    dc = np.clip(kc[None, :] - cols[:, None] + (NA_KW - 1), 0, 2 * NA_KW - 2)
    b2 = jnp.where(inside[None, None], rel_bias[:, :, dc].astype(F32), MASK_NEG)
    tabs = []
    for d0 in range(NA_KH):
        t = b2[:, d0:d0 + NA_KH]
        tabs.append(t.transpose(0, 2, 1, 3).reshape(NA_HEADS, GRID_W, NA_KH * GRID_W))
    return jnp.stack(tabs)


def neighborhood_attention(qkv, rel_bias, batch, rows):
    d = D_MODEL
    qkv4 = qkv.reshape(batch, rows, GRID_W, 3 * d)
    bias = _na_bias_table(rel_bias)
    el = pl.Element

    def kv_map(col):
        return lambda b, r: (b, _na_row_start(r, rows), 0, col)

    out = pl.pallas_call(
        _na_kernel,
        grid=(batch, rows),
        in_specs=[pl.BlockSpec((el(1), el(1), el(GRID_W), el(d)), lambda b, r: (b, r, 0, 0)),
                  pl.BlockSpec((el(1), el(NA_KH), el(GRID_W), el(d)), kv_map(d)),
                  pl.BlockSpec((el(1), el(NA_KH), el(GRID_W), el(d)), kv_map(2 * d)),
                  pl.BlockSpec((1, NA_HEADS, GRID_W, NA_KH * GRID_W),
                               lambda b, r: (_na_row_start(r, rows) - r + NA_KH - 1, 0, 0, 0))],
        out_specs=pl.BlockSpec((None, None, GRID_W, d), lambda b, r: (b, r, 0, 0)),
        out_shape=jax.ShapeDtypeStruct((batch, rows, GRID_W, d), BF16),
        compiler_params=_params("parallel", "arbitrary"),
        name="na_attn",
    )(qkv4, qkv4, qkv4, bias)
    return out.reshape(batch * rows * GRID_W, d)


def _rms_rows(x, g):
    return x * lax.rsqrt(jnp.mean(x * x, axis=-1, keepdims=True) + NORM_EPS) * g


def _mla_proj_kernel(x_ref, win_ref, qn_ref, kvn_ref, wqa_ref, wqb_ref, wkv_ref, cos_ref, sin_ref,
                     q_ref, k_ref, v_ref):
    hd = MLA_HEADS * LANES
    hin = _dot(x_ref[...].astype(BF16), win_ref[...])
    cq = _rms_rows(hin[:, :MLA_Q_RANK], qn_ref[...]).astype(BF16)
    ckv = _rms_rows(hin[:, MLA_Q_RANK:MLA_Q_RANK + MLA_KV_RANK], kvn_ref[...]).astype(BF16)
    cos = cos_ref[...]
    sin = sin_ref[...]
    cos_t = jnp.tile(cos, (1, MLA_HEADS))
    sin_t = jnp.tile(sin, (1, MLA_HEADS))
    q = _dot(cq, wqa_ref[...]) * cos_t + _dot(cq, wqb_ref[...]) * sin_t
    q_ref[...] = (q * ((MLA_NOPE + MLA_ROPE) ** -0.5)).astype(q_ref.dtype)
    kv = _dot(ckv, wkv_ref[...])
    base = MLA_Q_RANK + MLA_KV_RANK
    kpe = hin[:, base:base + LANES] * cos + hin[:, base + LANES:base + 2 * LANES] * sin
    k_ref[...] = (kv[:, :hd] + jnp.tile(kpe, (1, MLA_HEADS))).astype(k_ref.dtype)
    v_ref[...] = kv[:, hd:].astype(v_ref.dtype)


def _mla_weights(w_in, w_q_up, w_kv_up):
    r = MLA_ROPE
    half = r // 2
    dq = MLA_NOPE + r
    nh = MLA_HEADS
    base = MLA_Q_RANK + MLA_KV_RANK
    kpe = w_in[:, base:base + r]
    zpad = lambda w, lo, hi: jnp.pad(w, ((0, 0), (lo, hi)))
    swap = lambda w: jnp.concatenate([-w[..., half:], w[..., :half]], axis=-1)
    kpe_a = zpad(kpe, MLA_NOPE, LANES - MLA_NOPE - r)
    kpe_b = zpad(swap(kpe), MLA_NOPE, LANES - MLA_NOPE - r)
    win = jnp.concatenate([w_in[:, :base], kpe_a, kpe_b], axis=1)
    wq = w_q_up.reshape(MLA_Q_RANK, nh, dq)
    pad3 = lambda w, lo, hi: jnp.pad(w, ((0, 0), (0, 0), (lo, hi)))
    wqa = pad3(wq, 0, LANES - dq).reshape(MLA_Q_RANK, nh * LANES)
    wqb = pad3(swap(wq[:, :, MLA_NOPE:]), MLA_NOPE, LANES - dq).reshape(MLA_Q_RANK, nh * LANES)
    wkv = w_kv_up.reshape(MLA_KV_RANK, nh, MLA_NOPE + MLA_V)
    wk = pad3(wkv[:, :, :MLA_NOPE], 0, LANES - MLA_NOPE).reshape(MLA_KV_RANK, nh * LANES)
    wv = wkv[:, :, MLA_NOPE:].reshape(MLA_KV_RANK, nh * MLA_V)
    return (win.astype(BF16), wqa.astype(BF16), wqb.astype(BF16),
            jnp.concatenate([wk, wv], axis=1).astype(BF16))


def _rope_tables(seq):
    half = MLA_ROPE // 2
    inv_freq = ROPE_THETA ** (-jnp.arange(half, dtype=F32) * 2.0 / MLA_ROPE)
    ang = jnp.arange(seq, dtype=F32)[:, None] * inv_freq[None, :]
    cos = jnp.cos(ang)
    sin = jnp.sin(ang)
    tail = LANES - MLA_NOPE - MLA_ROPE
    cos_p = jnp.concatenate([jnp.ones((seq, MLA_NOPE), F32), cos, cos, jnp.zeros((seq, tail), F32)], axis=1)
    sin_p = jnp.concatenate([jnp.zeros((seq, MLA_NOPE), F32), sin, sin, jnp.zeros((seq, tail), F32)], axis=1)
    return cos_p, sin_p


def mla_projections(h, w_in, q_norm, kv_norm, w_q_up, w_kv_up, seq, tm=256):
    n, d = h.shape
    win, wqa, wqb, wkv = _mla_weights(w_in, w_q_up, w_kv_up)
    cos_p, sin_p = _rope_tables(seq)
    hd = MLA_HEADS * LANES
    per_seq = seq // tm
    full = lambda a: pl.BlockSpec(a.shape, lambda i: (0,) * a.ndim)
    qn = q_norm.reshape(1, -1).astype(F32)
    kvn = kv_norm.reshape(1, -1).astype(F32)
    return pl.pallas_call(
        _mla_proj_kernel,
        grid=(n // tm,),
        in_specs=[pl.BlockSpec((tm, d), lambda i: (i, 0)), full(win), full(qn), full(kvn),
                  full(wqa), full(wqb), full(wkv),
                  pl.BlockSpec((tm, LANES), lambda i: (i % per_seq, 0)),
                  pl.BlockSpec((tm, LANES), lambda i: (i % per_seq, 0))],
        out_specs=[pl.BlockSpec((tm, hd), lambda i: (i, 0)),
                   pl.BlockSpec((tm, hd), lambda i: (i, 0)),
                   pl.BlockSpec((tm, MLA_HEADS * MLA_V), lambda i: (i, 0))],
        out_shape=[jax.ShapeDtypeStruct((n, hd), BF16),
                   jax.ShapeDtypeStruct((n, hd), BF16),
                   jax.ShapeDtypeStruct((n, MLA_HEADS * MLA_V), BF16)],
        compiler_params=_params("parallel"),
        name="mla_proj",
    )(h, win, qn, kvn, wqa, wqb, wkv, cos_p, sin_p)


def _mla_attn_kernel(q_ref, k_ref, v_ref, o_ref):
    v = v_ref[...]
    lane = lax.broadcasted_iota(jnp.int32, o_ref.shape, 1)
    halves = []
    for half in range(2):
        cols = slice(half * LANES, (half + 1) * LANES)
        s = _dot_nt(q_ref[:, cols], k_ref[:, cols])
        m = jnp.max(s, axis=-1, keepdims=True)
        e = jnp.exp(s - m)
        l = jnp.sum(e, axis=-1, keepdims=True)
        halves.append(_dot(e.astype(BF16), v) / l)
    o_ref[...] = jnp.where(lane < MLA_V, halves[0], halves[1]).astype(o_ref.dtype)


def mla_attention(q, k, v, batch, seq, tq=256):
    hd = MLA_HEADS * LANES
    q3 = q.reshape(batch, seq, hd)
    k3 = k.reshape(batch, seq, hd)
    v3 = v.reshape(batch, seq, MLA_HEADS * MLA_V)
    out = pl.pallas_call(
        _mla_attn_kernel,
        grid=(batch, MLA_HEADS // 2, seq // tq),
        in_specs=[pl.BlockSpec((None, tq, 2 * LANES), lambda b, p, i: (b, i, p)),
                  pl.BlockSpec((None, seq, 2 * LANES), lambda b, p, i: (b, 0, p)),
                  pl.BlockSpec((None, seq, LANES), lambda b, p, i: (b, 0, p))],
        out_specs=pl.BlockSpec((None, tq, LANES), lambda b, p, i: (b, i, p)),
        out_shape=jax.ShapeDtypeStruct((batch, seq, MLA_HEADS * MLA_V), BF16),
        compiler_params=_params("parallel", "parallel", "arbitrary"),
        name="mla_attn",
    )(q3, k3, v3)
    return out.reshape(batch * seq, MLA_HEADS * MLA_V)


HG_LEVELS = int(math.log2(HG_CHUNK))
HG_UNROLL = 1


def _hg_constants(reverse):
    c = HG_CHUNK
    t = np.arange(c)[:, None]
    u = np.arange(c)[None, :]
    if not reverse:
        incl = u <= t
        rest = u > t
    else:
        incl = u >= t
        rest = u < t
    mats = [incl, rest]
    masks = []
    roles = []
    for lvl in range(1, HG_LEVELS + 1):
        size = 1 << lvl
        start = (t // size) * size
        mid = start + size // 2
        upper = t >= mid
        if not reverse:
            q_side = (u >= mid) & (u <= t)
            k_side = (u > t) & (u <= mid - 1)
            is_query = upper
        else:
            q_side = (u >= t) & (u < mid)
            k_side = (u >= mid) & (u < t)
            is_query = ~upper
        mats.append(np.where(is_query, q_side, k_side))
        same = (t // size) == (u // size)
        key_row = (~is_query).T
        masks.append(same & is_query & np.broadcast_to(key_row, (c, c)))
        roles.append(np.broadcast_to(is_query, (c, LANES)))
    w = np.concatenate(mats, axis=0).astype(np.float32)
    return (jnp.asarray(w, BF16), jnp.asarray(np.stack(masks).astype(np.float32)),
            jnp.asarray(np.stack(roles).astype(np.float32)))


def _hg_chunk(q, zf, v_b, lb, w_ref, m_ref, r_ref, st, total_row):
    c = HG_CHUNK
    sg = jax.nn.sigmoid(zf)
    k = (1.0 - lb) * jax.nn.sigmoid(-zf)
    lf = jnp.log(lb + (1.0 - lb) * sg)
    hi = lf.astype(BF16)
    r1 = lf - hi.astype(F32)
    mid = r1.astype(BF16)
    lo = (r1 - mid.astype(F32)).astype(BF16)
    w = w_ref[...]
    ex = _dot(w, hi) + _dot(w, mid) + _dot(w, lo)
    b_incl = ex[0:c]
    total = ex[total_row:total_row + 1]
    qd = q * jnp.exp(b_incl)
    kd = k * jnp.exp(ex[c:2 * c])
    row = lax.broadcasted_iota(jnp.int32, (c, c), 0)
    col = lax.broadcasted_iota(jnp.int32, (c, c), 1)
    a = jnp.where(row == col, jnp.sum(q * k, axis=-1, keepdims=True), 0.0)
    for lvl in range(HG_LEVELS):
        x = jnp.where(r_ref[lvl] > 0.5, q, k) * jnp.exp(ex[(2 + lvl) * c:(3 + lvl) * c])
        xb = x.astype(BF16)
        a = a + m_ref[lvl] * _dot_nt(xb, xb)
    o = _dot(a.astype(BF16), v_b) + _dot_nt(qd.astype(BF16), st.astype(BF16))
    return o, st * jnp.exp(total) + _dot_tn(v_b, kd.astype(BF16))


def _hg_kernel(zq_ref, zff_ref, zfb_ref, zi_ref, zg_ref, lb_ref, g_ref,
               wf_ref, mf_ref, rf_ref, wb_ref, mb_ref, rb_ref,
               o_ref, accf_ref, accb_ref):
    c = HG_CHUNK
    seq = zq_ref.shape[0]
    n = seq // c
    lb_f = lb_ref[0:1, :]
    lb_b = lb_ref[1:2, :]

    def gated(ref, rows):
        z = ref[rows, :]
        return z * jax.nn.sigmoid(z)

    def body(i, carry):
        st_f, st_b = carry
        rf = pl.ds(pl.multiple_of(i * c, c), c)
        rb = pl.ds(pl.multiple_of((n - 1 - i) * c, c), c)
        of, st_f = _hg_chunk(gated(zq_ref, rf), zff_ref[rf, :], zi_ref[rf, :].astype(BF16), lb_f,
                             wf_ref, mf_ref, rf_ref, st_f, c - 1)
        accf_ref[rf, :] = of
        ob, st_b = _hg_chunk(gated(zq_ref, rb), zfb_ref[rb, :], zi_ref[rb, :].astype(BF16), lb_b,
                             wb_ref, mb_ref, rb_ref, st_b, 0)
        accb_ref[rb, :] = ob
        return st_f, st_b

    zero = jnp.zeros((HG_F, HG_F), F32)
    lax.fori_loop(0, n, body, (zero, zero), unroll=HG_UNROLL)

    def finish(i, carry):
        rows = pl.ds(pl.multiple_of(i * c, c), c)
        o = accf_ref[rows, :] + accb_ref[rows, :]
        o = o * lax.rsqrt(jnp.mean(o * o, axis=-1, keepdims=True) + NORM_EPS)
        o_ref[rows, :] = (o * g_ref[...] * gated(zg_ref, rows)).astype(o_ref.dtype)
        return carry

    lax.fori_loop(0, n, finish, 0)


def hgrn2_scan(z, lb, norm_g, batch, seq):
    d = D_MODEL
    z3 = z.reshape(batch, seq, 5 * d)
    wf, mf, rf = _hg_constants(False)
    wb, mb, rb = _hg_constants(True)
    zspec = lambda j: pl.BlockSpec((None, seq, HG_F), lambda b, h: (b, 0, j * HG_HEADS + h))
    full = lambda a: pl.BlockSpec(a.shape, lambda b, h: (0,) * a.ndim)
    out = pl.pallas_call(
        _hg_kernel,
        grid=(batch, HG_HEADS),
        in_specs=[zspec(0), zspec(1), zspec(2), zspec(3), zspec(4),
                  pl.BlockSpec((2, HG_F), lambda b, h: (0, h)),
                  pl.BlockSpec((1, HG_F), lambda b, h: (0, h)),
                  full(wf), full(mf), full(rf), full(wb), full(mb), full(rb)],
        out_specs=pl.BlockSpec((None, seq, HG_F), lambda b, h: (b, 0, h)),
        out_shape=jax.ShapeDtypeStruct((batch, seq, d), BF16),
        scratch_shapes=[pltpu.VMEM((seq, HG_F), F32), pltpu.VMEM((seq, HG_F), F32)],
        compiler_params=_params("parallel", "parallel"),
        name="hgrn2_scan",
    )(z3, z3, z3, z3, z3, lb.astype(F32), norm_g.reshape(1, d).astype(F32),
      wf, mf, rf, wb, mb, rb)
    return out.reshape(batch * seq, d)


def _top_values(s, count, with_rank=False):
    vals = []
    cur = s
    rank = jnp.full(s.shape, float(count), F32)
    for it in range(count):
        m = jnp.max(cur, axis=0, keepdims=True)
        vals.append(m)
        hit = cur >= m
        if with_rank:
            rank = jnp.where(hit, float(it), rank)
        cur = jnp.where(hit, -jnp.inf, cur)
    vals = jnp.concatenate(vals, axis=0)
    return (vals, rank) if with_rank else vals


def _peer_route_kernel(ht_ref, wq_ref, keys_ref, cnt_ref, a0_ref, r1_ref, b1_ref):
    kk = PEER_TOPK
    ht = ht_ref[...]
    for h in range(PEER_HEADS):
        scores = []
        for c in range(2):
            g = 2 * h + c
            qt = _dot(wq_ref[g * LANES:(g + 1) * LANES, :], ht)
            scores.append(_dot(keys_ref[c], qt.astype(BF16)))
        s0, s1 = scores
        sv0 = _top_values(s0, kk)
        sv1, r1 = _top_values(s1, kk, with_rank=True)
        cands = [sv0[a:a + 1] + sv1[0:kk // (a + 1)] for a in range(kk)]
        n_cand = sum(kk // (a + 1) for a in range(kk))
        pad = (-n_cand) % 8
        if pad:
            cands.append(jnp.full((pad, ht.shape[1]), -jnp.inf, F32))
        tau = _top_values(jnp.concatenate(cands, axis=0), kk)[kk - 1:kk]
        e0 = jnp.exp(sv0 - sv0[0:1])
        e1 = jnp.exp(sv1 - sv1[0:1])
        z = jnp.zeros_like(tau)
        for a in range(kk):
            nb = kk // (a + 1)
            sel = (sv0[a:a + 1] + sv1[0:nb]) >= tau
            z = z + e0[a:a + 1] * jnp.sum(jnp.where(sel, e1[0:nb], 0.0), axis=0, keepdims=True)
        cnt = jnp.zeros(s0.shape, F32)
        for b in range(kk):
            cnt = cnt + jnp.where(s0 + sv1[b:b + 1] >= tau, 1.0, 0.0)
        cnt_ref[h] = cnt
        a0_ref[h] = jnp.exp(s0 - sv0[0:1])
        r1_ref[h] = r1.astype(BF16)
        b1_ref[h] = (jnp.exp(s1 - sv1[0:1]) / z).astype(BF16)


def peer_route(ht, wq_t, keys, tt=256):
    d, n = ht.shape
    shape = (PEER_HEADS, PEER_NKEYS, n)
    ospec = pl.BlockSpec((PEER_HEADS, PEER_NKEYS, tt), lambda i: (0, 0, i))
    return pl.pallas_call(
        _peer_route_kernel,
        grid=(n // tt,),
        in_specs=[pl.BlockSpec((d, tt), lambda i: (0, i)),
                  pl.BlockSpec(wq_t.shape, lambda i: (0, 0)),
                  pl.BlockSpec(keys.shape, lambda i: (0, 0, 0))],
        out_specs=[ospec, ospec, ospec, ospec],
        out_shape=[jax.ShapeDtypeStruct(shape, F32), jax.ShapeDtypeStruct(shape, F32),
                   jax.ShapeDtypeStruct(shape, BF16), jax.ShapeDtypeStruct(shape, BF16)],
        compiler_params=_params("parallel"),
        name="peer_route",
    )(ht, wq_t, keys)


def _peer_gate_block(hid_ref, w_ref, cnt_ref, a0_ref, r1_ref, b1_ref):
    tt = hid_ref.shape[1]
    zero = jnp.zeros((PEER_NKEYS, LANES), BF16)

    def row_tile(ref, h, ii, cols):
        row = jnp.broadcast_to(ref[h, ii:ii + 1, cols], (BF16_SUBLANES, LANES)).astype(BF16)
        return pltpu.repeat(row, PEER_NKEYS // BF16_SUBLANES, axis=0)

    for ii in range(PEER_EXPERT_BLOCK // PEER_NKEYS):
        rows = slice(ii * PEER_NKEYS, (ii + 1) * PEER_NKEYS)
        for tc in range(tt // LANES):
            cols = slice(tc * LANES, (tc + 1) * LANES)
            gate = zero
            for h in range(PEER_HEADS):
                cnt = row_tile(cnt_ref, h, ii, cols)
                a0 = row_tile(a0_ref, h, ii, cols)
                gate = gate + jnp.where(r1_ref[h, :, cols] < cnt, a0 * b1_ref[h, :, cols], zero)
            hid = hid_ref[rows, cols]
            act = (0.5 * hid * (1.0 + lax.erf(hid * RSQRT2))).astype(BF16)
            w_ref[rows, cols] = gate * act


def _peer_ffn_kernel(ht_ref, ua_ref, ub_ref, vtp_ref, vtq_ref, cntp_ref, a0p_ref, cnta_ref, a0a_ref,
                     r1_in_ref, b1_in_ref, res_ref, g_ref, b_ref,
                     o_ref, acc_ref, hidp_ref, hida_ref, wp_ref, wq_ref, r1_ref, b1_ref):
    g = pl.program_id(1)

    @pl.when(g == 0)
    def _():
        acc_ref[...] = jnp.zeros_like(acc_ref)
        hidp_ref[...] = jnp.zeros_like(hidp_ref)
        wp_ref[...] = jnp.zeros_like(wp_ref)
        r1_ref[...] = r1_in_ref[...]
        b1_ref[...] = b1_in_ref[...]

    ht = ht_ref[...]
    acc_ref[...] += _dot(vtp_ref[...], wp_ref[...])
    _peer_gate_block(hidp_ref, wq_ref, cntp_ref, a0p_ref, r1_ref, b1_ref)
    hida_ref[...] = _dot(ua_ref[...], ht)

    acc_ref[...] += _dot(vtq_ref[...], wq_ref[...])
    _peer_gate_block(hida_ref, wp_ref, cnta_ref, a0a_ref, r1_ref, b1_ref)
    hidp_ref[...] = _dot(ub_ref[...], ht)

    @pl.when(g == pl.num_programs(1) - 1)
    def _():
        y = acc_ref[...].T
        o_ref[...] = _layer_norm_rows(DN_ALPHA * res_ref[...] + y, g_ref[...], b_ref[...])


def peer_ffn(h, ht, route, u, v_blocks, g, b, tt=512):
    n, d = h.shape
    eb = PEER_EXPERT_BLOCK
    nb = u.shape[0] // eb
    last = nb - 1
    rspec = pl.BlockSpec((PEER_HEADS, PEER_NKEYS, tt), lambda i, s: (0, 0, i))
    rows_per_block = eb // PEER_NKEYS
    rowspec = lambda blk: pl.BlockSpec((PEER_HEADS, rows_per_block, tt),
                                       lambda i, s: (0, blk(s), i))
    prev_b = lambda s: jnp.maximum(2 * s - 1, 0)
    this_a = lambda s: jnp.minimum(2 * s, last)
    cnt, a0, r1, b1 = route
    return pl.pallas_call(
        _peer_ffn_kernel,
        grid=(n // tt, nb // 2 + 1),
        in_specs=[pl.BlockSpec((d, tt), lambda i, s: (0, i)),
                  pl.BlockSpec((eb, d), lambda i, s: (this_a(s), 0)),
                  pl.BlockSpec((eb, d), lambda i, s: (jnp.minimum(2 * s + 1, last), 0)),
                  pl.BlockSpec((None, d, eb), lambda i, s: (jnp.maximum(2 * s - 2, 0), 0, 0)),
                  pl.BlockSpec((None, d, eb), lambda i, s: (prev_b(s), 0, 0)),
                  rowspec(prev_b), rowspec(prev_b), rowspec(this_a), rowspec(this_a),
                  rspec, rspec,
                  pl.BlockSpec((tt, d), lambda i, s: (i, 0)),
                  pl.BlockSpec((1, d), lambda i, s: (0, 0)),
                  pl.BlockSpec((1, d), lambda i, s: (0, 0))],
        out_specs=pl.BlockSpec((tt, d), lambda i, s: (i, 0)),
        out_shape=jax.ShapeDtypeStruct((n, d), F32),
        scratch_shapes=[pltpu.VMEM((d, tt), F32),
                        pltpu.VMEM((eb, tt), F32), pltpu.VMEM((eb, tt), F32),
                        pltpu.VMEM((eb, tt), BF16), pltpu.VMEM((eb, tt), BF16),
                        pltpu.VMEM((PEER_HEADS, PEER_NKEYS, tt), BF16),
                        pltpu.VMEM((PEER_HEADS, PEER_NKEYS, tt), BF16)],
        compiler_params=_params("parallel", "arbitrary"),
        name="peer_ffn",
    )(ht, u, u, v_blocks, v_blocks, cnt, a0, cnt, a0, r1, b1, h, g.reshape(1, d), b.reshape(1, d))


def kernel(x, na_w_in, na_rel_bias, na_w_out, mla_w_in, mla_q_norm, mla_kv_norm, mla_w_q_up, mla_w_kv_up, mla_w_out, hg_w_in, hg_lower_bound, hg_norm, hg_w_out, peer_w_q, peer_sub_keys, peer_u, peer_v, ln_mix_g, ln_mix_b, ln_ffn_g, ln_ffn_b):
    batch, seq, d = x.shape
    rows = seq // GRID_W
    lb_w = jax.nn.softmax(hg_lower_bound.astype(F32), axis=0)
    lb_all = jnp.cumsum(lb_w, axis=0) - lb_w[0:1]
    h = x.reshape(batch * seq, d)
    for layer in range(DEPTH):
        kind = layer % N_MIXERS
        j = layer // N_MIXERS
        if kind == 0:
            qkv = matmul(h, na_w_in[j].astype(BF16), BF16)
            mix_in = neighborhood_attention(qkv, na_rel_bias[j], batch, rows)
            w_out = na_w_out[j]
        elif kind == 1:
            q, k, v = mla_projections(h, mla_w_in[j], mla_q_norm[j], mla_kv_norm[j],
                                      mla_w_q_up[j], mla_w_kv_up[j], seq)
            mix_in = mla_attention(q, k, v, batch, seq)
            w_out = mla_w_out[j]
        else:
            z = matmul(h, hg_w_in[j].astype(BF16), F32)
            mix_in = hgrn2_scan(z, lb_all[layer], hg_norm[j], batch, seq)
            w_out = hg_w_out[j]
        h, ht = matmul_res_ln(mix_in, w_out.astype(BF16), h, ln_mix_g[layer], ln_mix_b[layer])
        route = peer_route(ht, peer_w_q[layer].T.astype(BF16), peer_sub_keys[layer].astype(BF16))
        v_blocks = peer_v[layer].astype(BF16).reshape(-1, PEER_EXPERT_BLOCK, d).transpose(0, 2, 1)
        h = peer_ffn(h, ht, route, peer_u[layer].astype(BF16), v_blocks,
                     ln_ffn_g[layer], ln_ffn_b[layer])
    return h.reshape(batch, seq, d)
```

```python
import functools
import math

import jax
import jax.numpy as jnp
import numpy as np
from jax import lax
from jax.experimental import pallas as pl
from jax.experimental.pallas import tpu as pltpu

F32 = jnp.float32
BF16 = jnp.bfloat16

D_MODEL = 1024
DEPTH = 4
GRID_W = 64
N_MIXERS = 3

NA_HEADS = 16
NA_HEAD_DIM = 64
NA_KH = 8
NA_KW = 16

MLA_HEADS = 16
MLA_NOPE = 64
MLA_ROPE = 32
MLA_V = 64
MLA_Q_RANK = 256
MLA_KV_RANK = 256
ROPE_THETA = 10000.0

HG_HEADS = 8
HG_F = 128
HG_CHUNK = 128

PEER_HEADS = 8
PEER_NKEYS = 128
PEER_TOPK = 16
PEER_EXPERT_BLOCK = 1024
PEER_GATE_ROWS = 4

NORM_EPS = 1e-5
DN_ALPHA = (2.0 * DEPTH) ** 0.25

LANES = 128
BF16_SUBLANES = 16
VMEM_LIMIT = 48 * 1024 * 1024
MASK_NEG = -1e30
RSQRT2 = 0.7071067811865476


def _params(*sem):
    return pltpu.CompilerParams(dimension_semantics=sem, vmem_limit_bytes=VMEM_LIMIT)


def _layer_norm_rows(z, g, b):
    mu = jnp.mean(z, axis=-1, keepdims=True)
    zc = z - mu
    var = jnp.mean(zc * zc, axis=-1, keepdims=True)
    return zc * lax.rsqrt(var + NORM_EPS) * g + b


def _dot(a, b):
    return jnp.dot(a, b, preferred_element_type=F32)


def _dot_nt(a, b):
    return lax.dot_general(a, b, (((1,), (1,)), ((), ())), preferred_element_type=F32)


def _dot_tn(a, b):
    return lax.dot_general(a, b, (((0,), (0,)), ((), ())), preferred_element_type=F32)


def _mm_kernel(a_ref, w_ref, o_ref):
    o_ref[...] = _dot(a_ref[...].astype(BF16), w_ref[...]).astype(o_ref.dtype)


def matmul(a, w, out_dtype, tm=512, tn=1024):
    m, k = a.shape
    n = w.shape[1]
    tn = min(tn, n)
    return pl.pallas_call(
        _mm_kernel,
        grid=(m // tm, n // tn),
        in_specs=[pl.BlockSpec((tm, k), lambda i, j: (i, 0)),
                  pl.BlockSpec((k, tn), lambda i, j: (0, j))],
        out_specs=pl.BlockSpec((tm, tn), lambda i, j: (i, j)),
        out_shape=jax.ShapeDtypeStruct((m, n), out_dtype),
        compiler_params=_params("parallel", "parallel"),
        name="proj",
    )(a, w)


def _mm_res_ln_kernel(a_ref, w_ref, res_ref, g_ref, b_ref, o_ref, ot_ref):
    y = _dot(a_ref[...], w_ref[...])
    out = _layer_norm_rows(DN_ALPHA * res_ref[...] + y, g_ref[...], b_ref[...])
    o_ref[...] = out
    ot_ref[...] = out.T.astype(BF16)


def matmul_res_ln(a, w, res, g, b, tm=256):
    m, k = a.shape
    n = w.shape[1]
    return pl.pallas_call(
        _mm_res_ln_kernel,
        grid=(m // tm,),
        in_specs=[pl.BlockSpec((tm, k), lambda i: (i, 0)),
                  pl.BlockSpec((k, n), lambda i: (0, 0)),
                  pl.BlockSpec((tm, n), lambda i: (i, 0)),
                  pl.BlockSpec((1, n), lambda i: (0, 0)),
                  pl.BlockSpec((1, n), lambda i: (0, 0))],
        out_specs=[pl.BlockSpec((tm, n), lambda i: (i, 0)),
                   pl.BlockSpec((n, tm), lambda i: (0, i))],
        out_shape=[jax.ShapeDtypeStruct((m, n), F32),
                   jax.ShapeDtypeStruct((n, m), BF16)],
        compiler_params=_params("parallel"),
        name="out_proj_ln",
    )(a, w, res, g.reshape(1, n), b.reshape(1, n))


def _na_row_start(r, rows):
    return jnp.clip(r - NA_KH // 2, 0, rows - NA_KH)


def _na_proj_kernel(h_ref, wqk_ref, wvt_ref, qk_ref, vt_ref):
    hb = h_ref[...].astype(BF16)
    qk_ref[...] = _dot(hb, wqk_ref[...]).astype(qk_ref.dtype)
    vt_ref[...] = _dot_nt(wvt_ref[...], hb).astype(vt_ref.dtype)


def na_project(h, w_in, tm=512):
    n, d = h.shape
    wqk = w_in[:, :2 * d].astype(BF16)
    wvt = w_in[:, 2 * d:].T.astype(BF16)
    return pl.pallas_call(
        _na_proj_kernel,
        grid=(n // tm,),
        in_specs=[pl.BlockSpec((tm, d), lambda i: (i, 0)),
                  pl.BlockSpec((d, 2 * d), lambda i: (0, 0)),
                  pl.BlockSpec((d, d), lambda i: (0, 0))],
        out_specs=[pl.BlockSpec((tm, 2 * d), lambda i: (i, 0)),
                   pl.BlockSpec((d, tm), lambda i: (0, i))],
        out_shape=[jax.ShapeDtypeStruct((n, 2 * d), BF16), jax.ShapeDtypeStruct((d, n), BF16)],
        compiler_params=_params("parallel"),
        name="na_proj",
    )(h, wqk, wvt)


NA_QROWS = 2
NA_KROWS = NA_KH + NA_QROWS
NA_EDGE = NA_KH // 4


def _na_key_start(i, rows):
    return jnp.clip(i - NA_KH // 4, 0, (rows - NA_KROWS) // 2)


def _na_kernel(q_ref, k_ref, vt_ref, bias_ref, o_ref):
    n_keys = NA_KROWS * GRID_W
    n_q = NA_QROWS * GRID_W
    lane = lax.broadcasted_iota(jnp.int32, (n_q, LANES), 1)
    low = lane < NA_HEAD_DIM
    row_low = lax.broadcasted_iota(jnp.int32, (LANES, n_q), 0) < NA_HEAD_DIM
    for p in range(NA_HEADS // 2):
        cols = slice(p * LANES, (p + 1) * LANES)
        qp = q_ref[0, :, :, cols].reshape(n_q, LANES)
        zq = jnp.zeros_like(qp)
        kp = k_ref[0, :, :, cols].reshape(n_keys, LANES)
        vtp = vt_ref[cols, :]
        outs = []
        for half in range(2):
            qh = jnp.where(low, qp, zq) if half == 0 else jnp.where(low, zq, qp)
            s = _dot_nt(kp, qh) * (NA_HEAD_DIM ** -0.5) + bias_ref[0, 2 * p + half]
            m = jnp.max(s, axis=0, keepdims=True)
            e = jnp.exp(s - m)
            l = jnp.sum(e, axis=0, keepdims=True)
            outs.append(_dot(vtp, e.astype(BF16)) / l)
        o2 = jnp.where(row_low, outs[0], outs[1]).T.astype(o_ref.dtype)
        for a in range(NA_QROWS):
            o_ref[a, :, cols] = o2[a * GRID_W:(a + 1) * GRID_W]


def _na_bias_table(rel_bias, rows):
    cols = np.arange(GRID_W)
    c0 = np.clip(cols - NA_KW // 2, 0, GRID_W - NA_KW)
    kc = np.arange(GRID_W)
    inside = (kc[None, :] >= c0[:, None]) & (kc[None, :] < c0[:, None] + NA_KW)
    dc = np.clip(kc[None, :] - cols[:, None] + (NA_KW - 1), 0, 2 * NA_KW - 2)
    b2 = jnp.where(inside[None, None], rel_bias[:, :, dc].astype(F32), MASK_NEG)
    steps = rows // NA_QROWS
    variant_steps = list(range(NA_EDGE)) + [NA_EDGE] + list(range(steps - NA_EDGE, steps))
    dr = np.zeros((len(variant_steps), NA_QROWS, NA_KROWS), np.int32)
    ok = np.zeros(dr.shape, bool)
    for v, i in enumerate(variant_steps):
        key0 = 2 * int(np.clip(i - NA_KH // 4, 0, (rows - NA_KROWS) // 2))
        for a in range(NA_QROWS):
            r = NA_QROWS * i + a
            r0 = int(np.clip(r - NA_KH // 2, 0, rows - NA_KH))
            key_rows = key0 + np.arange(NA_KROWS)
            ok[v, a] = (key_rows >= r0) & (key_rows < r0 + NA_KH)
            dr[v, a] = np.clip(key_rows - r + NA_KH - 1, 0, 2 * NA_KH - 2)
    t = b2[:, dr]
    t = jnp.where(ok[None, :, :, :, None, None], t, MASK_NEG)
    t = t.transpose(1, 0, 3, 5, 2, 4)
    return t.reshape(len(variant_steps), NA_HEADS, NA_KROWS * GRID_W, NA_QROWS * GRID_W)


def neighborhood_attention(qk, vt, rel_bias, batch, rows):
    d = D_MODEL
    steps = rows // NA_QROWS
    n_keys = NA_KROWS * GRID_W
    qk4 = qk.reshape(batch, rows, GRID_W, 2 * d)
    bias = _na_bias_table(rel_bias, rows)
    el = pl.Element
    key0 = lambda i: _na_key_start(i, rows)

    def variant(i):
        return jnp.where(i < NA_EDGE, i, jnp.where(i >= steps - NA_EDGE, i - (steps - 2 * NA_EDGE - 1), NA_EDGE))

    out = pl.pallas_call(
        _na_kernel,
        grid=(batch, steps),
        in_specs=[pl.BlockSpec((el(1), el(NA_QROWS), el(GRID_W), el(d)),
                               lambda b, i: (b, NA_QROWS * i, 0, 0)),
                  pl.BlockSpec((el(1), el(NA_KROWS), el(GRID_W), el(d)),
                               lambda b, i: (b, 2 * key0(i), 0, d)),
                  pl.BlockSpec((el(d), el(n_keys)),
                               lambda b, i: (0, (b * (rows // 2) + key0(i)) * (2 * GRID_W))),
                  pl.BlockSpec((1, NA_HEADS, n_keys, NA_QROWS * GRID_W),
                               lambda b, i: (variant(i), 0, 0, 0))],
        out_specs=pl.BlockSpec((None, NA_QROWS, GRID_W, d), lambda b, i: (b, i, 0, 0)),
        out_shape=jax.ShapeDtypeStruct((batch, rows, GRID_W, d), BF16),
        compiler_params=_params("parallel", "arbitrary"),
        name="na_attn",
    )(qk4, qk4, vt, bias)
    return out.reshape(batch * rows * GRID_W, d)


def _rms_rows(x, g):
    return x * lax.rsqrt(jnp.mean(x * x, axis=-1, keepdims=True) + NORM_EPS) * g


def _mla_proj_kernel(x_ref, win_ref, qn_ref, kvn_ref, wqa_ref, wqb_ref, wkv_ref, cos_ref, sin_ref,
                     q_ref, k_ref, v_ref):
    hd = MLA_HEADS * LANES
    hin = _dot(x_ref[...].astype(BF16), win_ref[...])
    cq = _rms_rows(hin[:, :MLA_Q_RANK], qn_ref[...]).astype(BF16)
    ckv = _rms_rows(hin[:, MLA_Q_RANK:MLA_Q_RANK + MLA_KV_RANK], kvn_ref[...]).astype(BF16)
    cos = cos_ref[...]
    sin = sin_ref[...]
    cos_t = jnp.tile(cos, (1, MLA_HEADS))
    sin_t = jnp.tile(sin, (1, MLA_HEADS))
    q = _dot(cq, wqa_ref[...]) * cos_t + _dot(cq, wqb_ref[...]) * sin_t
    q_ref[...] = (q * ((MLA_NOPE + MLA_ROPE) ** -0.5)).astype(q_ref.dtype)
    kv = _dot(ckv, wkv_ref[...])
    base = MLA_Q_RANK + MLA_KV_RANK
    kpe = hin[:, base:base + LANES] * cos + hin[:, base + LANES:base + 2 * LANES] * sin
    k_ref[...] = (kv[:, :hd] + jnp.tile(kpe, (1, MLA_HEADS))).astype(k_ref.dtype)
    v_ref[...] = kv[:, hd:].astype(v_ref.dtype)


def _mla_weights(w_in, w_q_up, w_kv_up):
    r = MLA_ROPE
    half = r // 2
    dq = MLA_NOPE + r
    nh = MLA_HEADS
    base = MLA_Q_RANK + MLA_KV_RANK
    kpe = w_in[:, base:base + r]
    zpad = lambda w, lo, hi: jnp.pad(w, ((0, 0), (lo, hi)))
    swap = lambda w: jnp.concatenate([-w[..., half:], w[..., :half]], axis=-1)
    kpe_a = zpad(kpe, MLA_NOPE, LANES - MLA_NOPE - r)
    kpe_b = zpad(swap(kpe), MLA_NOPE, LANES - MLA_NOPE - r)
    win = jnp.concatenate([w_in[:, :base], kpe_a, kpe_b], axis=1)
    wq = w_q_up.reshape(MLA_Q_RANK, nh, dq)
    pad3 = lambda w, lo, hi: jnp.pad(w, ((0, 0), (0, 0), (lo, hi)))
    wqa = pad3(wq, 0, LANES - dq).reshape(MLA_Q_RANK, nh * LANES)
    wqb = pad3(swap(wq[:, :, MLA_NOPE:]), MLA_NOPE, LANES - dq).reshape(MLA_Q_RANK, nh * LANES)
    wkv = w_kv_up.reshape(MLA_KV_RANK, nh, MLA_NOPE + MLA_V)
    wk = pad3(wkv[:, :, :MLA_NOPE], 0, LANES - MLA_NOPE).reshape(MLA_KV_RANK, nh * LANES)
    wv = wkv[:, :, MLA_NOPE:].reshape(MLA_KV_RANK, nh * MLA_V)
    return (win.astype(BF16), wqa.astype(BF16), wqb.astype(BF16),
            jnp.concatenate([wk, wv], axis=1).astype(BF16))


def _rope_tables(seq):
    half = MLA_ROPE // 2
    inv_freq = ROPE_THETA ** (-jnp.arange(half, dtype=F32) * 2.0 / MLA_ROPE)
    ang = jnp.arange(seq, dtype=F32)[:, None] * inv_freq[None, :]
    cos = jnp.cos(ang)
    sin = jnp.sin(ang)
    tail = LANES - MLA_NOPE - MLA_ROPE
    cos_p = jnp.concatenate([jnp.ones((seq, MLA_NOPE), F32), cos, cos, jnp.zeros((seq, tail), F32)], axis=1)
    sin_p = jnp.concatenate([jnp.zeros((seq, MLA_NOPE), F32), sin, sin, jnp.zeros((seq, tail), F32)], axis=1)
    return cos_p, sin_p


def mla_projections(h, w_in, q_norm, kv_norm, w_q_up, w_kv_up, seq, tm=256):
    n, d = h.shape
    win, wqa, wqb, wkv = _mla_weights(w_in, w_q_up, w_kv_up)
    cos_p, sin_p = _rope_tables(seq)
    hd = MLA_HEADS * LANES
    per_seq = seq // tm
    full = lambda a: pl.BlockSpec(a.shape, lambda i: (0,) * a.ndim)
    qn = q_norm.reshape(1, -1).astype(F32)
    kvn = kv_norm.reshape(1, -1).astype(F32)
    return pl.pallas_call(
        _mla_proj_kernel,
        grid=(n // tm,),
        in_specs=[pl.BlockSpec((tm, d), lambda i: (i, 0)), full(win), full(qn), full(kvn),
                  full(wqa), full(wqb), full(wkv),
                  pl.BlockSpec((tm, LANES), lambda i: (i % per_seq, 0)),
                  pl.BlockSpec((tm, LANES), lambda i: (i % per_seq, 0))],
        out_specs=[pl.BlockSpec((tm, hd), lambda i: (i, 0)),
                   pl.BlockSpec((tm, hd), lambda i: (i, 0)),
                   pl.BlockSpec((tm, MLA_HEADS * MLA_V), lambda i: (i, 0))],
        out_shape=[jax.ShapeDtypeStruct((n, hd), BF16),
                   jax.ShapeDtypeStruct((n, hd), BF16),
                   jax.ShapeDtypeStruct((n, MLA_HEADS * MLA_V), BF16)],
        compiler_params=_params("parallel"),
        name="mla_proj",
    )(h, win, qn, kvn, wqa, wqb, wkv, cos_p, sin_p)


def _mla_attn_kernel(q_ref, k_ref, v_ref, o_ref):
    v = v_ref[...]
    lane = lax.broadcasted_iota(jnp.int32, o_ref.shape, 1)
    halves = []
    for half in range(2):
        cols = slice(half * LANES, (half + 1) * LANES)
        s = _dot_nt(q_ref[:, cols], k_ref[:, cols])
        m = jnp.max(s, axis=-1, keepdims=True)
        e = jnp.exp(s - m)
        l = jnp.sum(e, axis=-1, keepdims=True)
        halves.append(_dot(e.astype(BF16), v) / l)
    o_ref[...] = jnp.where(lane < MLA_V, halves[0], halves[1]).astype(o_ref.dtype)


def mla_attention(q, k, v, batch, seq, tq=256):
    hd = MLA_HEADS * LANES
    q3 = q.reshape(batch, seq, hd)
    k3 = k.reshape(batch, seq, hd)
    v3 = v.reshape(batch, seq, MLA_HEADS * MLA_V)
    out = pl.pallas_call(
        _mla_attn_kernel,
        grid=(batch, MLA_HEADS // 2, seq // tq),
        in_specs=[pl.BlockSpec((None, tq, 2 * LANES), lambda b, p, i: (b, i, p)),
                  pl.BlockSpec((None, seq, 2 * LANES), lambda b, p, i: (b, 0, p)),
                  pl.BlockSpec((None, seq, LANES), lambda b, p, i: (b, 0, p))],
        out_specs=pl.BlockSpec((None, tq, LANES), lambda b, p, i: (b, i, p)),
        out_shape=jax.ShapeDtypeStruct((batch, seq, MLA_HEADS * MLA_V), BF16),
        compiler_params=_params("parallel", "parallel", "arbitrary"),
        name="mla_attn",
    )(q3, k3, v3)
    return out.reshape(batch * seq, MLA_HEADS * MLA_V)


HG_LEVELS = int(math.log2(HG_CHUNK))
HG_UNROLL = 1


def _hg_constants(reverse):
    c = HG_CHUNK
    t = np.arange(c)[:, None]
    u = np.arange(c)[None, :]
    if not reverse:
        incl = u <= t
        rest = u > t
    else:
        incl = u >= t
        rest = u < t
    mats = [incl, rest]
    masks = []
    roles = []
    for lvl in range(1, HG_LEVELS + 1):
        size = 1 << lvl
        start = (t // size) * size
        mid = start + size // 2
        upper = t >= mid
        if not reverse:
            q_side = (u >= mid) & (u <= t)
            k_side = (u > t) & (u <= mid - 1)
            is_query = upper
        else:
            q_side = (u >= t) & (u < mid)
            k_side = (u >= mid) & (u < t)
            is_query = ~upper
        mats.append(np.where(is_query, q_side, k_side))
        same = (t // size) == (u // size)
        key_row = (~is_query).T
        masks.append(same & is_query & np.broadcast_to(key_row, (c, c)))
        roles.append(np.broadcast_to(is_query, (c, LANES)))
    w = np.concatenate(mats, axis=0).astype(np.float32)
    return (jnp.asarray(w, BF16), jnp.asarray(np.stack(masks).astype(np.float32)),
            jnp.asarray(np.stack(roles).astype(np.float32)))


def _hg_chunk(q, zf, v_b, lb, w_ref, m_ref, r_ref, st, total_row):
    c = HG_CHUNK
    sg = jax.nn.sigmoid(zf)
    k = (1.0 - lb) * jax.nn.sigmoid(-zf)
    lf = jnp.log(lb + (1.0 - lb) * sg)
    hi = lf.astype(BF16)
    r1 = lf - hi.astype(F32)
    mid = r1.astype(BF16)
    lo = (r1 - mid.astype(F32)).astype(BF16)
    w = w_ref[...]
    ex = _dot(w, hi) + _dot(w, mid) + _dot(w, lo)
    b_incl = ex[0:c]
    total = ex[total_row:total_row + 1]
    qd = q * jnp.exp(b_incl)
    kd = k * jnp.exp(ex[c:2 * c])
    row = lax.broadcasted_iota(jnp.int32, (c, c), 0)
    col = lax.broadcasted_iota(jnp.int32, (c, c), 1)
    a = jnp.where(row == col, jnp.sum(q * k, axis=-1, keepdims=True), 0.0)
    for lvl in range(HG_LEVELS):
        x = jnp.where(r_ref[lvl] > 0.5, q, k) * jnp.exp(ex[(2 + lvl) * c:(3 + lvl) * c])
        xb = x.astype(BF16)
        a = a + m_ref[lvl] * _dot_nt(xb, xb)
    o = _dot(a.astype(BF16), v_b) + _dot_nt(qd.astype(BF16), st.astype(BF16))
    return o, st * jnp.exp(total) + _dot_tn(v_b, kd.astype(BF16))


def _hg_kernel(zq_ref, zff_ref, zfb_ref, zi_ref, zg_ref, lb_ref, g_ref,
               wf_ref, mf_ref, rf_ref, wb_ref, mb_ref, rb_ref,
               o_ref, accf_ref, accb_ref):
    c = HG_CHUNK
    seq = zq_ref.shape[0]
    n = seq // c
    lb_f = lb_ref[0:1, :]
    lb_b = lb_ref[1:2, :]

    def gated(ref, rows):
        z = ref[rows, :]
        return z * jax.nn.sigmoid(z)

    def body(i, carry):
        st_f, st_b = carry
        rf = pl.ds(pl.multiple_of(i * c, c), c)
        rb = pl.ds(pl.multiple_of((n - 1 - i) * c, c), c)
        of, st_f = _hg_chunk(gated(zq_ref, rf), zff_ref[rf, :], zi_ref[rf, :].astype(BF16), lb_f,
                             wf_ref, mf_ref, rf_ref, st_f, c - 1)
        accf_ref[rf, :] = of
        ob, st_b = _hg_chunk(gated(zq_ref, rb), zfb_ref[rb, :], zi_ref[rb, :].astype(BF16), lb_b,
                             wb_ref, mb_ref, rb_ref, st_b, 0)
        accb_ref[rb, :] = ob
        return st_f, st_b

    zero = jnp.zeros((HG_F, HG_F), F32)
    lax.fori_loop(0, n, body, (zero, zero), unroll=HG_UNROLL)

    def finish(i, carry):
        rows = pl.ds(pl.multiple_of(i * c, c), c)
        o = accf_ref[rows, :] + accb_ref[rows, :]
        o = o * lax.rsqrt(jnp.mean(o * o, axis=-1, keepdims=True) + NORM_EPS)
        o_ref[rows, :] = (o * g_ref[...] * gated(zg_ref, rows)).astype(o_ref.dtype)
        return carry

    lax.fori_loop(0, n, finish, 0)


def hgrn2_scan(z, lb, norm_g, batch, seq):
    d = D_MODEL
    z3 = z.reshape(batch, seq, 5 * d)
    wf, mf, rf = _hg_constants(False)
    wb, mb, rb = _hg_constants(True)
    zspec = lambda j: pl.BlockSpec((None, seq, HG_F), lambda b, h: (b, 0, j * HG_HEADS + h))
    full = lambda a: pl.BlockSpec(a.shape, lambda b, h: (0,) * a.ndim)
    out = pl.pallas_call(
        _hg_kernel,
        grid=(batch, HG_HEADS),
        in_specs=[zspec(0), zspec(1), zspec(2), zspec(3), zspec(4),
                  pl.BlockSpec((2, HG_F), lambda b, h: (0, h)),
                  pl.BlockSpec((1, HG_F), lambda b, h: (0, h)),
                  full(wf), full(mf), full(rf), full(wb), full(mb), full(rb)],
        out_specs=pl.BlockSpec((None, seq, HG_F), lambda b, h: (b, 0, h)),
        out_shape=jax.ShapeDtypeStruct((batch, seq, d), BF16),
        scratch_shapes=[pltpu.VMEM((seq, HG_F), F32), pltpu.VMEM((seq, HG_F), F32)],
        compiler_params=_params("parallel", "parallel"),
        name="hgrn2_scan",
    )(z3, z3, z3, z3, z3, lb.astype(F32), norm_g.reshape(1, d).astype(F32),
      wf, mf, rf, wb, mb, rb)
    return out.reshape(batch * seq, d)


def _top_values(s, count, with_rank=False):
    vals = []
    cur = s
    rank = jnp.full(s.shape, float(count), F32)
    for it in range(count):
        m = jnp.max(cur, axis=0, keepdims=True)
        vals.append(m)
        hit = cur >= m
        if with_rank:
            rank = jnp.where(hit, float(it), rank)
        cur = jnp.where(hit, -jnp.inf, cur)
    vals = jnp.concatenate(vals, axis=0)
    return (vals, rank) if with_rank else vals


def _peer_route_kernel(ht_ref, wq_ref, keys_ref, cnt_ref, a0_ref, r1_ref, b1_ref):
    kk = PEER_TOPK
    ht = ht_ref[...]
    for h in range(PEER_HEADS):
        scores = []
        for c in range(2):
            g = 2 * h + c
            qt = _dot(wq_ref[g * LANES:(g + 1) * LANES, :], ht)
            scores.append(_dot(keys_ref[c], qt.astype(BF16)))
        s0, s1 = scores
        sv0 = _top_values(s0, kk)
        sv1, r1 = _top_values(s1, kk, with_rank=True)
        cands = [sv0[a:a + 1] + sv1[0:kk // (a + 1)] for a in range(kk)]
        n_cand = sum(kk // (a + 1) for a in range(kk))
        pad = (-n_cand) % 8
        if pad:
            cands.append(jnp.full((pad, ht.shape[1]), -jnp.inf, F32))
        tau = _top_values(jnp.concatenate(cands, axis=0), kk)[kk - 1:kk]
        e0 = jnp.exp(sv0 - sv0[0:1])
        e1 = jnp.exp(sv1 - sv1[0:1])
        z = jnp.zeros_like(tau)
        for a in range(kk):
            nb = kk // (a + 1)
            sel = (sv0[a:a + 1] + sv1[0:nb]) >= tau
            z = z + e0[a:a + 1] * jnp.sum(jnp.where(sel, e1[0:nb], 0.0), axis=0, keepdims=True)
        cnt = jnp.zeros(s0.shape, F32)
        for a in range(kk):
            pair_ok = (sv0[a:a + 1] + sv1) >= tau
            n_a = jnp.sum(jnp.where(pair_ok, 1.0, 0.0), axis=0, keepdims=True)
            cnt = jnp.where(s0 == sv0[a:a + 1], n_a, cnt)
        cnt_ref[h] = cnt
        a0_ref[h] = jnp.exp(s0 - sv0[0:1])
        r1_ref[h] = r1.astype(BF16)
        b1_ref[h] = (jnp.exp(s1 - sv1[0:1]) * (0.5 / z)).astype(BF16)


def peer_route(ht, wq_t, keys, tt=256):
    d, n = ht.shape
    shape = (PEER_HEADS, PEER_NKEYS, n)
    ospec = pl.BlockSpec((PEER_HEADS, PEER_NKEYS, tt), lambda i: (0, 0, i))
    return pl.pallas_call(
        _peer_route_kernel,
        grid=(n // tt,),
        in_specs=[pl.BlockSpec((d, tt), lambda i: (0, i)),
                  pl.BlockSpec(wq_t.shape, lambda i: (0, 0)),
                  pl.BlockSpec(keys.shape, lambda i: (0, 0, 0))],
        out_specs=[ospec, ospec, ospec, ospec],
        out_shape=[jax.ShapeDtypeStruct(shape, F32), jax.ShapeDtypeStruct(shape, F32),
                   jax.ShapeDtypeStruct(shape, BF16), jax.ShapeDtypeStruct(shape, BF16)],
        compiler_params=_params("parallel"),
        name="peer_route",
    )(ht, wq_t, keys)


def _peer_gate_block(hid_ref, w_ref, cnt_ref, a0_ref, r1_ref, b1_ref):
    tt = hid_ref.shape[1]
    zero = jnp.zeros((PEER_NKEYS, LANES), BF16)

    def row_tile(ref, h, ii, cols):
        row = jnp.broadcast_to(ref[h, ii:ii + 1, cols], (BF16_SUBLANES, LANES)).astype(BF16)
        return pltpu.repeat(row, PEER_NKEYS // BF16_SUBLANES, axis=0)

    group = PEER_GATE_ROWS
    for tc in range(tt // LANES):
        cols = slice(tc * LANES, (tc + 1) * LANES)
        for i0 in range(0, PEER_EXPERT_BLOCK // PEER_NKEYS, group):
            gates = [zero] * group
            for h in range(PEER_HEADS):
                r1 = r1_ref[h, :, cols]
                b1 = b1_ref[h, :, cols]
                for k in range(group):
                    cnt = row_tile(cnt_ref, h, i0 + k, cols)
                    a0 = row_tile(a0_ref, h, i0 + k, cols)
                    gates[k] = gates[k] + jnp.where(r1 < cnt, a0 * b1, zero)
            for k in range(group):
                rows = slice((i0 + k) * PEER_NKEYS, (i0 + k + 1) * PEER_NKEYS)
                hid = hid_ref[rows, cols]
                act = hid.astype(BF16) * (1.0 + lax.erf(hid * RSQRT2)).astype(BF16)
                w_ref[rows, cols] = gates[k] * act


def _peer_ffn_kernel(ht_ref, ua_ref, ub_ref, vtp_ref, vtq_ref, cntp_ref, a0p_ref, cnta_ref, a0a_ref,
                     r1_in_ref, b1_in_ref, res_ref, g_ref, b_ref,
                     o_ref, acc_ref, hidp_ref, hida_ref, wp_ref, wq_ref, r1_ref, b1_ref):
    g = pl.program_id(1)

    @pl.when(g == 0)
    def _():
        acc_ref[...] = jnp.zeros_like(acc_ref)
        hidp_ref[...] = jnp.zeros_like(hidp_ref)
        wp_ref[...] = jnp.zeros_like(wp_ref)
        r1_ref[...] = r1_in_ref[...]
        b1_ref[...] = b1_in_ref[...]

    ht = ht_ref[...]
    acc_ref[...] += _dot(vtp_ref[...], wp_ref[...])
    _peer_gate_block(hidp_ref, wq_ref, cntp_ref, a0p_ref, r1_ref, b1_ref)
    hida_ref[...] = _dot(ua_ref[...], ht)

    acc_ref[...] += _dot(vtq_ref[...], wq_ref[...])
    _peer_gate_block(hida_ref, wp_ref, cnta_ref, a0a_ref, r1_ref, b1_ref)
    hidp_ref[...] = _dot(ub_ref[...], ht)

    @pl.when(g == pl.num_programs(1) - 1)
    def _():
        y = acc_ref[...].T
        o_ref[...] = _layer_norm_rows(DN_ALPHA * res_ref[...] + y, g_ref[...], b_ref[...])


def peer_ffn(h, ht, route, u, v_blocks, g, b, tt=512):
    n, d = h.shape
    eb = PEER_EXPERT_BLOCK
    nb = u.shape[0] // eb
    last = nb - 1
    rspec = pl.BlockSpec((PEER_HEADS, PEER_NKEYS, tt), lambda i, s: (0, 0, i))
    rows_per_block = eb // PEER_NKEYS
    rowspec = lambda blk: pl.BlockSpec((PEER_HEADS, rows_per_block, tt),
                                       lambda i, s: (0, blk(s), i))
    prev_b = lambda s: jnp.maximum(2 * s - 1, 0)
    this_a = lambda s: jnp.minimum(2 * s, last)
    cnt, a0, r1, b1 = route
    return pl.pallas_call(
        _peer_ffn_kernel,
        grid=(n // tt, nb // 2 + 1),
        in_specs=[pl.BlockSpec((d, tt), lambda i, s: (0, i)),
                  pl.BlockSpec((eb, d), lambda i, s: (this_a(s), 0)),
                  pl.BlockSpec((eb, d), lambda i, s: (jnp.minimum(2 * s + 1, last), 0)),
                  pl.BlockSpec((None, d, eb), lambda i, s: (jnp.maximum(2 * s - 2, 0), 0, 0)),
                  pl.BlockSpec((None, d, eb), lambda i, s: (prev_b(s), 0, 0)),
                  rowspec(prev_b), rowspec(prev_b), rowspec(this_a), rowspec(this_a),
                  rspec, rspec,
                  pl.BlockSpec((tt, d), lambda i, s: (i, 0)),
                  pl.BlockSpec((1, d), lambda i, s: (0, 0)),
                  pl.BlockSpec((1, d), lambda i, s: (0, 0))],
        out_specs=pl.BlockSpec((tt, d), lambda i, s: (i, 0)),
        out_shape=jax.ShapeDtypeStruct((n, d), F32),
        scratch_shapes=[pltpu.VMEM((d, tt), F32),
                        pltpu.VMEM((eb, tt), F32), pltpu.VMEM((eb, tt), F32),
                        pltpu.VMEM((eb, tt), BF16), pltpu.VMEM((eb, tt), BF16),
                        pltpu.VMEM((PEER_HEADS, PEER_NKEYS, tt), BF16),
                        pltpu.VMEM((PEER_HEADS, PEER_NKEYS, tt), BF16)],
        compiler_params=_params("parallel", "arbitrary"),
        name="peer_ffn",
    )(ht, u, u, v_blocks, v_blocks, cnt, a0, cnt, a0, r1, b1, h, g.reshape(1, d), b.reshape(1, d))


def kernel(x, na_w_in, na_rel_bias, na_w_out, mla_w_in, mla_q_norm, mla_kv_norm, mla_w_q_up, mla_w_kv_up, mla_w_out, hg_w_in, hg_lower_bound, hg_norm, hg_w_out, peer_w_q, peer_sub_keys, peer_u, peer_v, ln_mix_g, ln_mix_b, ln_ffn_g, ln_ffn_b):
    batch, seq, d = x.shape
    rows = seq // GRID_W
    lb_w = jax.nn.softmax(hg_lower_bound.astype(F32), axis=0)
    lb_all = jnp.cumsum(lb_w, axis=0) - lb_w[0:1]
    h = x.reshape(batch * seq, d)
    for layer in range(DEPTH):
        kind = layer % N_MIXERS
        j = layer // N_MIXERS
        if kind == 0:
            qk, vt = na_project(h, na_w_in[j])
            mix_in = neighborhood_attention(qk, vt, na_rel_bias[j], batch, rows)
            w_out = na_w_out[j]
        elif kind == 1:
            q, k, v = mla_projections(h, mla_w_in[j], mla_q_norm[j], mla_kv_norm[j],
                                      mla_w_q_up[j], mla_w_kv_up[j], seq)
            mix_in = mla_attention(q, k, v, batch, seq)
            w_out = mla_w_out[j]
        else:
            z = matmul(h, hg_w_in[j].astype(BF16), F32)
            mix_in = hgrn2_scan(z, lb_all[layer], hg_norm[j], batch, seq)
            w_out = hg_w_out[j]
        h, ht = matmul_res_ln(mix_in, w_out.astype(BF16), h, ln_mix_g[layer], ln_mix_b[layer])
        route = peer_route(ht, peer_w_q[layer].T.astype(BF16), peer_sub_keys[layer].astype(BF16))
        v_blocks = peer_v[layer].astype(BF16).reshape(-1, PEER_EXPERT_BLOCK, d).transpose(0, 2, 1)
        h = peer_ffn(h, ht, route, peer_u[layer].astype(BF16), v_blocks,
                     ln_ffn_g[layer], ln_ffn_b[layer])
    return h.reshape(batch, seq, d)
```

```python
import functools
import math

import jax
import jax.numpy as jnp
import numpy as np
from jax import lax
from jax.experimental import pallas as pl
from jax.experimental.pallas import tpu as pltpu

F32 = jnp.float32
BF16 = jnp.bfloat16

D_MODEL = 1024
DEPTH = 4
GRID_W = 64
N_MIXERS = 3

NA_HEADS = 16
NA_HEAD_DIM = 64
NA_KH = 8
NA_KW = 16

MLA_HEADS = 16
MLA_NOPE = 64
MLA_ROPE = 32
MLA_V = 64
MLA_Q_RANK = 256
MLA_KV_RANK = 256
ROPE_THETA = 10000.0

HG_HEADS = 8
HG_F = 128
HG_CHUNK = 128

PEER_HEADS = 8
PEER_NKEYS = 128
PEER_TOPK = 16
PEER_EXPERT_BLOCK = 1024
PEER_CHUNK_ROWS = 2

NORM_EPS = 1e-5
DN_ALPHA = (2.0 * DEPTH) ** 0.25

LANES = 128
BF16_SUBLANES = 16
VMEM_LIMIT = 48 * 1024 * 1024
MASK_NEG = -1e30
RSQRT2 = 0.7071067811865476


def _params(*sem):
    return pltpu.CompilerParams(dimension_semantics=sem, vmem_limit_bytes=VMEM_LIMIT)


def _layer_norm_rows(z, g, b):
    mu = jnp.mean(z, axis=-1, keepdims=True)
    zc = z - mu
    var = jnp.mean(zc * zc, axis=-1, keepdims=True)
    return zc * lax.rsqrt(var + NORM_EPS) * g + b


def _dot(a, b):
    return jnp.dot(a, b, preferred_element_type=F32)


def _dot_nt(a, b):
    return lax.dot_general(a, b, (((1,), (1,)), ((), ())), preferred_element_type=F32)


def _dot_tn(a, b):
    return lax.dot_general(a, b, (((0,), (0,)), ((), ())), preferred_element_type=F32)


def _mm_kernel(a_ref, w_ref, o_ref):
    o_ref[...] = _dot(a_ref[...].astype(BF16), w_ref[...]).astype(o_ref.dtype)


def matmul(a, w, out_dtype, tm=512, tn=1024):
    m, k = a.shape
    n = w.shape[1]
    tn = min(tn, n)
    return pl.pallas_call(
        _mm_kernel,
        grid=(m // tm, n // tn),
        in_specs=[pl.BlockSpec((tm, k), lambda i, j: (i, 0)),
                  pl.BlockSpec((k, tn), lambda i, j: (0, j))],
        out_specs=pl.BlockSpec((tm, tn), lambda i, j: (i, j)),
        out_shape=jax.ShapeDtypeStruct((m, n), out_dtype),
        compiler_params=_params("parallel", "parallel"),
        name="proj",
    )(a, w)


def _mm_res_ln_kernel(a_ref, w_ref, res_ref, g_ref, b_ref, o_ref, ot_ref):
    y = _dot(a_ref[...], w_ref[...])
    out = _layer_norm_rows(DN_ALPHA * res_ref[...] + y, g_ref[...], b_ref[...])
    o_ref[...] = out
    ot_ref[...] = out.T.astype(BF16)


def matmul_res_ln(a, w, res, g, b, tm=256):
    m, k = a.shape
    n = w.shape[1]
    return pl.pallas_call(
        _mm_res_ln_kernel,
        grid=(m // tm,),
        in_specs=[pl.BlockSpec((tm, k), lambda i: (i, 0)),
                  pl.BlockSpec((k, n), lambda i: (0, 0)),
                  pl.BlockSpec((tm, n), lambda i: (i, 0)),
                  pl.BlockSpec((1, n), lambda i: (0, 0)),
                  pl.BlockSpec((1, n), lambda i: (0, 0))],
        out_specs=[pl.BlockSpec((tm, n), lambda i: (i, 0)),
                   pl.BlockSpec((n, tm), lambda i: (0, i))],
        out_shape=[jax.ShapeDtypeStruct((m, n), F32),
                   jax.ShapeDtypeStruct((n, m), BF16)],
        compiler_params=_params("parallel"),
        name="out_proj_ln",
    )(a, w, res, g.reshape(1, n), b.reshape(1, n))


def _na_row_start(r, rows):
    return jnp.clip(r - NA_KH // 2, 0, rows - NA_KH)


def _na_proj_kernel(h_ref, wqk_ref, wvt_ref, qk_ref, vt_ref):
    hb = h_ref[...].astype(BF16)
    qk_ref[...] = _dot(hb, wqk_ref[...]).astype(qk_ref.dtype)
    vt_ref[...] = _dot_nt(wvt_ref[...], hb).astype(vt_ref.dtype)


def na_project(h, w_in, tm=512):
    n, d = h.shape
    wqk = w_in[:, :2 * d].astype(BF16)
    wvt = w_in[:, 2 * d:].T.astype(BF16)
    return pl.pallas_call(
        _na_proj_kernel,
        grid=(n // tm,),
        in_specs=[pl.BlockSpec((tm, d), lambda i: (i, 0)),
                  pl.BlockSpec((d, 2 * d), lambda i: (0, 0)),
                  pl.BlockSpec((d, d), lambda i: (0, 0))],
        out_specs=[pl.BlockSpec((tm, 2 * d), lambda i: (i, 0)),
                   pl.BlockSpec((d, tm), lambda i: (0, i))],
        out_shape=[jax.ShapeDtypeStruct((n, 2 * d), BF16), jax.ShapeDtypeStruct((d, n), BF16)],
        compiler_params=_params("parallel"),
        name="na_proj",
    )(h, wqk, wvt)


NA_QROWS = 2
NA_KROWS = NA_KH + NA_QROWS
NA_EDGE = NA_KH // 4


def _na_key_start(i, rows):
    return jnp.clip(i - NA_KH // 4, 0, (rows - NA_KROWS) // 2)


def _na_kernel(q_ref, k_ref, vt_ref, bias_ref, o_ref):
    n_keys = NA_KROWS * GRID_W
    n_q = NA_QROWS * GRID_W
    lane = lax.broadcasted_iota(jnp.int32, (n_q, LANES), 1)
    low = lane < NA_HEAD_DIM
    row_low = lax.broadcasted_iota(jnp.int32, (LANES, n_q), 0) < NA_HEAD_DIM
    for p in range(NA_HEADS // 2):
        cols = slice(p * LANES, (p + 1) * LANES)
        qp = q_ref[0, :, :, cols].reshape(n_q, LANES)
        zq = jnp.zeros_like(qp)
        kp = k_ref[0, :, :, cols].reshape(n_keys, LANES)
        vtp = vt_ref[cols, :]
        outs = []
        for half in range(2):
            qh = jnp.where(low, qp, zq) if half == 0 else jnp.where(low, zq, qp)
            s = _dot_nt(kp, qh) * (NA_HEAD_DIM ** -0.5) + bias_ref[0, 2 * p + half]
            m = jnp.max(s, axis=0, keepdims=True)
            e = jnp.exp(s - m)
            l = jnp.sum(e, axis=0, keepdims=True)
            outs.append(_dot(vtp, e.astype(BF16)) / l)
        o2 = jnp.where(row_low, outs[0], outs[1]).T.astype(o_ref.dtype)
        for a in range(NA_QROWS):
            o_ref[a, :, cols] = o2[a * GRID_W:(a + 1) * GRID_W]


def _na_bias_table(rel_bias, rows):
    cols = np.arange(GRID_W)
    c0 = np.clip(cols - NA_KW // 2, 0, GRID_W - NA_KW)
    kc = np.arange(GRID_W)
    inside = (kc[None, :] >= c0[:, None]) & (kc[None, :] < c0[:, None] + NA_KW)
    dc = np.clip(kc[None, :] - cols[:, None] + (NA_KW - 1), 0, 2 * NA_KW - 2)
    b2 = jnp.where(inside[None, None], rel_bias[:, :, dc].astype(F32), MASK_NEG)
    steps = rows // NA_QROWS
    variant_steps = list(range(NA_EDGE)) + [NA_EDGE] + list(range(steps - NA_EDGE, steps))
    dr = np.zeros((len(variant_steps), NA_QROWS, NA_KROWS), np.int32)
    ok = np.zeros(dr.shape, bool)
    for v, i in enumerate(variant_steps):
        key0 = 2 * int(np.clip(i - NA_KH // 4, 0, (rows - NA_KROWS) // 2))
        for a in range(NA_QROWS):
            r = NA_QROWS * i + a
            r0 = int(np.clip(r - NA_KH // 2, 0, rows - NA_KH))
            key_rows = key0 + np.arange(NA_KROWS)
            ok[v, a] = (key_rows >= r0) & (key_rows < r0 + NA_KH)
            dr[v, a] = np.clip(key_rows - r + NA_KH - 1, 0, 2 * NA_KH - 2)
    t = b2[:, dr]
    t = jnp.where(ok[None, :, :, :, None, None], t, MASK_NEG)
    t = t.transpose(1, 0, 3, 5, 2, 4)
    return t.reshape(len(variant_steps), NA_HEADS, NA_KROWS * GRID_W, NA_QROWS * GRID_W)


def neighborhood_attention(qk, vt, rel_bias, batch, rows):
    d = D_MODEL
    steps = rows // NA_QROWS
    n_keys = NA_KROWS * GRID_W
    qk4 = qk.reshape(batch, rows, GRID_W, 2 * d)
    bias = _na_bias_table(rel_bias, rows)
    el = pl.Element
    key0 = lambda i: _na_key_start(i, rows)

    def variant(i):
        return jnp.where(i < NA_EDGE, i, jnp.where(i >= steps - NA_EDGE, i - (steps - 2 * NA_EDGE - 1), NA_EDGE))

    out = pl.pallas_call(
        _na_kernel,
        grid=(batch, steps),
        in_specs=[pl.BlockSpec((el(1), el(NA_QROWS), el(GRID_W), el(d)),
                               lambda b, i: (b, NA_QROWS * i, 0, 0)),
                  pl.BlockSpec((el(1), el(NA_KROWS), el(GRID_W), el(d)),
                               lambda b, i: (b, 2 * key0(i), 0, d)),
                  pl.BlockSpec((el(d), el(n_keys)),
                               lambda b, i: (0, (b * (rows // 2) + key0(i)) * (2 * GRID_W))),
                  pl.BlockSpec((1, NA_HEADS, n_keys, NA_QROWS * GRID_W),
                               lambda b, i: (variant(i), 0, 0, 0))],
        out_specs=pl.BlockSpec((None, NA_QROWS, GRID_W, d), lambda b, i: (b, i, 0, 0)),
        out_shape=jax.ShapeDtypeStruct((batch, rows, GRID_W, d), BF16),
        compiler_params=_params("parallel", "arbitrary"),
        name="na_attn",
    )(qk4, qk4, vt, bias)
    return out.reshape(batch * rows * GRID_W, d)


def _rms_rows(x, g):
    return x * lax.rsqrt(jnp.mean(x * x, axis=-1, keepdims=True) + NORM_EPS) * g


def _mla_proj_kernel(x_ref, win_ref, qn_ref, kvn_ref, wqa_ref, wqb_ref, wkv_ref, cos_ref, sin_ref,
                     q_ref, k_ref, v_ref):
    hd = MLA_HEADS * LANES
    hin = _dot(x_ref[...].astype(BF16), win_ref[...])
    cq = _rms_rows(hin[:, :MLA_Q_RANK], qn_ref[...]).astype(BF16)
    ckv = _rms_rows(hin[:, MLA_Q_RANK:MLA_Q_RANK + MLA_KV_RANK], kvn_ref[...]).astype(BF16)
    cos = cos_ref[...]
    sin = sin_ref[...]
    cos_t = jnp.tile(cos, (1, MLA_HEADS))
    sin_t = jnp.tile(sin, (1, MLA_HEADS))
    q = _dot(cq, wqa_ref[...]) * cos_t + _dot(cq, wqb_ref[...]) * sin_t
    q_ref[...] = (q * ((MLA_NOPE + MLA_ROPE) ** -0.5)).astype(q_ref.dtype)
    kv = _dot(ckv, wkv_ref[...])
    base = MLA_Q_RANK + MLA_KV_RANK
    kpe = hin[:, base:base + LANES] * cos + hin[:, base + LANES:base + 2 * LANES] * sin
    k_ref[...] = (kv[:, :hd] + jnp.tile(kpe, (1, MLA_HEADS))).astype(k_ref.dtype)
    v_ref[...] = kv[:, hd:].astype(v_ref.dtype)


def _mla_weights(w_in, w_q_up, w_kv_up):
    r = MLA_ROPE
    half = r // 2
    dq = MLA_NOPE + r
    nh = MLA_HEADS
    base = MLA_Q_RANK + MLA_KV_RANK
    kpe = w_in[:, base:base + r]
    zpad = lambda w, lo, hi: jnp.pad(w, ((0, 0), (lo, hi)))
    swap = lambda w: jnp.concatenate([-w[..., half:], w[..., :half]], axis=-1)
    kpe_a = zpad(kpe, MLA_NOPE, LANES - MLA_NOPE - r)
    kpe_b = zpad(swap(kpe), MLA_NOPE, LANES - MLA_NOPE - r)
    win = jnp.concatenate([w_in[:, :base], kpe_a, kpe_b], axis=1)
    wq = w_q_up.reshape(MLA_Q_RANK, nh, dq)
    pad3 = lambda w, lo, hi: jnp.pad(w, ((0, 0), (0, 0), (lo, hi)))
    wqa = pad3(wq, 0, LANES - dq).reshape(MLA_Q_RANK, nh * LANES)
    wqb = pad3(swap(wq[:, :, MLA_NOPE:]), MLA_NOPE, LANES - dq).reshape(MLA_Q_RANK, nh * LANES)
    wkv = w_kv_up.reshape(MLA_KV_RANK, nh, MLA_NOPE + MLA_V)
    wk = pad3(wkv[:, :, :MLA_NOPE], 0, LANES - MLA_NOPE).reshape(MLA_KV_RANK, nh * LANES)
    wv = wkv[:, :, MLA_NOPE:].reshape(MLA_KV_RANK, nh * MLA_V)
    return (win.astype(BF16), wqa.astype(BF16), wqb.astype(BF16),
            jnp.concatenate([wk, wv], axis=1).astype(BF16))


def _rope_tables(seq):
    half = MLA_ROPE // 2
    inv_freq = ROPE_THETA ** (-jnp.arange(half, dtype=F32) * 2.0 / MLA_ROPE)
    ang = jnp.arange(seq, dtype=F32)[:, None] * inv_freq[None, :]
    cos = jnp.cos(ang)
    sin = jnp.sin(ang)
    tail = LANES - MLA_NOPE - MLA_ROPE
    cos_p = jnp.concatenate([jnp.ones((seq, MLA_NOPE), F32), cos, cos, jnp.zeros((seq, tail), F32)], axis=1)
    sin_p = jnp.concatenate([jnp.zeros((seq, MLA_NOPE), F32), sin, sin, jnp.zeros((seq, tail), F32)], axis=1)
    return cos_p, sin_p


def mla_projections(h, w_in, q_norm, kv_norm, w_q_up, w_kv_up, seq, tm=256):
    n, d = h.shape
    win, wqa, wqb, wkv = _mla_weights(w_in, w_q_up, w_kv_up)
    cos_p, sin_p = _rope_tables(seq)
    hd = MLA_HEADS * LANES
    per_seq = seq // tm
    full = lambda a: pl.BlockSpec(a.shape, lambda i: (0,) * a.ndim)
    qn = q_norm.reshape(1, -1).astype(F32)
    kvn = kv_norm.reshape(1, -1).astype(F32)
    return pl.pallas_call(
        _mla_proj_kernel,
        grid=(n // tm,),
        in_specs=[pl.BlockSpec((tm, d), lambda i: (i, 0)), full(win), full(qn), full(kvn),
                  full(wqa), full(wqb), full(wkv),
                  pl.BlockSpec((tm, LANES), lambda i: (i % per_seq, 0)),
                  pl.BlockSpec((tm, LANES), lambda i: (i % per_seq, 0))],
        out_specs=[pl.BlockSpec((tm, hd), lambda i: (i, 0)),
                   pl.BlockSpec((tm, hd), lambda i: (i, 0)),
                   pl.BlockSpec((tm, MLA_HEADS * MLA_V), lambda i: (i, 0))],
        out_shape=[jax.ShapeDtypeStruct((n, hd), BF16),
                   jax.ShapeDtypeStruct((n, hd), BF16),
                   jax.ShapeDtypeStruct((n, MLA_HEADS * MLA_V), BF16)],
        compiler_params=_params("parallel"),
        name="mla_proj",
    )(h, win, qn, kvn, wqa, wqb, wkv, cos_p, sin_p)


def _mla_attn_kernel(q_ref, k_ref, v_ref, o_ref):
    v = v_ref[...]
    lane = lax.broadcasted_iota(jnp.int32, o_ref.shape, 1)
    halves = []
    for half in range(2):
        cols = slice(half * LANES, (half + 1) * LANES)
        s = _dot_nt(q_ref[:, cols], k_ref[:, cols])
        m = jnp.max(s, axis=-1, keepdims=True)
        e = jnp.exp(s - m)
        l = jnp.sum(e, axis=-1, keepdims=True)
        halves.append(_dot(e.astype(BF16), v) / l)
    o_ref[...] = jnp.where(lane < MLA_V, halves[0], halves[1]).astype(o_ref.dtype)


def mla_attention(q, k, v, batch, seq, tq=256):
    hd = MLA_HEADS * LANES
    q3 = q.reshape(batch, seq, hd)
    k3 = k.reshape(batch, seq, hd)
    v3 = v.reshape(batch, seq, MLA_HEADS * MLA_V)
    out = pl.pallas_call(
        _mla_attn_kernel,
        grid=(batch, MLA_HEADS // 2, seq // tq),
        in_specs=[pl.BlockSpec((None, tq, 2 * LANES), lambda b, p, i: (b, i, p)),
                  pl.BlockSpec((None, seq, 2 * LANES), lambda b, p, i: (b, 0, p)),
                  pl.BlockSpec((None, seq, LANES), lambda b, p, i: (b, 0, p))],
        out_specs=pl.BlockSpec((None, tq, LANES), lambda b, p, i: (b, i, p)),
        out_shape=jax.ShapeDtypeStruct((batch, seq, MLA_HEADS * MLA_V), BF16),
        compiler_params=_params("parallel", "parallel", "arbitrary"),
        name="mla_attn",
    )(q3, k3, v3)
    return out.reshape(batch * seq, MLA_HEADS * MLA_V)


HG_LEVELS = int(math.log2(HG_CHUNK))
HG_UNROLL = 1


def _hg_constants(reverse):
    c = HG_CHUNK
    t = np.arange(c)[:, None]
    u = np.arange(c)[None, :]
    if not reverse:
        incl = u <= t
        rest = u > t
    else:
        incl = u >= t
        rest = u < t
    mats = [incl, rest]
    masks = []
    roles = []
    for lvl in range(1, HG_LEVELS + 1):
        size = 1 << lvl
        start = (t // size) * size
        mid = start + size // 2
        upper = t >= mid
        if not reverse:
            q_side = (u >= mid) & (u <= t)
            k_side = (u > t) & (u <= mid - 1)
            is_query = upper
        else:
            q_side = (u >= t) & (u < mid)
            k_side = (u >= mid) & (u < t)
            is_query = ~upper
        mats.append(np.where(is_query, q_side, k_side))
        same = (t // size) == (u // size)
        key_row = (~is_query).T
        masks.append(same & is_query & np.broadcast_to(key_row, (c, c)))
        roles.append(np.broadcast_to(is_query, (c, LANES)))
    w = np.concatenate(mats, axis=0).astype(np.float32)
    return (jnp.asarray(w, BF16), jnp.asarray(np.stack(masks).astype(np.float32)),
            jnp.asarray(np.stack(roles).astype(np.float32)))


def _hg_chunk(q, zf, v_b, lb, w_ref, m_ref, r_ref, st, total_row):
    c = HG_CHUNK
    sg = jax.nn.sigmoid(zf)
    k = (1.0 - lb) * jax.nn.sigmoid(-zf)
    lf = jnp.log(lb + (1.0 - lb) * sg)
    hi = lf.astype(BF16)
    r1 = lf - hi.astype(F32)
    mid = r1.astype(BF16)
    lo = (r1 - mid.astype(F32)).astype(BF16)
    f = lf.shape[1]
    ex3 = _dot(w_ref[...], jnp.concatenate([hi, mid, lo], axis=1))
    ex = ex3[:, :f] + ex3[:, f:2 * f] + ex3[:, 2 * f:]
    b_incl = ex[0:c]
    total = ex[total_row:total_row + 1]
    qd = q * jnp.exp(b_incl)
    kd = k * jnp.exp(ex[c:2 * c])
    row = lax.broadcasted_iota(jnp.int32, (c, c), 0)
    col = lax.broadcasted_iota(jnp.int32, (c, c), 1)
    a = jnp.where(row == col, jnp.sum(q * k, axis=-1, keepdims=True), 0.0)
    xs = []
    for lvl in range(HG_LEVELS):
        x = jnp.where(r_ref[lvl] > 0.5, q, k) * jnp.exp(ex[(2 + lvl) * c:(3 + lvl) * c])
        xs.append(x.astype(BF16))
    for lvl in range(0, HG_LEVELS, 2):
        pair = xs[lvl:lvl + 2]
        xb = jnp.concatenate(pair, axis=0)
        g = _dot_nt(xb, xb)
        for j in range(len(pair)):
            a = a + m_ref[lvl + j] * g[j * c:(j + 1) * c, j * c:(j + 1) * c]
    o = _dot(a.astype(BF16), v_b) + _dot_nt(qd.astype(BF16), st.astype(BF16))
    return o, st * jnp.exp(total) + _dot_tn(v_b, kd.astype(BF16))


def _hg_kernel(zq_ref, zff_ref, zfb_ref, zi_ref, zg_ref, lb_ref, g_ref,
               wf_ref, mf_ref, rf_ref, wb_ref, mb_ref, rb_ref,
               o_ref, accf_ref, accb_ref):
    c = HG_CHUNK
    seq = zq_ref.shape[0]
    n = seq // c
    lb_f = lb_ref[0:1, :]
    lb_b = lb_ref[1:2, :]

    def gated(ref, rows):
        z = ref[rows, :]
        return z * jax.nn.sigmoid(z)

    def body(i, carry):
        st_f, st_b = carry
        rf = pl.ds(pl.multiple_of(i * c, c), c)
        rb = pl.ds(pl.multiple_of((n - 1 - i) * c, c), c)
        of, st_f = _hg_chunk(gated(zq_ref, rf), zff_ref[rf, :], zi_ref[rf, :].astype(BF16), lb_f,
                             wf_ref, mf_ref, rf_ref, st_f, c - 1)
        accf_ref[rf, :] = of
        ob, st_b = _hg_chunk(gated(zq_ref, rb), zfb_ref[rb, :], zi_ref[rb, :].astype(BF16), lb_b,
                             wb_ref, mb_ref, rb_ref, st_b, 0)
        accb_ref[rb, :] = ob
        return st_f, st_b

    zero = jnp.zeros((HG_F, HG_F), F32)
    lax.fori_loop(0, n, body, (zero, zero), unroll=HG_UNROLL)

    def finish(i, carry):
        rows = pl.ds(pl.multiple_of(i * c, c), c)
        o = accf_ref[rows, :] + accb_ref[rows, :]
        o = o * lax.rsqrt(jnp.mean(o * o, axis=-1, keepdims=True) + NORM_EPS)
        o_ref[rows, :] = (o * g_ref[...] * gated(zg_ref, rows)).astype(o_ref.dtype)
        return carry

    lax.fori_loop(0, n, finish, 0)


def hgrn2_scan(z, lb, norm_g, batch, seq):
    d = D_MODEL
    z3 = z.reshape(batch, seq, 5 * d)
    wf, mf, rf = _hg_constants(False)
    wb, mb, rb = _hg_constants(True)
    zspec = lambda j: pl.BlockSpec((None, seq, HG_F), lambda b, h: (b, 0, j * HG_HEADS + h))
    full = lambda a: pl.BlockSpec(a.shape, lambda b, h: (0,) * a.ndim)
    out = pl.pallas_call(
        _hg_kernel,
        grid=(batch, HG_HEADS),
        in_specs=[zspec(0), zspec(1), zspec(2), zspec(3), zspec(4),
                  pl.BlockSpec((2, HG_F), lambda b, h: (0, h)),
                  pl.BlockSpec((1, HG_F), lambda b, h: (0, h)),
                  full(wf), full(mf), full(rf), full(wb), full(mb), full(rb)],
        out_specs=pl.BlockSpec((None, seq, HG_F), lambda b, h: (b, 0, h)),
        out_shape=jax.ShapeDtypeStruct((batch, seq, d), BF16),
        scratch_shapes=[pltpu.VMEM((seq, HG_F), F32), pltpu.VMEM((seq, HG_F), F32)],
        compiler_params=_params("parallel", "parallel"),
        name="hgrn2_scan",
    )(z3, z3, z3, z3, z3, lb.astype(F32), norm_g.reshape(1, d).astype(F32),
      wf, mf, rf, wb, mb, rb)
    return out.reshape(batch * seq, d)


def _top_values(s, count, with_rank=False):
    vals = []
    cur = s
    rank = jnp.full(s.shape, float(count), F32)
    for it in range(count):
        m = jnp.max(cur, axis=0, keepdims=True)
        vals.append(m)
        hit = cur >= m
        if with_rank:
            rank = jnp.where(hit, float(it), rank)
        cur = jnp.where(hit, -jnp.inf, cur)
    vals = jnp.concatenate(vals, axis=0)
    return (vals, rank) if with_rank else vals


def _peer_route_kernel(ht_ref, wq_ref, keys_ref, cnt_ref, a0_ref, r1_ref, b1_ref):
    kk = PEER_TOPK
    ht = ht_ref[...]
    for h in range(PEER_HEADS):
        scores = []
        for c in range(2):
            g = 2 * h + c
            qt = _dot(wq_ref[g * LANES:(g + 1) * LANES, :], ht)
            scores.append(_dot(keys_ref[c], qt.astype(BF16)))
        s0, s1 = scores
        sv0 = _top_values(s0, kk)
        sv1, r1 = _top_values(s1, kk, with_rank=True)
        cands = [sv0[a:a + 1] + sv1[0:kk // (a + 1)] for a in range(kk)]
        n_cand = sum(kk // (a + 1) for a in range(kk))
        pad = (-n_cand) % 8
        if pad:
            cands.append(jnp.full((pad, ht.shape[1]), -jnp.inf, F32))
        tau = _top_values(jnp.concatenate(cands, axis=0), kk)[kk - 1:kk]
        e0 = jnp.exp(sv0 - sv0[0:1])
        e1 = jnp.exp(sv1 - sv1[0:1])
        z = jnp.zeros_like(tau)
        for a in range(kk):
            nb = kk // (a + 1)
            sel = (sv0[a:a + 1] + sv1[0:nb]) >= tau
            z = z + e0[a:a + 1] * jnp.sum(jnp.where(sel, e1[0:nb], 0.0), axis=0, keepdims=True)
        cnt = jnp.zeros(s0.shape, F32)
        for a in range(kk):
            pair_ok = (sv0[a:a + 1] + sv1) >= tau
            n_a = jnp.sum(jnp.where(pair_ok, 1.0, 0.0), axis=0, keepdims=True)
            cnt = jnp.where(s0 == sv0[a:a + 1], n_a, cnt)
        cnt_ref[h] = cnt
        a0_ref[h] = jnp.exp(s0 - sv0[0:1])
        r1_ref[h] = r1.astype(BF16)
        b1_ref[h] = (jnp.exp(s1 - sv1[0:1]) * (0.5 / z)).astype(BF16)


def peer_route(ht, wq_t, keys, tt=256):
    d, n = ht.shape
    shape = (PEER_HEADS, PEER_NKEYS, n)
    ospec = pl.BlockSpec((PEER_HEADS, PEER_NKEYS, tt), lambda i: (0, 0, i))
    return pl.pallas_call(
        _peer_route_kernel,
        grid=(n // tt,),
        in_specs=[pl.BlockSpec((d, tt), lambda i: (0, i)),
                  pl.BlockSpec(wq_t.shape, lambda i: (0, 0)),
                  pl.BlockSpec(keys.shape, lambda i: (0, 0, 0))],
        out_specs=[ospec, ospec, ospec, ospec],
        out_shape=[jax.ShapeDtypeStruct(shape, F32), jax.ShapeDtypeStruct(shape, F32),
                   jax.ShapeDtypeStruct(shape, BF16), jax.ShapeDtypeStruct(shape, BF16)],
        compiler_params=_params("parallel"),
        name="peer_route",
    )(ht, wq_t, keys)


def _peer_gate_chunk(hid, i0, cnt_ref, a0_ref, r1_ref, b1_ref):
    tt = hid.shape[1]
    zero = jnp.zeros((PEER_NKEYS, LANES), BF16)

    def row_tile(ref, h, ii, cols):
        row = jnp.broadcast_to(ref[h, ii:ii + 1, cols], (BF16_SUBLANES, LANES)).astype(BF16)
        return pltpu.repeat(row, PEER_NKEYS // BF16_SUBLANES, axis=0)

    out_rows = []
    for k in range(PEER_CHUNK_ROWS):
        tiles = []
        for tc in range(tt // LANES):
            cols = slice(tc * LANES, (tc + 1) * LANES)
            gate = zero
            for h in range(PEER_HEADS):
                cnt = row_tile(cnt_ref, h, i0 + k, cols)
                a0 = row_tile(a0_ref, h, i0 + k, cols)
                gate = gate + jnp.where(r1_ref[h, :, cols] < cnt, a0 * b1_ref[h, :, cols], zero)
            x = hid[k * PEER_NKEYS:(k + 1) * PEER_NKEYS, cols]
            act = x.astype(BF16) * (1.0 + lax.erf(x * RSQRT2)).astype(BF16)
            tiles.append(gate * act)
        out_rows.append(jnp.concatenate(tiles, axis=1))
    return jnp.concatenate(out_rows, axis=0)


def _peer_ffn_kernel(ht_ref, u_ref, vt_ref, cnt_ref, a0_ref, r1_in_ref, b1_in_ref, res_ref, g_ref, b_ref,
                     o_ref, acc_ref, hid_ref, r1_ref, b1_ref):
    e = pl.program_id(1)

    @pl.when(e == 0)
    def _():
        acc_ref[...] = jnp.zeros_like(acc_ref)
        r1_ref[...] = r1_in_ref[...]
        b1_ref[...] = b1_in_ref[...]

    hid_ref[...] = _dot(u_ref[...], ht_ref[...])
    acc = acc_ref[...]
    chunk = PEER_CHUNK_ROWS * PEER_NKEYS
    for c in range(PEER_EXPERT_BLOCK // chunk):
        span = slice(c * chunk, (c + 1) * chunk)
        w = _peer_gate_chunk(hid_ref[span, :], c * PEER_CHUNK_ROWS, cnt_ref, a0_ref, r1_ref, b1_ref)
        acc = acc + _dot(vt_ref[:, span], w)
    acc_ref[...] = acc

    @pl.when(e == pl.num_programs(1) - 1)
    def _():
        y = acc_ref[...].T
        o_ref[...] = _layer_norm_rows(DN_ALPHA * res_ref[...] + y, g_ref[...], b_ref[...])


def peer_ffn(h, ht, route, u, v_blocks, g, b, tt=512):
    n, d = h.shape
    eb = PEER_EXPERT_BLOCK
    rspec = pl.BlockSpec((PEER_HEADS, PEER_NKEYS, tt), lambda i, e: (0, 0, i))
    rowspec = pl.BlockSpec((PEER_HEADS, eb // PEER_NKEYS, tt), lambda i, e: (0, e, i))
    cnt, a0, r1, b1 = route
    return pl.pallas_call(
        _peer_ffn_kernel,
        grid=(n // tt, u.shape[0] // eb),
        in_specs=[pl.BlockSpec((d, tt), lambda i, e: (0, i)),
                  pl.BlockSpec((eb, d), lambda i, e: (e, 0)),
                  pl.BlockSpec((None, d, eb), lambda i, e: (e, 0, 0)),
                  rowspec, rowspec, rspec, rspec,
                  pl.BlockSpec((tt, d), lambda i, e: (i, 0)),
                  pl.BlockSpec((1, d), lambda i, e: (0, 0)),
                  pl.BlockSpec((1, d), lambda i, e: (0, 0))],
        out_specs=pl.BlockSpec((tt, d), lambda i, e: (i, 0)),
        out_shape=jax.ShapeDtypeStruct((n, d), F32),
        scratch_shapes=[pltpu.VMEM((d, tt), F32), pltpu.VMEM((eb, tt), F32),
                        pltpu.VMEM((PEER_HEADS, PEER_NKEYS, tt), BF16),
                        pltpu.VMEM((PEER_HEADS, PEER_NKEYS, tt), BF16)],
        compiler_params=_params("parallel", "arbitrary"),
        name="peer_ffn",
    )(ht, u, v_blocks, cnt, a0, r1, b1, h, g.reshape(1, d), b.reshape(1, d))


def kernel(x, na_w_in, na_rel_bias, na_w_out, mla_w_in, mla_q_norm, mla_kv_norm, mla_w_q_up, mla_w_kv_up, mla_w_out, hg_w_in, hg_lower_bound, hg_norm, hg_w_out, peer_w_q, peer_sub_keys, peer_u, peer_v, ln_mix_g, ln_mix_b, ln_ffn_g, ln_ffn_b):
    batch, seq, d = x.shape
    rows = seq // GRID_W
    lb_w = jax.nn.softmax(hg_lower_bound.astype(F32), axis=0)
    lb_all = jnp.cumsum(lb_w, axis=0) - lb_w[0:1]
    h = x.reshape(batch * seq, d)
    for layer in range(DEPTH):
        kind = layer % N_MIXERS
        j = layer // N_MIXERS
        if kind == 0:
            qk, vt = na_project(h, na_w_in[j])
            mix_in = neighborhood_attention(qk, vt, na_rel_bias[j], batch, rows)
            w_out = na_w_out[j]
        elif kind == 1:
            q, k, v = mla_projections(h, mla_w_in[j], mla_q_norm[j], mla_kv_norm[j],
                                      mla_w_q_up[j], mla_w_kv_up[j], seq)
            mix_in = mla_attention(q, k, v, batch, seq)
            w_out = mla_w_out[j]
        else:
            z = matmul(h, hg_w_in[j].astype(BF16), F32)
            mix_in = hgrn2_scan(z, lb_all[layer], hg_norm[j], batch, seq)
            w_out = hg_w_out[j]
        h, ht = matmul_res_ln(mix_in, w_out.astype(BF16), h, ln_mix_g[layer], ln_mix_b[layer])
        route = peer_route(ht, peer_w_q[layer].T.astype(BF16), peer_sub_keys[layer].astype(BF16))
        v_blocks = peer_v[layer].astype(BF16).reshape(-1, PEER_EXPERT_BLOCK, d).transpose(0, 2, 1)
        h = peer_ffn(h, ht, route, peer_u[layer].astype(BF16), v_blocks,
                     ln_ffn_g[layer], ln_ffn_b[layer])
    return h.reshape(batch, seq, d)
```

```python
import functools
import math

import jax
import jax.numpy as jnp
import numpy as np
from jax import lax
from jax.experimental import pallas as pl
from jax.experimental.pallas import tpu as pltpu

F32 = jnp.float32
BF16 = jnp.bfloat16

D_MODEL = 1024
DEPTH = 4
GRID_W = 64
N_MIXERS = 3

NA_HEADS = 16
NA_HEAD_DIM = 64
NA_KH = 8
NA_KW = 16

MLA_HEADS = 16
MLA_NOPE = 64
MLA_ROPE = 32
MLA_V = 64
MLA_Q_RANK = 256
MLA_KV_RANK = 256
ROPE_THETA = 10000.0

HG_HEADS = 8
HG_F = 128
HG_CHUNK = 128

PEER_HEADS = 8
PEER_NKEYS = 128
PEER_TOPK = 16
PEER_EXPERT_BLOCK = 1024
PEER_CHUNK_ROWS = 2
NORM_EPS = 1e-5
DN_ALPHA = (2.0 * DEPTH) ** 0.25

LANES = 128
BF16_SUBLANES = 16
VMEM_LIMIT = 48 * 1024 * 1024
MASK_NEG = -1e30
RSQRT2 = 0.7071067811865476


def _params(*sem):
    return pltpu.CompilerParams(dimension_semantics=sem, vmem_limit_bytes=VMEM_LIMIT)


def _layer_norm_rows(z, g, b):
    mu = jnp.mean(z, axis=-1, keepdims=True)
    zc = z - mu
    var = jnp.mean(zc * zc, axis=-1, keepdims=True)
    return zc * lax.rsqrt(var + NORM_EPS) * g + b


def _dot(a, b):
    return jnp.dot(a, b, preferred_element_type=F32)


def _dot_nt(a, b):
    return lax.dot_general(a, b, (((1,), (1,)), ((), ())), preferred_element_type=F32)


def _dot_tn(a, b):
    return lax.dot_general(a, b, (((0,), (0,)), ((), ())), preferred_element_type=F32)


def _mm_kernel(a_ref, w_ref, o_ref):
    o_ref[...] = _dot(a_ref[...].astype(BF16), w_ref[...]).astype(o_ref.dtype)


def matmul(a, w, out_dtype, tm=512, tn=1024):
    m, k = a.shape
    n = w.shape[1]
    tn = min(tn, n)
    return pl.pallas_call(
        _mm_kernel,
        grid=(m // tm, n // tn),
        in_specs=[pl.BlockSpec((tm, k), lambda i, j: (i, 0)),
                  pl.BlockSpec((k, tn), lambda i, j: (0, j))],
        out_specs=pl.BlockSpec((tm, tn), lambda i, j: (i, j)),
        out_shape=jax.ShapeDtypeStruct((m, n), out_dtype),
        compiler_params=_params("parallel", "parallel"),
        name="proj",
    )(a, w)


def _mm_res_ln_kernel(a_ref, w_ref, res_ref, g_ref, b_ref, o_ref, ot_ref):
    y = _dot(a_ref[...], w_ref[...])
    out = _layer_norm_rows(DN_ALPHA * res_ref[...] + y, g_ref[...], b_ref[...])
    o_ref[...] = out
    ot_ref[...] = out.T.astype(BF16)


def matmul_res_ln(a, w, res, g, b, tm=256):
    m, k = a.shape
    n = w.shape[1]
    return pl.pallas_call(
        _mm_res_ln_kernel,
        grid=(m // tm,),
        in_specs=[pl.BlockSpec((tm, k), lambda i: (i, 0)),
                  pl.BlockSpec((k, n), lambda i: (0, 0)),
                  pl.BlockSpec((tm, n), lambda i: (i, 0)),
                  pl.BlockSpec((1, n), lambda i: (0, 0)),
                  pl.BlockSpec((1, n), lambda i: (0, 0))],
        out_specs=[pl.BlockSpec((tm, n), lambda i: (i, 0)),
                   pl.BlockSpec((n, tm), lambda i: (0, i))],
        out_shape=[jax.ShapeDtypeStruct((m, n), F32),
                   jax.ShapeDtypeStruct((n, m), BF16)],
        compiler_params=_params("parallel"),
        name="out_proj_ln",
    )(a, w, res, g.reshape(1, n), b.reshape(1, n))


def _na_row_start(r, rows):
    return jnp.clip(r - NA_KH // 2, 0, rows - NA_KH)


def _na_proj_kernel(h_ref, wqk_ref, wvt_ref, qk_ref, vt_ref):
    hb = h_ref[...].astype(BF16)
    qk_ref[...] = _dot(hb, wqk_ref[...]).astype(qk_ref.dtype)
    vt_ref[...] = _dot_nt(wvt_ref[...], hb).astype(vt_ref.dtype)


def na_project(h, w_in, tm=512):
    n, d = h.shape
    wqk = w_in[:, :2 * d].astype(BF16)
    wvt = w_in[:, 2 * d:].T.astype(BF16)
    return pl.pallas_call(
        _na_proj_kernel,
        grid=(n // tm,),
        in_specs=[pl.BlockSpec((tm, d), lambda i: (i, 0)),
                  pl.BlockSpec((d, 2 * d), lambda i: (0, 0)),
                  pl.BlockSpec((d, d), lambda i: (0, 0))],
        out_specs=[pl.BlockSpec((tm, 2 * d), lambda i: (i, 0)),
                   pl.BlockSpec((d, tm), lambda i: (0, i))],
        out_shape=[jax.ShapeDtypeStruct((n, 2 * d), BF16), jax.ShapeDtypeStruct((d, n), BF16)],
        compiler_params=_params("parallel"),
        name="na_proj",
    )(h, wqk, wvt)


NA_QROWS = 2
NA_KROWS = NA_KH + NA_QROWS
NA_EDGE = NA_KH // 4


def _na_key_start(i, rows):
    return jnp.clip(i - NA_KH // 4, 0, (rows - NA_KROWS) // 2)


def _na_kernel(q_ref, k_ref, vt_ref, bias_ref, o_ref):
    n_keys = NA_KROWS * GRID_W
    n_q = NA_QROWS * GRID_W
    lane = lax.broadcasted_iota(jnp.int32, (n_q, LANES), 1)
    low = lane < NA_HEAD_DIM
    row_low = lax.broadcasted_iota(jnp.int32, (LANES, n_q), 0) < NA_HEAD_DIM
    for p in range(NA_HEADS // 2):
        cols = slice(p * LANES, (p + 1) * LANES)
        qp = q_ref[0, :, :, cols].reshape(n_q, LANES)
        zq = jnp.zeros_like(qp)
        kp = k_ref[0, :, :, cols].reshape(n_keys, LANES)
        vtp = vt_ref[cols, :]
        outs = []
        for half in range(2):
            qh = jnp.where(low, qp, zq) if half == 0 else jnp.where(low, zq, qp)
            s = _dot_nt(kp, qh) * (NA_HEAD_DIM ** -0.5) + bias_ref[0, 2 * p + half]
            m = jnp.max(s, axis=0, keepdims=True)
            e = jnp.exp(s - m)
            l = jnp.sum(e, axis=0, keepdims=True)
            outs.append(_dot(vtp, e.astype(BF16)) / l)
        o2 = jnp.where(row_low, outs[0], outs[1]).T.astype(o_ref.dtype)
        for a in range(NA_QROWS):
            o_ref[a, :, cols] = o2[a * GRID_W:(a + 1) * GRID_W]


def _na_bias_table(rel_bias, rows):
    cols = np.arange(GRID_W)
    c0 = np.clip(cols - NA_KW // 2, 0, GRID_W - NA_KW)
    kc = np.arange(GRID_W)
    inside = (kc[None, :] >= c0[:, None]) & (kc[None, :] < c0[:, None] + NA_KW)
    dc = np.clip(kc[None, :] - cols[:, None] + (NA_KW - 1), 0, 2 * NA_KW - 2)
    b2 = jnp.where(inside[None, None], rel_bias[:, :, dc].astype(F32), MASK_NEG)
    steps = rows // NA_QROWS
    variant_steps = list(range(NA_EDGE)) + [NA_EDGE] + list(range(steps - NA_EDGE, steps))
    dr = np.zeros((len(variant_steps), NA_QROWS, NA_KROWS), np.int32)
    ok = np.zeros(dr.shape, bool)
    for v, i in enumerate(variant_steps):
        key0 = 2 * int(np.clip(i - NA_KH // 4, 0, (rows - NA_KROWS) // 2))
        for a in range(NA_QROWS):
            r = NA_QROWS * i + a
            r0 = int(np.clip(r - NA_KH // 2, 0, rows - NA_KH))
            key_rows = key0 + np.arange(NA_KROWS)
            ok[v, a] = (key_rows >= r0) & (key_rows < r0 + NA_KH)
            dr[v, a] = np.clip(key_rows - r + NA_KH - 1, 0, 2 * NA_KH - 2)
    t = b2[:, dr]
    t = jnp.where(ok[None, :, :, :, None, None], t, MASK_NEG)
    t = t.transpose(1, 0, 3, 5, 2, 4)
    return t.reshape(len(variant_steps), NA_HEADS, NA_KROWS * GRID_W, NA_QROWS * GRID_W)


def neighborhood_attention(qk, vt, rel_bias, batch, rows):
    d = D_MODEL
    steps = rows // NA_QROWS
    n_keys = NA_KROWS * GRID_W
    qk4 = qk.reshape(batch, rows, GRID_W, 2 * d)
    bias = _na_bias_table(rel_bias, rows)
    el = pl.Element
    key0 = lambda i: _na_key_start(i, rows)

    def variant(i):
        return jnp.where(i < NA_EDGE, i, jnp.where(i >= steps - NA_EDGE, i - (steps - 2 * NA_EDGE - 1), NA_EDGE))

    out = pl.pallas_call(
        _na_kernel,
        grid=(batch, steps),
        in_specs=[pl.BlockSpec((el(1), el(NA_QROWS), el(GRID_W), el(d)),
                               lambda b, i: (b, NA_QROWS * i, 0, 0)),
                  pl.BlockSpec((el(1), el(NA_KROWS), el(GRID_W), el(d)),
                               lambda b, i: (b, 2 * key0(i), 0, d)),
                  pl.BlockSpec((el(d), el(n_keys)),
                               lambda b, i: (0, (b * (rows // 2) + key0(i)) * (2 * GRID_W))),
                  pl.BlockSpec((1, NA_HEADS, n_keys, NA_QROWS * GRID_W),
                               lambda b, i: (variant(i), 0, 0, 0))],
        out_specs=pl.BlockSpec((None, NA_QROWS, GRID_W, d), lambda b, i: (b, i, 0, 0)),
        out_shape=jax.ShapeDtypeStruct((batch, rows, GRID_W, d), BF16),
        compiler_params=_params("parallel", "arbitrary"),
        name="na_attn",
    )(qk4, qk4, vt, bias)
    return out.reshape(batch * rows * GRID_W, d)


def _rms_rows(x, g):
    return x * lax.rsqrt(jnp.mean(x * x, axis=-1, keepdims=True) + NORM_EPS) * g


def _mla_proj_kernel(x_ref, win_ref, qn_ref, kvn_ref, wqa_ref, wqb_ref, wkv_ref, cos_ref, sin_ref,
                     q_ref, k_ref, v_ref):
    hd = MLA_HEADS * LANES
    hin = _dot(x_ref[...].astype(BF16), win_ref[...])
    cq = _rms_rows(hin[:, :MLA_Q_RANK], qn_ref[...]).astype(BF16)
    ckv = _rms_rows(hin[:, MLA_Q_RANK:MLA_Q_RANK + MLA_KV_RANK], kvn_ref[...]).astype(BF16)
    cos = cos_ref[...]
    sin = sin_ref[...]
    cos_t = jnp.tile(cos, (1, MLA_HEADS))
    sin_t = jnp.tile(sin, (1, MLA_HEADS))
    q = _dot(cq, wqa_ref[...]) * cos_t + _dot(cq, wqb_ref[...]) * sin_t
    q_ref[...] = (q * ((MLA_NOPE + MLA_ROPE) ** -0.5)).astype(q_ref.dtype)
    kv = _dot(ckv, wkv_ref[...])
    base = MLA_Q_RANK + MLA_KV_RANK
    kpe = hin[:, base:base + LANES] * cos + hin[:, base + LANES:base + 2 * LANES] * sin
    k_ref[...] = (kv[:, :hd] + jnp.tile(kpe, (1, MLA_HEADS))).astype(k_ref.dtype)
    v_ref[...] = kv[:, hd:].astype(v_ref.dtype)


def _mla_weights(w_in, w_q_up, w_kv_up):
    r = MLA_ROPE
    half = r // 2
    dq = MLA_NOPE + r
    nh = MLA_HEADS
    base = MLA_Q_RANK + MLA_KV_RANK
    kpe = w_in[:, base:base + r]
    zpad = lambda w, lo, hi: jnp.pad(w, ((0, 0), (lo, hi)))
    swap = lambda w: jnp.concatenate([-w[..., half:], w[..., :half]], axis=-1)
    kpe_a = zpad(kpe, MLA_NOPE, LANES - MLA_NOPE - r)
    kpe_b = zpad(swap(kpe), MLA_NOPE, LANES - MLA_NOPE - r)
    win = jnp.concatenate([w_in[:, :base], kpe_a, kpe_b], axis=1)
    wq = w_q_up.reshape(MLA_Q_RANK, nh, dq)
    pad3 = lambda w, lo, hi: jnp.pad(w, ((0, 0), (0, 0), (lo, hi)))
    wqa = pad3(wq, 0, LANES - dq).reshape(MLA_Q_RANK, nh * LANES)
    wqb = pad3(swap(wq[:, :, MLA_NOPE:]), MLA_NOPE, LANES - dq).reshape(MLA_Q_RANK, nh * LANES)
    wkv = w_kv_up.reshape(MLA_KV_RANK, nh, MLA_NOPE + MLA_V)
    wk = pad3(wkv[:, :, :MLA_NOPE], 0, LANES - MLA_NOPE).reshape(MLA_KV_RANK, nh * LANES)
    wv = wkv[:, :, MLA_NOPE:].reshape(MLA_KV_RANK, nh * MLA_V)
    return (win.astype(BF16), wqa.astype(BF16), wqb.astype(BF16),
            jnp.concatenate([wk, wv], axis=1).astype(BF16))


def _rope_tables(seq):
    half = MLA_ROPE // 2
    inv_freq = ROPE_THETA ** (-jnp.arange(half, dtype=F32) * 2.0 / MLA_ROPE)
    ang = jnp.arange(seq, dtype=F32)[:, None] * inv_freq[None, :]
    cos = jnp.cos(ang)
    sin = jnp.sin(ang)
    tail = LANES - MLA_NOPE - MLA_ROPE
    cos_p = jnp.concatenate([jnp.ones((seq, MLA_NOPE), F32), cos, cos, jnp.zeros((seq, tail), F32)], axis=1)
    sin_p = jnp.concatenate([jnp.zeros((seq, MLA_NOPE), F32), sin, sin, jnp.zeros((seq, tail), F32)], axis=1)
    return cos_p, sin_p


def mla_projections(h, w_in, q_norm, kv_norm, w_q_up, w_kv_up, seq, tm=256):
    n, d = h.shape
    win, wqa, wqb, wkv = _mla_weights(w_in, w_q_up, w_kv_up)
    cos_p, sin_p = _rope_tables(seq)
    hd = MLA_HEADS * LANES
    per_seq = seq // tm
    full = lambda a: pl.BlockSpec(a.shape, lambda i: (0,) * a.ndim)
    qn = q_norm.reshape(1, -1).astype(F32)
    kvn = kv_norm.reshape(1, -1).astype(F32)
    return pl.pallas_call(
        _mla_proj_kernel,
        grid=(n // tm,),
        in_specs=[pl.BlockSpec((tm, d), lambda i: (i, 0)), full(win), full(qn), full(kvn),
                  full(wqa), full(wqb), full(wkv),
                  pl.BlockSpec((tm, LANES), lambda i: (i % per_seq, 0)),
                  pl.BlockSpec((tm, LANES), lambda i: (i % per_seq, 0))],
        out_specs=[pl.BlockSpec((tm, hd), lambda i: (i, 0)),
                   pl.BlockSpec((tm, hd), lambda i: (i, 0)),
                   pl.BlockSpec((tm, MLA_HEADS * MLA_V), lambda i: (i, 0))],
        out_shape=[jax.ShapeDtypeStruct((n, hd), BF16),
                   jax.ShapeDtypeStruct((n, hd), BF16),
                   jax.ShapeDtypeStruct((n, MLA_HEADS * MLA_V), BF16)],
        compiler_params=_params("parallel"),
        name="mla_proj",
    )(h, win, qn, kvn, wqa, wqb, wkv, cos_p, sin_p)


def _mla_attn_kernel(q_ref, k_ref, v_ref, o_ref):
    v = v_ref[...]
    lane = lax.broadcasted_iota(jnp.int32, o_ref.shape, 1)
    halves = []
    for half in range(2):
        cols = slice(half * LANES, (half + 1) * LANES)
        s = _dot_nt(q_ref[:, cols], k_ref[:, cols])
        m = jnp.max(s, axis=-1, keepdims=True)
        e = jnp.exp(s - m)
        l = jnp.sum(e, axis=-1, keepdims=True)
        halves.append(_dot(e.astype(BF16), v) / l)
    o_ref[...] = jnp.where(lane < MLA_V, halves[0], halves[1]).astype(o_ref.dtype)


def mla_attention(q, k, v, batch, seq, tq=256):
    hd = MLA_HEADS * LANES
    q3 = q.reshape(batch, seq, hd)
    k3 = k.reshape(batch, seq, hd)
    v3 = v.reshape(batch, seq, MLA_HEADS * MLA_V)
    out = pl.pallas_call(
        _mla_attn_kernel,
        grid=(batch, MLA_HEADS // 2, seq // tq),
        in_specs=[pl.BlockSpec((None, tq, 2 * LANES), lambda b, p, i: (b, i, p)),
                  pl.BlockSpec((None, seq, 2 * LANES), lambda b, p, i: (b, 0, p)),
                  pl.BlockSpec((None, seq, LANES), lambda b, p, i: (b, 0, p))],
        out_specs=pl.BlockSpec((None, tq, LANES), lambda b, p, i: (b, i, p)),
        out_shape=jax.ShapeDtypeStruct((batch, seq, MLA_HEADS * MLA_V), BF16),
        compiler_params=_params("parallel", "parallel", "arbitrary"),
        name="mla_attn",
    )(q3, k3, v3)
    return out.reshape(batch * seq, MLA_HEADS * MLA_V)


HG_LEVELS = int(math.log2(HG_CHUNK))
HG_UNROLL = 1


def _hg_constants(reverse):
    c = HG_CHUNK
    t = np.arange(c)[:, None]
    u = np.arange(c)[None, :]
    if not reverse:
        incl = u <= t
        rest = u > t
    else:
        incl = u >= t
        rest = u < t
    mats = [incl, rest]
    masks = []
    roles = []
    for lvl in range(1, HG_LEVELS + 1):
        size = 1 << lvl
        start = (t // size) * size
        mid = start + size // 2
        upper = t >= mid
        if not reverse:
            q_side = (u >= mid) & (u <= t)
            k_side = (u > t) & (u <= mid - 1)
            is_query = upper
        else:
            q_side = (u >= t) & (u < mid)
            k_side = (u >= mid) & (u < t)
            is_query = ~upper
        mats.append(np.where(is_query, q_side, k_side))
        same = (t // size) == (u // size)
        key_row = (~is_query).T
        masks.append(same & is_query & np.broadcast_to(key_row, (c, c)))
        roles.append(np.broadcast_to(is_query, (c, LANES)))
    w = np.concatenate(mats, axis=0).astype(np.float32)
    return (jnp.asarray(w, BF16), jnp.asarray(np.stack(masks).astype(np.float32)),
            jnp.asarray(np.stack(roles).astype(np.float32)))


def _hg_chunk(q, zf, v_b, lb, w_ref, m_ref, r_ref, st, total_row):
    c = HG_CHUNK
    sg = jax.nn.sigmoid(zf)
    k = (1.0 - lb) * jax.nn.sigmoid(-zf)
    lf = jnp.log(lb + (1.0 - lb) * sg)
    hi = lf.astype(BF16)
    r1 = lf - hi.astype(F32)
    mid = r1.astype(BF16)
    lo = (r1 - mid.astype(F32)).astype(BF16)
    f = lf.shape[1]
    ex3 = _dot(w_ref[...], jnp.concatenate([hi, mid, lo], axis=1))
    ex = ex3[:, :f] + ex3[:, f:2 * f] + ex3[:, 2 * f:]
    b_incl = ex[0:c]
    total = ex[total_row:total_row + 1]
    qd = q * jnp.exp(b_incl)
    kd = k * jnp.exp(ex[c:2 * c])
    row = lax.broadcasted_iota(jnp.int32, (c, c), 0)
    col = lax.broadcasted_iota(jnp.int32, (c, c), 1)
    a = jnp.where(row == col, jnp.sum(q * k, axis=-1, keepdims=True), 0.0)
    xs = []
    for lvl in range(HG_LEVELS):
        x = jnp.where(r_ref[lvl] > 0.5, q, k) * jnp.exp(ex[(2 + lvl) * c:(3 + lvl) * c])
        xs.append(x.astype(BF16))
    for lvl in range(0, HG_LEVELS, 2):
        pair = xs[lvl:lvl + 2]
        xb = jnp.concatenate(pair, axis=0)
        g = _dot_nt(xb, xb)
        for j in range(len(pair)):
            a = a + m_ref[lvl + j] * g[j * c:(j + 1) * c, j * c:(j + 1) * c]
    o = _dot(a.astype(BF16), v_b) + _dot_nt(qd.astype(BF16), st.astype(BF16))
    return o, st * jnp.exp(total) + _dot_tn(v_b, kd.astype(BF16))


def _hg_kernel(zq_ref, zff_ref, zfb_ref, zi_ref, zg_ref, lb_ref, g_ref,
               wf_ref, mf_ref, rf_ref, wb_ref, mb_ref, rb_ref,
               o_ref, accf_ref, accb_ref):
    c = HG_CHUNK
    seq = zq_ref.shape[0]
    n = seq // c
    lb_f = lb_ref[0:1, :]
    lb_b = lb_ref[1:2, :]

    def gated(ref, rows):
        z = ref[rows, :]
        return z * jax.nn.sigmoid(z)

    def body(i, carry):
        st_f, st_b = carry
        rf = pl.ds(pl.multiple_of(i * c, c), c)
        rb = pl.ds(pl.multiple_of((n - 1 - i) * c, c), c)
        of, st_f = _hg_chunk(gated(zq_ref, rf), zff_ref[rf, :], zi_ref[rf, :].astype(BF16), lb_f,
                             wf_ref, mf_ref, rf_ref, st_f, c - 1)
        accf_ref[rf, :] = of
        ob, st_b = _hg_chunk(gated(zq_ref, rb), zfb_ref[rb, :], zi_ref[rb, :].astype(BF16), lb_b,
                             wb_ref, mb_ref, rb_ref, st_b, 0)
        accb_ref[rb, :] = ob
        return st_f, st_b

    zero = jnp.zeros((HG_F, HG_F), F32)
    lax.fori_loop(0, n, body, (zero, zero), unroll=HG_UNROLL)

    def finish(i, carry):
        rows = pl.ds(pl.multiple_of(i * c, c), c)
        o = accf_ref[rows, :] + accb_ref[rows, :]
        o = o * lax.rsqrt(jnp.mean(o * o, axis=-1, keepdims=True) + NORM_EPS)
        o_ref[rows, :] = (o * g_ref[...] * gated(zg_ref, rows)).astype(o_ref.dtype)
        return carry

    lax.fori_loop(0, n, finish, 0)


def hgrn2_scan(z, lb, norm_g, batch, seq):
    d = D_MODEL
    z3 = z.reshape(batch, seq, 5 * d)
    wf, mf, rf = _hg_constants(False)
    wb, mb, rb = _hg_constants(True)
    zspec = lambda j: pl.BlockSpec((None, seq, HG_F), lambda b, h: (b, 0, j * HG_HEADS + h))
    full = lambda a: pl.BlockSpec(a.shape, lambda b, h: (0,) * a.ndim)
    out = pl.pallas_call(
        _hg_kernel,
        grid=(batch, HG_HEADS),
        in_specs=[zspec(0), zspec(1), zspec(2), zspec(3), zspec(4),
                  pl.BlockSpec((2, HG_F), lambda b, h: (0, h)),
                  pl.BlockSpec((1, HG_F), lambda b, h: (0, h)),
                  full(wf), full(mf), full(rf), full(wb), full(mb), full(rb)],
        out_specs=pl.BlockSpec((None, seq, HG_F), lambda b, h: (b, 0, h)),
        out_shape=jax.ShapeDtypeStruct((batch, seq, d), BF16),
        scratch_shapes=[pltpu.VMEM((seq, HG_F), F32), pltpu.VMEM((seq, HG_F), F32)],
        compiler_params=_params("parallel", "parallel"),
        name="hgrn2_scan",
    )(z3, z3, z3, z3, z3, lb.astype(F32), norm_g.reshape(1, d).astype(F32),
      wf, mf, rf, wb, mb, rb)
    return out.reshape(batch * seq, d)


def _top_values(s, count, with_rank=False):
    vals = []
    cur = s
    rank = jnp.full(s.shape, float(count), F32)
    for it in range(count):
        m = jnp.max(cur, axis=0, keepdims=True)
        vals.append(m)
        hit = cur >= m
        if with_rank:
            rank = jnp.where(hit, float(it), rank)
        cur = jnp.where(hit, -jnp.inf, cur)
    vals = jnp.concatenate(vals, axis=0)
    return (vals, rank) if with_rank else vals


def _peer_route_kernel(ht_ref, wq_ref, keys_ref, cnt_ref, a0_ref, r1_ref, b1_ref):
    kk = PEER_TOPK
    ht = ht_ref[...]
    for h in range(PEER_HEADS):
        scores = []
        for c in range(2):
            g = 2 * h + c
            qt = _dot(wq_ref[g * LANES:(g + 1) * LANES, :], ht)
            scores.append(_dot(keys_ref[c], qt.astype(BF16)))
        s0, s1 = scores
        sv0 = _top_values(s0, kk)
        sv1, r1 = _top_values(s1, kk, with_rank=True)
        cands = [sv0[a:a + 1] + sv1[0:kk // (a + 1)] for a in range(kk)]
        n_cand = sum(kk // (a + 1) for a in range(kk))
        pad = (-n_cand) % 8
        if pad:
            cands.append(jnp.full((pad, ht.shape[1]), -jnp.inf, F32))
        tau = _top_values(jnp.concatenate(cands, axis=0), kk)[kk - 1:kk]
        e0 = jnp.exp(sv0 - sv0[0:1])
        e1 = jnp.exp(sv1 - sv1[0:1])
        z = jnp.zeros_like(tau)
        for a in range(kk):
            nb = kk // (a + 1)
            sel = (sv0[a:a + 1] + sv1[0:nb]) >= tau
            z = z + e0[a:a + 1] * jnp.sum(jnp.where(sel, e1[0:nb], 0.0), axis=0, keepdims=True)
        cnt = jnp.zeros(s0.shape, F32)
        for a in range(kk):
            pair_ok = (sv0[a:a + 1] + sv1) >= tau
            n_a = jnp.sum(jnp.where(pair_ok, 1.0, 0.0), axis=0, keepdims=True)
            cnt = jnp.where(s0 == sv0[a:a + 1], n_a, cnt)
        cnt_ref[h] = cnt
        a0_ref[h] = jnp.exp(s0 - sv0[0:1])
        r1_ref[h] = r1.astype(BF16)
        b1_ref[h] = (jnp.exp(s1 - sv1[0:1]) * (0.5 / z)).astype(BF16)


def peer_route(ht, wq_t, keys, tt=256):
    d, n = ht.shape
    shape = (PEER_HEADS, PEER_NKEYS, n)
    ospec = pl.BlockSpec((PEER_HEADS, PEER_NKEYS, tt), lambda i: (0, 0, i))
    return pl.pallas_call(
        _peer_route_kernel,
        grid=(n // tt,),
        in_specs=[pl.BlockSpec((d, tt), lambda i: (0, i)),
                  pl.BlockSpec(wq_t.shape, lambda i: (0, 0)),
                  pl.BlockSpec(keys.shape, lambda i: (0, 0, 0))],
        out_specs=[ospec, ospec, ospec, ospec],
        out_shape=[jax.ShapeDtypeStruct(shape, F32), jax.ShapeDtypeStruct(shape, F32),
                   jax.ShapeDtypeStruct(shape, BF16), jax.ShapeDtypeStruct(shape, BF16)],
        compiler_params=_params("parallel"),
        name="peer_route",
    )(ht, wq_t, keys)


def _peer_gate_chunk(hid, i0, cnt_ref, a0_ref, r1_ref, b1_ref):
    tt = hid.shape[1]
    zero = jnp.zeros((PEER_NKEYS, LANES), BF16)

    def row_tile(ref, h, ii, cols):
        row = jnp.broadcast_to(ref[h, ii:ii + 1, cols], (BF16_SUBLANES, LANES)).astype(BF16)
        return pltpu.repeat(row, PEER_NKEYS // BF16_SUBLANES, axis=0)

    out_rows = []
    for k in range(PEER_CHUNK_ROWS):
        tiles = []
        for tc in range(tt // LANES):
            cols = slice(tc * LANES, (tc + 1) * LANES)
            gate = zero
            for h in range(PEER_HEADS):
                cnt = row_tile(cnt_ref, h, i0 + k, cols)
                a0 = row_tile(a0_ref, h, i0 + k, cols)
                gate = gate + jnp.where(r1_ref[h, :, cols] < cnt, a0 * b1_ref[h, :, cols], zero)
            x = hid[k * PEER_NKEYS:(k + 1) * PEER_NKEYS, cols]
            act = x.astype(BF16) * (1.0 + lax.erf(x * RSQRT2)).astype(BF16)
            tiles.append(gate * act)
        out_rows.append(jnp.concatenate(tiles, axis=1))
    return jnp.concatenate(out_rows, axis=0)


def _ordered_after(x, dep):
    z = pltpu.bitcast(dep[:BF16_SUBLANES, :LANES], jnp.uint32)
    z = (z >> 16) >> 16
    zero = pltpu.bitcast(z, BF16)
    zero = pltpu.repeat(pltpu.repeat(zero, x.shape[0] // BF16_SUBLANES, axis=0), x.shape[1] // LANES, axis=1)
    return x + zero


def _peer_ffn_kernel(ht_ref, u0_ref, ub_ref, un_ref, vta_ref, vtb_ref, cnta_ref, a0a_ref, cntb_ref, a0b_ref,
                     r1_in_ref, b1_in_ref, res_ref, g_ref, b_ref,
                     o_ref, acc_ref, hida_ref, hidb_ref, r1_ref, b1_ref):
    s = pl.program_id(1)
    half = ht_ref.shape[1] // 2

    @pl.when(s == 0)
    def _():
        acc_ref[...] = jnp.zeros_like(acc_ref)
        r1_ref[...] = r1_in_ref[...]
        b1_ref[...] = b1_in_ref[...]
        hida_ref[...] = _dot(u0_ref[...], ht_ref[...])

    chunk = PEER_CHUNK_ROWS * PEER_NKEYS
    n_chunks = PEER_EXPERT_BLOCK // chunk

    def run_block(acc, hid_ref, vt_ref, cnt_ref, a0_ref, next_u_ref, next_hid_ref):
        for c in range(n_chunks):
            span = slice(c * chunk, (c + 1) * chunk)
            w = _peer_gate_chunk(hid_ref[span, :], c * PEER_CHUNK_ROWS, cnt_ref, a0_ref, r1_ref, b1_ref)
            acc = acc + _dot(vt_ref[:, span], w)
            if c % (n_chunks // 2) == n_chunks // 2 - 1:
                hh = c // (n_chunks // 2)
                tcols = slice(hh * half, (hh + 1) * half)
                next_hid_ref[:, tcols] = _dot(next_u_ref[...], _ordered_after(ht_ref[:, tcols], w))
        return acc

    acc = run_block(acc_ref[...], hida_ref, vta_ref, cnta_ref, a0a_ref, ub_ref, hidb_ref)
    acc_ref[...] = run_block(acc, hidb_ref, vtb_ref, cntb_ref, a0b_ref, un_ref, hida_ref)

    @pl.when(s == pl.num_programs(1) - 1)
    def _():
        y = acc_ref[...].T
        o_ref[...] = _layer_norm_rows(DN_ALPHA * res_ref[...] + y, g_ref[...], b_ref[...])


def peer_ffn(h, ht, route, u, v_blocks, g, b, tt=512):
    n, d = h.shape
    eb = PEER_EXPERT_BLOCK
    last = u.shape[0] // eb - 1
    rspec = pl.BlockSpec((PEER_HEADS, PEER_NKEYS, tt), lambda i, s: (0, 0, i))
    rowspec = lambda off: pl.BlockSpec((PEER_HEADS, eb // PEER_NKEYS, tt), lambda i, s: (0, 2 * s + off, i))
    cnt, a0, r1, b1 = route
    return pl.pallas_call(
        _peer_ffn_kernel,
        grid=(n // tt, (last + 1) // 2),
        in_specs=[pl.BlockSpec((d, tt), lambda i, s: (0, i)),
                  pl.BlockSpec((eb, d), lambda i, s: (0, 0)),
                  pl.BlockSpec((eb, d), lambda i, s: (2 * s + 1, 0)),
                  pl.BlockSpec((eb, d), lambda i, s: (jnp.minimum(2 * s + 2, last), 0)),
                  pl.BlockSpec((None, d, eb), lambda i, s: (2 * s, 0, 0)),
                  pl.BlockSpec((None, d, eb), lambda i, s: (2 * s + 1, 0, 0)),
                  rowspec(0), rowspec(0), rowspec(1), rowspec(1),
                  rspec, rspec,
                  pl.BlockSpec((tt, d), lambda i, s: (i, 0)),
                  pl.BlockSpec((1, d), lambda i, s: (0, 0)),
                  pl.BlockSpec((1, d), lambda i, s: (0, 0))],
        out_specs=pl.BlockSpec((tt, d), lambda i, s: (i, 0)),
        out_shape=jax.ShapeDtypeStruct((n, d), F32),
        scratch_shapes=[pltpu.VMEM((d, tt), F32), pltpu.VMEM((eb, tt), F32), pltpu.VMEM((eb, tt), F32),
                        pltpu.VMEM((PEER_HEADS, PEER_NKEYS, tt), BF16),
                        pltpu.VMEM((PEER_HEADS, PEER_NKEYS, tt), BF16)],
        compiler_params=_params("parallel", "arbitrary"),
        name="peer_ffn",
    )(ht, u, u, u, v_blocks, v_blocks, cnt, a0, cnt, a0, r1, b1, h, g.reshape(1, d), b.reshape(1, d))


def kernel(x, na_w_in, na_rel_bias, na_w_out, mla_w_in, mla_q_norm, mla_kv_norm, mla_w_q_up, mla_w_kv_up, mla_w_out, hg_w_in, hg_lower_bound, hg_norm, hg_w_out, peer_w_q, peer_sub_keys, peer_u, peer_v, ln_mix_g, ln_mix_b, ln_ffn_g, ln_ffn_b):
    batch, seq, d = x.shape
    rows = seq // GRID_W
    lb_w = jax.nn.softmax(hg_lower_bound.astype(F32), axis=0)
    lb_all = jnp.cumsum(lb_w, axis=0) - lb_w[0:1]
    h = x.reshape(batch * seq, d)
    for layer in range(DEPTH):
        kind = layer % N_MIXERS
        j = layer // N_MIXERS
        if kind == 0:
            qk, vt = na_project(h, na_w_in[j])
            mix_in = neighborhood_attention(qk, vt, na_rel_bias[j], batch, rows)
            w_out = na_w_out[j]
        elif kind == 1:
            q, k, v = mla_projections(h, mla_w_in[j], mla_q_norm[j], mla_kv_norm[j],
                                      mla_w_q_up[j], mla_w_kv_up[j], seq)
            mix_in = mla_attention(q, k, v, batch, seq)
            w_out = mla_w_out[j]
        else:
            z = matmul(h, hg_w_in[j].astype(BF16), F32)
            mix_in = hgrn2_scan(z, lb_all[layer], hg_norm[j], batch, seq)
            w_out = hg_w_out[j]
        h, ht = matmul_res_ln(mix_in, w_out.astype(BF16), h, ln_mix_g[layer], ln_mix_b[layer])
        route = peer_route(ht, peer_w_q[layer].T.astype(BF16), peer_sub_keys[layer].astype(BF16))
        v_blocks = peer_v[layer].astype(BF16).reshape(-1, PEER_EXPERT_BLOCK, d).transpose(0, 2, 1)
        h = peer_ffn(h, ht, route, peer_u[layer].astype(BF16), v_blocks,
                     ln_ffn_g[layer], ln_ffn_b[layer])
    return h.reshape(batch, seq, d)
```

```python
import functools
import math

import jax
import jax.numpy as jnp
import numpy as np
from jax import lax
from jax.experimental import pallas as pl
from jax.experimental.pallas import tpu as pltpu

F32 = jnp.float32
BF16 = jnp.bfloat16

D_MODEL = 1024
DEPTH = 4
GRID_W = 64
N_MIXERS = 3

NA_HEADS = 16
NA_HEAD_DIM = 64
NA_KH = 8
NA_KW = 16

MLA_HEADS = 16
MLA_NOPE = 64
MLA_ROPE = 32
MLA_V = 64
MLA_Q_RANK = 256
MLA_KV_RANK = 256
ROPE_THETA = 10000.0

HG_HEADS = 8
HG_F = 128
HG_CHUNK = 128

PEER_HEADS = 8
PEER_NKEYS = 128
PEER_TOPK = 16
PEER_EXPERT_BLOCK = 1024
PEER_CHUNK_ROWS = 2
NORM_EPS = 1e-5
DN_ALPHA = (2.0 * DEPTH) ** 0.25

LANES = 128
F32_SUBLANES = 8
BF16_SUBLANES = 16
VMEM_LIMIT = 48 * 1024 * 1024
MASK_NEG = -1e30
RSQRT2 = 0.7071067811865476


def _params(*sem):
    return pltpu.CompilerParams(dimension_semantics=sem, vmem_limit_bytes=VMEM_LIMIT)


def _layer_norm_rows(z, g, b):
    mu = jnp.mean(z, axis=-1, keepdims=True)
    zc = z - mu
    var = jnp.mean(zc * zc, axis=-1, keepdims=True)
    return zc * lax.rsqrt(var + NORM_EPS) * g + b


def _dot(a, b):
    return jnp.dot(a, b, preferred_element_type=F32)


def _dot_nt(a, b):
    return lax.dot_general(a, b, (((1,), (1,)), ((), ())), preferred_element_type=F32)


def _dot_tn(a, b):
    return lax.dot_general(a, b, (((0,), (0,)), ((), ())), preferred_element_type=F32)


def _mm_kernel(a_ref, w_ref, o_ref):
    o_ref[...] = _dot(a_ref[...].astype(BF16), w_ref[...]).astype(o_ref.dtype)


def matmul(a, w, out_dtype, tm=512, tn=1024):
    m, k = a.shape
    n = w.shape[1]
    tn = min(tn, n)
    return pl.pallas_call(
        _mm_kernel,
        grid=(m // tm, n // tn),
        in_specs=[pl.BlockSpec((tm, k), lambda i, j: (i, 0)),
                  pl.BlockSpec((k, tn), lambda i, j: (0, j))],
        out_specs=pl.BlockSpec((tm, tn), lambda i, j: (i, j)),
        out_shape=jax.ShapeDtypeStruct((m, n), out_dtype),
        compiler_params=_params("parallel", "parallel"),
        name="proj",
    )(a, w)


def _mm_res_ln_kernel(a_ref, w_ref, res_ref, g_ref, b_ref, o_ref, ot_ref):
    y = _dot(a_ref[...], w_ref[...])
    out = _layer_norm_rows(DN_ALPHA * res_ref[...] + y, g_ref[...], b_ref[...])
    o_ref[...] = out
    ot_ref[...] = out.T.astype(BF16)


def matmul_res_ln(a, w, res, g, b, tm=256):
    m, k = a.shape
    n = w.shape[1]
    return pl.pallas_call(
        _mm_res_ln_kernel,
        grid=(m // tm,),
        in_specs=[pl.BlockSpec((tm, k), lambda i: (i, 0)),
                  pl.BlockSpec((k, n), lambda i: (0, 0)),
                  pl.BlockSpec((tm, n), lambda i: (i, 0)),
                  pl.BlockSpec((1, n), lambda i: (0, 0)),
                  pl.BlockSpec((1, n), lambda i: (0, 0))],
        out_specs=[pl.BlockSpec((tm, n), lambda i: (i, 0)),
                   pl.BlockSpec((n, tm), lambda i: (0, i))],
        out_shape=[jax.ShapeDtypeStruct((m, n), F32),
                   jax.ShapeDtypeStruct((n, m), BF16)],
        compiler_params=_params("parallel"),
        name="out_proj_ln",
    )(a, w, res, g.reshape(1, n), b.reshape(1, n))


def _na_row_start(r, rows):
    return jnp.clip(r - NA_KH // 2, 0, rows - NA_KH)


def _na_proj_kernel(h_ref, wqk_ref, wvt_ref, qk_ref, vt_ref):
    hb = h_ref[...].astype(BF16)
    qk_ref[...] = _dot(hb, wqk_ref[...]).astype(qk_ref.dtype)
    vt_ref[...] = _dot_nt(wvt_ref[...], hb).astype(vt_ref.dtype)


def na_project(h, w_in, tm=512):
    n, d = h.shape
    wqk = w_in[:, :2 * d].astype(BF16)
    wvt = w_in[:, 2 * d:].T.astype(BF16)
    return pl.pallas_call(
        _na_proj_kernel,
        grid=(n // tm,),
        in_specs=[pl.BlockSpec((tm, d), lambda i: (i, 0)),
                  pl.BlockSpec((d, 2 * d), lambda i: (0, 0)),
                  pl.BlockSpec((d, d), lambda i: (0, 0))],
        out_specs=[pl.BlockSpec((tm, 2 * d), lambda i: (i, 0)),
                   pl.BlockSpec((d, tm), lambda i: (0, i))],
        out_shape=[jax.ShapeDtypeStruct((n, 2 * d), BF16), jax.ShapeDtypeStruct((d, n), BF16)],
        compiler_params=_params("parallel"),
        name="na_proj",
    )(h, wqk, wvt)


NA_QROWS = 2
NA_KROWS = NA_KH + NA_QROWS
NA_EDGE = NA_KH // 4


def _na_key_start(i, rows):
    return jnp.clip(i - NA_KH // 4, 0, (rows - NA_KROWS) // 2)


def _na_kernel(q_ref, k_ref, vt_ref, bias_ref, o_ref):
    n_keys = NA_KROWS * GRID_W
    n_q = NA_QROWS * GRID_W
    lane = lax.broadcasted_iota(jnp.int32, (n_q, LANES), 1)
    low = lane < NA_HEAD_DIM
    row_low = lax.broadcasted_iota(jnp.int32, (LANES, n_q), 0) < NA_HEAD_DIM
    for p in range(NA_HEADS // 2):
        cols = slice(p * LANES, (p + 1) * LANES)
        qp = q_ref[0, :, :, cols].reshape(n_q, LANES)
        zq = jnp.zeros_like(qp)
        kp = k_ref[0, :, :, cols].reshape(n_keys, LANES)
        vtp = vt_ref[cols, :]
        outs = []
        for half in range(2):
            qh = jnp.where(low, qp, zq) if half == 0 else jnp.where(low, zq, qp)
            s = _dot_nt(kp, qh) * (NA_HEAD_DIM ** -0.5) + bias_ref[0, 2 * p + half]
            m = jnp.max(s, axis=0, keepdims=True)
            e = jnp.exp(s - m)
            l = jnp.sum(e, axis=0, keepdims=True)
            outs.append(_dot(vtp, e.astype(BF16)) / l)
        o2 = jnp.where(row_low, outs[0], outs[1]).T.astype(o_ref.dtype)
        for a in range(NA_QROWS):
            o_ref[a, :, cols] = o2[a * GRID_W:(a + 1) * GRID_W]


def _na_bias_table(rel_bias, rows):
    cols = np.arange(GRID_W)
    c0 = np.clip(cols - NA_KW // 2, 0, GRID_W - NA_KW)
    kc = np.arange(GRID_W)
    inside = (kc[None, :] >= c0[:, None]) & (kc[None, :] < c0[:, None] + NA_KW)
    dc = np.clip(kc[None, :] - cols[:, None] + (NA_KW - 1), 0, 2 * NA_KW - 2)
    b2 = jnp.where(inside[None, None], rel_bias[:, :, dc].astype(F32), MASK_NEG)
    steps = rows // NA_QROWS
    variant_steps = list(range(NA_EDGE)) + [NA_EDGE] + list(range(steps - NA_EDGE, steps))
    dr = np.zeros((len(variant_steps), NA_QROWS, NA_KROWS), np.int32)
    ok = np.zeros(dr.shape, bool)
    for v, i in enumerate(variant_steps):
        key0 = 2 * int(np.clip(i - NA_KH // 4, 0, (rows - NA_KROWS) // 2))
        for a in range(NA_QROWS):
            r = NA_QROWS * i + a
            r0 = int(np.clip(r - NA_KH // 2, 0, rows - NA_KH))
            key_rows = key0 + np.arange(NA_KROWS)
            ok[v, a] = (key_rows >= r0) & (key_rows < r0 + NA_KH)
            dr[v, a] = np.clip(key_rows - r + NA_KH - 1, 0, 2 * NA_KH - 2)
    t = b2[:, dr]
    t = jnp.where(ok[None, :, :, :, None, None], t, MASK_NEG)
    t = t.transpose(1, 0, 3, 5, 2, 4)
    return t.reshape(len(variant_steps), NA_HEADS, NA_KROWS * GRID_W, NA_QROWS * GRID_W)


def neighborhood_attention(qk, vt, rel_bias, batch, rows):
    d = D_MODEL
    steps = rows // NA_QROWS
    n_keys = NA_KROWS * GRID_W
    qk4 = qk.reshape(batch, rows, GRID_W, 2 * d)
    bias = _na_bias_table(rel_bias, rows)
    el = pl.Element
    key0 = lambda i: _na_key_start(i, rows)

    def variant(i):
        return jnp.where(i < NA_EDGE, i, jnp.where(i >= steps - NA_EDGE, i - (steps - 2 * NA_EDGE - 1), NA_EDGE))

    out = pl.pallas_call(
        _na_kernel,
        grid=(batch, steps),
        in_specs=[pl.BlockSpec((el(1), el(NA_QROWS), el(GRID_W), el(d)),
                               lambda b, i: (b, NA_QROWS * i, 0, 0)),
                  pl.BlockSpec((el(1), el(NA_KROWS), el(GRID_W), el(d)),
                               lambda b, i: (b, 2 * key0(i), 0, d)),
                  pl.BlockSpec((el(d), el(n_keys)),
                               lambda b, i: (0, (b * (rows // 2) + key0(i)) * (2 * GRID_W))),
                  pl.BlockSpec((1, NA_HEADS, n_keys, NA_QROWS * GRID_W),
                               lambda b, i: (variant(i), 0, 0, 0))],
        out_specs=pl.BlockSpec((None, NA_QROWS, GRID_W, d), lambda b, i: (b, i, 0, 0)),
        out_shape=jax.ShapeDtypeStruct((batch, rows, GRID_W, d), BF16),
        compiler_params=_params("parallel", "arbitrary"),
        name="na_attn",
    )(qk4, qk4, vt, bias)
    return out.reshape(batch * rows * GRID_W, d)


def _rms_rows(x, g):
    return x * lax.rsqrt(jnp.mean(x * x, axis=-1, keepdims=True) + NORM_EPS) * g


def _mla_proj_kernel(x_ref, win_ref, qn_ref, kvn_ref, wqa_ref, wqb_ref, wkv_ref, cos_ref, sin_ref,
                     q_ref, k_ref, v_ref):
    hd = MLA_HEADS * LANES
    hin = _dot(x_ref[...].astype(BF16), win_ref[...])
    cq = _rms_rows(hin[:, :MLA_Q_RANK], qn_ref[...]).astype(BF16)
    ckv = _rms_rows(hin[:, MLA_Q_RANK:MLA_Q_RANK + MLA_KV_RANK], kvn_ref[...]).astype(BF16)
    cos = cos_ref[...]
    sin = sin_ref[...]
    cos_t = jnp.tile(cos, (1, MLA_HEADS))
    sin_t = jnp.tile(sin, (1, MLA_HEADS))
    q = _dot(cq, wqa_ref[...]) * cos_t + _dot(cq, wqb_ref[...]) * sin_t
    q_ref[...] = (q * ((MLA_NOPE + MLA_ROPE) ** -0.5)).astype(q_ref.dtype)
    kv = _dot(ckv, wkv_ref[...])
    base = MLA_Q_RANK + MLA_KV_RANK
    kpe = hin[:, base:base + LANES] * cos + hin[:, base + LANES:base + 2 * LANES] * sin
    k_ref[...] = (kv[:, :hd] + jnp.tile(kpe, (1, MLA_HEADS))).astype(k_ref.dtype)
    v_ref[...] = kv[:, hd:].astype(v_ref.dtype)


def _mla_weights(w_in, w_q_up, w_kv_up):
    r = MLA_ROPE
    half = r // 2
    dq = MLA_NOPE + r
    nh = MLA_HEADS
    base = MLA_Q_RANK + MLA_KV_RANK
    kpe = w_in[:, base:base + r]
    zpad = lambda w, lo, hi: jnp.pad(w, ((0, 0), (lo, hi)))
    swap = lambda w: jnp.concatenate([-w[..., half:], w[..., :half]], axis=-1)
    kpe_a = zpad(kpe, MLA_NOPE, LANES - MLA_NOPE - r)
    kpe_b = zpad(swap(kpe), MLA_NOPE, LANES - MLA_NOPE - r)
    win = jnp.concatenate([w_in[:, :base], kpe_a, kpe_b], axis=1)
    wq = w_q_up.reshape(MLA_Q_RANK, nh, dq)
    pad3 = lambda w, lo, hi: jnp.pad(w, ((0, 0), (0, 0), (lo, hi)))
    wqa = pad3(wq, 0, LANES - dq).reshape(MLA_Q_RANK, nh * LANES)
    wqb = pad3(swap(wq[:, :, MLA_NOPE:]), MLA_NOPE, LANES - dq).reshape(MLA_Q_RANK, nh * LANES)
    wkv = w_kv_up.reshape(MLA_KV_RANK, nh, MLA_NOPE + MLA_V)
    wk = pad3(wkv[:, :, :MLA_NOPE], 0, LANES - MLA_NOPE).reshape(MLA_KV_RANK, nh * LANES)
    wv = wkv[:, :, MLA_NOPE:].reshape(MLA_KV_RANK, nh * MLA_V)
    return (win.astype(BF16), wqa.astype(BF16), wqb.astype(BF16),
            jnp.concatenate([wk, wv], axis=1).astype(BF16))


def _rope_tables(seq):
    half = MLA_ROPE // 2
    inv_freq = ROPE_THETA ** (-jnp.arange(half, dtype=F32) * 2.0 / MLA_ROPE)
    ang = jnp.arange(seq, dtype=F32)[:, None] * inv_freq[None, :]
    cos = jnp.cos(ang)
    sin = jnp.sin(ang)
    tail = LANES - MLA_NOPE - MLA_ROPE
    cos_p = jnp.concatenate([jnp.ones((seq, MLA_NOPE), F32), cos, cos, jnp.zeros((seq, tail), F32)], axis=1)
    sin_p = jnp.concatenate([jnp.zeros((seq, MLA_NOPE), F32), sin, sin, jnp.zeros((seq, tail), F32)], axis=1)
    return cos_p, sin_p


def mla_projections(h, w_in, q_norm, kv_norm, w_q_up, w_kv_up, seq, tm=256):
    n, d = h.shape
    win, wqa, wqb, wkv = _mla_weights(w_in, w_q_up, w_kv_up)
    cos_p, sin_p = _rope_tables(seq)
    hd = MLA_HEADS * LANES
    per_seq = seq // tm
    full = lambda a: pl.BlockSpec(a.shape, lambda i: (0,) * a.ndim)
    qn = q_norm.reshape(1, -1).astype(F32)
    kvn = kv_norm.reshape(1, -1).astype(F32)
    return pl.pallas_call(
        _mla_proj_kernel,
        grid=(n // tm,),
        in_specs=[pl.BlockSpec((tm, d), lambda i: (i, 0)), full(win), full(qn), full(kvn),
                  full(wqa), full(wqb), full(wkv),
                  pl.BlockSpec((tm, LANES), lambda i: (i % per_seq, 0)),
                  pl.BlockSpec((tm, LANES), lambda i: (i % per_seq, 0))],
        out_specs=[pl.BlockSpec((tm, hd), lambda i: (i, 0)),
                   pl.BlockSpec((tm, hd), lambda i: (i, 0)),
                   pl.BlockSpec((tm, MLA_HEADS * MLA_V), lambda i: (i, 0))],
        out_shape=[jax.ShapeDtypeStruct((n, hd), BF16),
                   jax.ShapeDtypeStruct((n, hd), BF16),
                   jax.ShapeDtypeStruct((n, MLA_HEADS * MLA_V), BF16)],
        compiler_params=_params("parallel"),
        name="mla_proj",
    )(h, win, qn, kvn, wqa, wqb, wkv, cos_p, sin_p)


def _mla_attn_kernel(q_ref, k_ref, v_ref, o_ref):
    v = v_ref[...]
    lane = lax.broadcasted_iota(jnp.int32, o_ref.shape, 1)
    halves = []
    for half in range(2):
        cols = slice(half * LANES, (half + 1) * LANES)
        s = _dot_nt(q_ref[:, cols], k_ref[:, cols])
        m = jnp.max(s, axis=-1, keepdims=True)
        e = jnp.exp(s - m)
        l = jnp.sum(e, axis=-1, keepdims=True)
        halves.append(_dot(e.astype(BF16), v) / l)
    o_ref[...] = jnp.where(lane < MLA_V, halves[0], halves[1]).astype(o_ref.dtype)


def mla_attention(q, k, v, batch, seq, tq=256):
    hd = MLA_HEADS * LANES
    q3 = q.reshape(batch, seq, hd)
    k3 = k.reshape(batch, seq, hd)
    v3 = v.reshape(batch, seq, MLA_HEADS * MLA_V)
    out = pl.pallas_call(
        _mla_attn_kernel,
        grid=(batch, MLA_HEADS // 2, seq // tq),
        in_specs=[pl.BlockSpec((None, tq, 2 * LANES), lambda b, p, i: (b, i, p)),
                  pl.BlockSpec((None, seq, 2 * LANES), lambda b, p, i: (b, 0, p)),
                  pl.BlockSpec((None, seq, LANES), lambda b, p, i: (b, 0, p))],
        out_specs=pl.BlockSpec((None, tq, LANES), lambda b, p, i: (b, i, p)),
        out_shape=jax.ShapeDtypeStruct((batch, seq, MLA_HEADS * MLA_V), BF16),
        compiler_params=_params("parallel", "parallel", "arbitrary"),
        name="mla_attn",
    )(q3, k3, v3)
    return out.reshape(batch * seq, MLA_HEADS * MLA_V)


HG_LEVELS = int(math.log2(HG_CHUNK))
HG_UNROLL = 1


def _hg_constants(reverse):
    c = HG_CHUNK
    t = np.arange(c)[:, None]
    u = np.arange(c)[None, :]
    if not reverse:
        incl = u <= t
        rest = u > t
    else:
        incl = u >= t
        rest = u < t
    mats = [incl, rest]
    masks = []
    roles = []
    for lvl in range(1, HG_LEVELS + 1):
        size = 1 << lvl
        start = (t // size) * size
        mid = start + size // 2
        upper = t >= mid
        if not reverse:
            q_side = (u >= mid) & (u <= t)
            k_side = (u > t) & (u <= mid - 1)
            is_query = upper
        else:
            q_side = (u >= t) & (u < mid)
            k_side = (u >= mid) & (u < t)
            is_query = ~upper
        mats.append(np.where(is_query, q_side, k_side))
        same = (t // size) == (u // size)
        key_row = (~is_query).T
        masks.append(same & is_query & np.broadcast_to(key_row, (c, c)))
        roles.append(np.broadcast_to(is_query, (c, LANES)))
    w = np.concatenate(mats, axis=0).astype(np.float32)
    return (jnp.asarray(w, BF16), jnp.asarray(np.stack(masks).astype(np.float32)),
            jnp.asarray(np.stack(roles).astype(np.float32)))


def _hg_chunk(q, zf, v_b, lb, w_ref, m_ref, r_ref, st, total_row):
    c = HG_CHUNK
    sg = jax.nn.sigmoid(zf)
    k = (1.0 - lb) * jax.nn.sigmoid(-zf)
    lf = jnp.log(lb + (1.0 - lb) * sg)
    hi = lf.astype(BF16)
    lo = (lf - hi.astype(F32)).astype(BF16)
    f = lf.shape[1]
    ex2 = _dot(w_ref[...], jnp.concatenate([hi, lo], axis=1))
    ex = ex2[:, :f] + ex2[:, f:]
    b_incl = ex[0:c]
    total = ex[total_row:total_row + 1]
    qd = q * jnp.exp(b_incl)
    kd = k * jnp.exp(ex[c:2 * c])
    row = lax.broadcasted_iota(jnp.int32, (c, c), 0)
    col = lax.broadcasted_iota(jnp.int32, (c, c), 1)
    a = jnp.where(row == col, jnp.sum(q * k, axis=-1, keepdims=True), 0.0)
    xs = []
    for lvl in range(HG_LEVELS):
        x = jnp.where(r_ref[lvl] > 0.5, q, k) * jnp.exp(ex[(2 + lvl) * c:(3 + lvl) * c])
        xs.append(x.astype(BF16))
    for lvl in range(0, HG_LEVELS, 2):
        pair = xs[lvl:lvl + 2]
        xb = jnp.concatenate(pair, axis=0)
        g = _dot_nt(xb, xb)
        for j in range(len(pair)):
            a = a + m_ref[lvl + j] * g[j * c:(j + 1) * c, j * c:(j + 1) * c]
    o = _dot(a.astype(BF16), v_b) + _dot_nt(qd.astype(BF16), st.astype(BF16))
    return o, st * jnp.exp(total) + _dot_tn(v_b, kd.astype(BF16))


def _hg_kernel(zq_ref, zff_ref, zfb_ref, zi_ref, zg_ref, lb_ref, g_ref,
               wf_ref, mf_ref, rf_ref, wb_ref, mb_ref, rb_ref,
               o_ref, accf_ref, accb_ref):
    c = HG_CHUNK
    seq = zq_ref.shape[0]
    n = seq // c
    lb_f = lb_ref[0:1, :]
    lb_b = lb_ref[1:2, :]

    def gated(ref, rows):
        z = ref[rows, :]
        return z * jax.nn.sigmoid(z)

    def body(i, carry):
        st_f, st_b = carry
        rf = pl.ds(pl.multiple_of(i * c, c), c)
        rb = pl.ds(pl.multiple_of((n - 1 - i) * c, c), c)
        of, st_f = _hg_chunk(gated(zq_ref, rf), zff_ref[rf, :], zi_ref[rf, :].astype(BF16), lb_f,
                             wf_ref, mf_ref, rf_ref, st_f, c - 1)
        accf_ref[rf, :] = of
        ob, st_b = _hg_chunk(gated(zq_ref, rb), zfb_ref[rb, :], zi_ref[rb, :].astype(BF16), lb_b,
                             wb_ref, mb_ref, rb_ref, st_b, 0)
        accb_ref[rb, :] = ob
        return st_f, st_b

    zero = jnp.zeros((HG_F, HG_F), F32)
    lax.fori_loop(0, n, body, (zero, zero), unroll=HG_UNROLL)

    def finish(i, carry):
        rows = pl.ds(pl.multiple_of(i * c, c), c)
        o = accf_ref[rows, :] + accb_ref[rows, :]
        o = o * lax.rsqrt(jnp.mean(o * o, axis=-1, keepdims=True) + NORM_EPS)
        o_ref[rows, :] = (o * g_ref[...] * gated(zg_ref, rows)).astype(o_ref.dtype)
        return carry

    lax.fori_loop(0, n, finish, 0)


def hgrn2_scan(z, lb, norm_g, batch, seq):
    d = D_MODEL
    z3 = z.reshape(batch, seq, 5 * d)
    wf, mf, rf = _hg_constants(False)
    wb, mb, rb = _hg_constants(True)
    zspec = lambda j: pl.BlockSpec((None, seq, HG_F), lambda b, h: (b, 0, j * HG_HEADS + h))
    full = lambda a: pl.BlockSpec(a.shape, lambda b, h: (0,) * a.ndim)
    out = pl.pallas_call(
        _hg_kernel,
        grid=(batch, HG_HEADS),
        in_specs=[zspec(0), zspec(1), zspec(2), zspec(3), zspec(4),
                  pl.BlockSpec((2, HG_F), lambda b, h: (0, h)),
                  pl.BlockSpec((1, HG_F), lambda b, h: (0, h)),
                  full(wf), full(mf), full(rf), full(wb), full(mb), full(rb)],
        out_specs=pl.BlockSpec((None, seq, HG_F), lambda b, h: (b, 0, h)),
        out_shape=jax.ShapeDtypeStruct((batch, seq, d), BF16),
        scratch_shapes=[pltpu.VMEM((seq, HG_F), F32), pltpu.VMEM((seq, HG_F), F32)],
        compiler_params=_params("parallel", "parallel"),
        name="hgrn2_scan",
    )(z3, z3, z3, z3, z3, lb.astype(F32), norm_g.reshape(1, d).astype(F32),
      wf, mf, rf, wb, mb, rb)
    return out.reshape(batch * seq, d)


def _merge_sort_network(n):
    pairs = []
    p = 1
    while p < n:
        k = p
        while k >= 1:
            for j in range(k % p, n - k, 2 * k):
                for i in range(min(k, n - j - k)):
                    if (i + j) // (2 * p) == (i + j + k) // (2 * p):
                        pairs.append((i + j, i + j + k))
            k //= 2
        p *= 2
    return pairs


def _top_values(s, count):
    rows = s.shape[0]
    n = rows // F32_SUBLANES
    size = 1 << (n - 1).bit_length()
    neg = jnp.full((F32_SUBLANES, s.shape[1]), -jnp.inf, F32)
    lists = [s[i * F32_SUBLANES:(i + 1) * F32_SUBLANES] for i in range(n)] + [neg] * (size - n)
    for i, j in _merge_sort_network(size):
        hi = jnp.maximum(lists[i], lists[j])
        lists[j] = jnp.minimum(lists[i], lists[j])
        lists[i] = hi
    lists = lists[:min(size, count)]
    vals = []
    for it in range(count):
        head = lists[0]
        m = jnp.max(head, axis=0, keepdims=True)
        vals.append(m)
        hit = head >= m
        depth = min(len(lists), count - it - 1)
        for k in range(depth):
            nxt = lists[k + 1] if k + 1 < len(lists) else neg
            lists[k] = jnp.where(hit, nxt, lists[k])
        lists = lists[:max(depth, 1)]
    return jnp.concatenate(vals, axis=0)


def _peer_route_kernel(ht_ref, wq_ref, keys_ref, cnt_ref, a0_ref, r1_ref, b1_ref):
    kk = PEER_TOPK
    ht = ht_ref[...]
    for h in range(PEER_HEADS):
        scores = []
        for c in range(2):
            g = 2 * h + c
            qt = _dot(wq_ref[g * LANES:(g + 1) * LANES, :], ht)
            scores.append(_dot(keys_ref[c], qt.astype(BF16)))
        s0, s1 = scores
        sv0 = _top_values(s0, kk)
        sv1 = _top_values(s1, kk)
        r1 = jnp.full(s1.shape, float(kk), F32)
        for b in range(kk):
            r1 = jnp.where(s1 == sv1[b:b + 1], float(b), r1)
        cands = [sv0[a:a + 1] + sv1[0:kk // (a + 1)] for a in range(kk)]
        n_cand = sum(kk // (a + 1) for a in range(kk))
        pad = (-n_cand) % 8
        if pad:
            cands.append(jnp.full((pad, ht.shape[1]), -jnp.inf, F32))
        tau = _top_values(jnp.concatenate(cands, axis=0), kk)[kk - 1:kk]
        e0 = jnp.exp(sv0 - sv0[0:1])
        e1 = jnp.exp(sv1 - sv1[0:1])
        z = jnp.zeros_like(tau)
        for a in range(kk):
            nb = kk // (a + 1)
            sel = (sv0[a:a + 1] + sv1[0:nb]) >= tau
            z = z + e0[a:a + 1] * jnp.sum(jnp.where(sel, e1[0:nb], 0.0), axis=0, keepdims=True)
        cnt = jnp.zeros(s0.shape, F32)
        for a in range(kk):
            pair_ok = (sv0[a:a + 1] + sv1) >= tau
            n_a = jnp.sum(jnp.where(pair_ok, 1.0, 0.0), axis=0, keepdims=True)
            cnt = jnp.where(s0 == sv0[a:a + 1], n_a, cnt)
        cnt_ref[h] = cnt
        a0_ref[h] = jnp.exp(s0 - sv0[0:1])
        r1_ref[h] = r1.astype(BF16)
        b1_ref[h] = (jnp.exp(s1 - sv1[0:1]) * (0.5 / z)).astype(BF16)


def peer_route(ht, wq_t, keys, tt=256):
    d, n = ht.shape
    shape = (PEER_HEADS, PEER_NKEYS, n)
    ospec = pl.BlockSpec((PEER_HEADS, PEER_NKEYS, tt), lambda i: (0, 0, i))
    return pl.pallas_call(
        _peer_route_kernel,
        grid=(n // tt,),
        in_specs=[pl.BlockSpec((d, tt), lambda i: (0, i)),
                  pl.BlockSpec(wq_t.shape, lambda i: (0, 0)),
                  pl.BlockSpec(keys.shape, lambda i: (0, 0, 0))],
        out_specs=[ospec, ospec, ospec, ospec],
        out_shape=[jax.ShapeDtypeStruct(shape, F32), jax.ShapeDtypeStruct(shape, F32),
                   jax.ShapeDtypeStruct(shape, BF16), jax.ShapeDtypeStruct(shape, BF16)],
        compiler_params=_params("parallel"),
        name="peer_route",
    )(ht, wq_t, keys)


def _peer_gate_chunk(hid, i0, cnt_ref, a0_ref, r1_ref, b1_ref):
    tt = hid.shape[1]
    zero = jnp.zeros((PEER_NKEYS, LANES), BF16)

    def row_tile(ref, h, ii, cols):
        row = jnp.broadcast_to(ref[h, ii:ii + 1, cols], (BF16_SUBLANES, LANES)).astype(BF16)
        return pltpu.repeat(row, PEER_NKEYS // BF16_SUBLANES, axis=0)

    out_rows = []
    for k in range(PEER_CHUNK_ROWS):
        tiles = []
        for tc in range(tt // LANES):
            cols = slice(tc * LANES, (tc + 1) * LANES)
            gate = zero
            for h in range(PEER_HEADS):
                cnt = row_tile(cnt_ref, h, i0 + k, cols)
                a0 = row_tile(a0_ref, h, i0 + k, cols)
                gate = gate + jnp.where(r1_ref[h, :, cols] < cnt, a0 * b1_ref[h, :, cols], zero)
            x = hid[k * PEER_NKEYS:(k + 1) * PEER_NKEYS, cols]
            act = x.astype(BF16) * (1.0 + lax.erf(x * RSQRT2)).astype(BF16)
            tiles.append(gate * act)
        out_rows.append(jnp.concatenate(tiles, axis=1))
    return jnp.concatenate(out_rows, axis=0)


def _ordered_after(x, dep):
    z = pltpu.bitcast(dep[:BF16_SUBLANES, :LANES], jnp.uint32)
    z = (z >> 16) >> 16
    zero = pltpu.bitcast(z, BF16)
    zero = pltpu.repeat(pltpu.repeat(zero, x.shape[0] // BF16_SUBLANES, axis=0), x.shape[1] // LANES, axis=1)
    return x + zero


def _peer_ffn_kernel(ht_ref, u0_ref, ub_ref, un_ref, vta_ref, vtb_ref, cnta_ref, a0a_ref, cntb_ref, a0b_ref,
                     r1_in_ref, b1_in_ref, res_ref, g_ref, b_ref,
                     o_ref, acc_ref, hida_ref, hidb_ref, r1_ref, b1_ref):
    s = pl.program_id(1)
    half = ht_ref.shape[1] // 2

    @pl.when(s == 0)
    def _():
        acc_ref[...] = jnp.zeros_like(acc_ref)
        r1_ref[...] = r1_in_ref[...]
        b1_ref[...] = b1_in_ref[...]
        hida_ref[...] = _dot(u0_ref[...], ht_ref[...])

    chunk = PEER_CHUNK_ROWS * PEER_NKEYS
    n_chunks = PEER_EXPERT_BLOCK // chunk

    def run_block(acc, hid_ref, vt_ref, cnt_ref, a0_ref, next_u_ref, next_hid_ref):
        for c in range(n_chunks):
            span = slice(c * chunk, (c + 1) * chunk)
            w = _peer_gate_chunk(hid_ref[span, :], c * PEER_CHUNK_ROWS, cnt_ref, a0_ref, r1_ref, b1_ref)
            acc = acc + _dot(vt_ref[:, span], w)
            if c % (n_chunks // 2) == n_chunks // 2 - 1:
                hh = c // (n_chunks // 2)
                tcols = slice(hh * half, (hh + 1) * half)
                next_hid_ref[:, tcols] = _dot(next_u_ref[...], _ordered_after(ht_ref[:, tcols], w))
        return acc

    acc = run_block(acc_ref[...], hida_ref, vta_ref, cnta_ref, a0a_ref, ub_ref, hidb_ref)
    acc_ref[...] = run_block(acc, hidb_ref, vtb_ref, cntb_ref, a0b_ref, un_ref, hida_ref)

    @pl.when(s == pl.num_programs(1) - 1)
    def _():
        y = acc_ref[...].T
        o_ref[...] = _layer_norm_rows(DN_ALPHA * res_ref[...] + y, g_ref[...], b_ref[...])


def peer_ffn(h, ht, route, u, v_blocks, layer, g, b, tt=512):
    n, d = h.shape
    eb = PEER_EXPERT_BLOCK
    last = u.shape[1] // eb - 1
    rspec = pl.BlockSpec((PEER_HEADS, PEER_NKEYS, tt), lambda i, s: (0, 0, i))
    rowspec = lambda off: pl.BlockSpec((PEER_HEADS, eb // PEER_NKEYS, tt), lambda i, s: (0, 2 * s + off, i))
    cnt, a0, r1, b1 = route
    return pl.pallas_call(
        _peer_ffn_kernel,
        grid=(n // tt, (last + 1) // 2),
        in_specs=[pl.BlockSpec((d, tt), lambda i, s: (0, i)),
                  pl.BlockSpec((None, eb, d), lambda i, s: (layer, 0, 0)),
                  pl.BlockSpec((None, eb, d), lambda i, s: (layer, 2 * s + 1, 0)),
                  pl.BlockSpec((None, eb, d), lambda i, s: (layer, jnp.minimum(2 * s + 2, last), 0)),
                  pl.BlockSpec((None, None, d, eb), lambda i, s: (layer, 2 * s, 0, 0)),
                  pl.BlockSpec((None, None, d, eb), lambda i, s: (layer, 2 * s + 1, 0, 0)),
                  rowspec(0), rowspec(0), rowspec(1), rowspec(1),
                  rspec, rspec,
                  pl.BlockSpec((tt, d), lambda i, s: (i, 0)),
                  pl.BlockSpec((1, d), lambda i, s: (0, 0)),
                  pl.BlockSpec((1, d), lambda i, s: (0, 0))],
        out_specs=pl.BlockSpec((tt, d), lambda i, s: (i, 0)),
        out_shape=jax.ShapeDtypeStruct((n, d), F32),
        scratch_shapes=[pltpu.VMEM((d, tt), F32), pltpu.VMEM((eb, tt), F32), pltpu.VMEM((eb, tt), F32),
                        pltpu.VMEM((PEER_HEADS, PEER_NKEYS, tt), BF16),
                        pltpu.VMEM((PEER_HEADS, PEER_NKEYS, tt), BF16)],
        compiler_params=_params("parallel", "arbitrary"),
        name="peer_ffn",
    )(ht, u, u, u, v_blocks, v_blocks, cnt, a0, cnt, a0, r1, b1, h, g.reshape(1, d), b.reshape(1, d))


def kernel(x, na_w_in, na_rel_bias, na_w_out, mla_w_in, mla_q_norm, mla_kv_norm, mla_w_q_up, mla_w_kv_up, mla_w_out, hg_w_in, hg_lower_bound, hg_norm, hg_w_out, peer_w_q, peer_sub_keys, peer_u, peer_v, ln_mix_g, ln_mix_b, ln_ffn_g, ln_ffn_b):
    batch, seq, d = x.shape
    rows = seq // GRID_W
    lb_w = jax.nn.softmax(hg_lower_bound.astype(F32), axis=0)
    lb_all = jnp.cumsum(lb_w, axis=0) - lb_w[0:1]
    h = x.reshape(batch * seq, d)
    u_all = peer_u.astype(BF16)
    v_all = peer_v.reshape(DEPTH, -1, PEER_EXPERT_BLOCK, d).transpose(0, 1, 3, 2).astype(BF16)
    for layer in range(DEPTH):
        kind = layer % N_MIXERS
        j = layer // N_MIXERS
        if kind == 0:
            qk, vt = na_project(h, na_w_in[j])
            mix_in = neighborhood_attention(qk, vt, na_rel_bias[j], batch, rows)
            w_out = na_w_out[j]
        elif kind == 1:
            q, k, v = mla_projections(h, mla_w_in[j], mla_q_norm[j], mla_kv_norm[j],
                                      mla_w_q_up[j], mla_w_kv_up[j], seq)
            mix_in = mla_attention(q, k, v, batch, seq)
            w_out = mla_w_out[j]
        else:
            z = matmul(h, hg_w_in[j].astype(BF16), F32)
            mix_in = hgrn2_scan(z, lb_all[layer], hg_norm[j], batch, seq)
            w_out = hg_w_out[j]
        h, ht = matmul_res_ln(mix_in, w_out.astype(BF16), h, ln_mix_g[layer], ln_mix_b[layer])
        route = peer_route(ht, peer_w_q[layer].T.astype(BF16), peer_sub_keys[layer].astype(BF16))
        h = peer_ffn(h, ht, route, u_all, v_all, layer, ln_ffn_g[layer], ln_ffn_b[layer])
    return h.reshape(batch, seq, d)
```

```python
import functools
import math

import jax
import jax.numpy as jnp
import numpy as np
from jax import lax
from jax.experimental import pallas as pl
from jax.experimental.pallas import tpu as pltpu

F32 = jnp.float32
BF16 = jnp.bfloat16

D_MODEL = 1024
DEPTH = 4
GRID_W = 64
N_MIXERS = 3

NA_HEADS = 16
NA_HEAD_DIM = 64
NA_KH = 8
NA_KW = 16

MLA_HEADS = 16
MLA_NOPE = 64
MLA_ROPE = 32
MLA_V = 64
MLA_Q_RANK = 256
MLA_KV_RANK = 256
ROPE_THETA = 10000.0

HG_HEADS = 8
HG_F = 128
HG_CHUNK = 128

PEER_HEADS = 8
PEER_NKEYS = 128
PEER_TOPK = 16
PEER_EXPERT_BLOCK = 1024
PEER_CHUNK_ROWS = 2
NORM_EPS = 1e-5
DN_ALPHA = (2.0 * DEPTH) ** 0.25

LANES = 128
F32_SUBLANES = 8
BF16_SUBLANES = 16
VMEM_LIMIT = 48 * 1024 * 1024
MASK_NEG = -1e30
RSQRT2 = 0.7071067811865476


def _params(*sem):
    return pltpu.CompilerParams(dimension_semantics=sem, vmem_limit_bytes=VMEM_LIMIT)


def _layer_norm_rows(z, g, b):
    mu = jnp.mean(z, axis=-1, keepdims=True)
    zc = z - mu
    var = jnp.mean(zc * zc, axis=-1, keepdims=True)
    return zc * lax.rsqrt(var + NORM_EPS) * g + b


def _dot(a, b):
    return jnp.dot(a, b, preferred_element_type=F32)


def _dot_nt(a, b):
    return lax.dot_general(a, b, (((1,), (1,)), ((), ())), preferred_element_type=F32)


def _dot_tn(a, b):
    return lax.dot_general(a, b, (((0,), (0,)), ((), ())), preferred_element_type=F32)


def _mm_kernel(a_ref, w_ref, o_ref):
    o_ref[...] = _dot(a_ref[...].astype(BF16), w_ref[...]).astype(o_ref.dtype)


def matmul(a, w, out_dtype, tm=512, tn=1024):
    m, k = a.shape
    n = w.shape[1]
    tn = min(tn, n)
    return pl.pallas_call(
        _mm_kernel,
        grid=(m // tm, n // tn),
        in_specs=[pl.BlockSpec((tm, k), lambda i, j: (i, 0)),
                  pl.BlockSpec((k, tn), lambda i, j: (0, j))],
        out_specs=pl.BlockSpec((tm, tn), lambda i, j: (i, j)),
        out_shape=jax.ShapeDtypeStruct((m, n), out_dtype),
        compiler_params=_params("parallel", "parallel"),
        name="proj",
    )(a, w)


def _mm_res_ln_kernel(a_ref, w_ref, res_ref, g_ref, b_ref, o_ref, ot_ref):
    y = _dot(a_ref[...], w_ref[...])
    out = _layer_norm_rows(DN_ALPHA * res_ref[...] + y, g_ref[...], b_ref[...])
    o_ref[...] = out
    ot_ref[...] = out.T.astype(BF16)


def matmul_res_ln(a, w, res, g, b, tm=256):
    m, k = a.shape
    n = w.shape[1]
    return pl.pallas_call(
        _mm_res_ln_kernel,
        grid=(m // tm,),
        in_specs=[pl.BlockSpec((tm, k), lambda i: (i, 0)),
                  pl.BlockSpec((k, n), lambda i: (0, 0)),
                  pl.BlockSpec((tm, n), lambda i: (i, 0)),
                  pl.BlockSpec((1, n), lambda i: (0, 0)),
                  pl.BlockSpec((1, n), lambda i: (0, 0))],
        out_specs=[pl.BlockSpec((tm, n), lambda i: (i, 0)),
                   pl.BlockSpec((n, tm), lambda i: (0, i))],
        out_shape=[jax.ShapeDtypeStruct((m, n), F32),
                   jax.ShapeDtypeStruct((n, m), BF16)],
        compiler_params=_params("parallel"),
        name="out_proj_ln",
    )(a, w, res, g.reshape(1, n), b.reshape(1, n))


def _na_row_start(r, rows):
    return jnp.clip(r - NA_KH // 2, 0, rows - NA_KH)


def _na_proj_kernel(h_ref, wqk_ref, wvt_ref, qk_ref, vt_ref):
    hb = h_ref[...].astype(BF16)
    qk_ref[...] = _dot(hb, wqk_ref[...]).astype(qk_ref.dtype)
    vt_ref[...] = _dot_nt(wvt_ref[...], hb).astype(vt_ref.dtype)


def na_project(h, w_in, tm=512):
    n, d = h.shape
    wqk = w_in[:, :2 * d].astype(BF16)
    wvt = w_in[:, 2 * d:].T.astype(BF16)
    return pl.pallas_call(
        _na_proj_kernel,
        grid=(n // tm,),
        in_specs=[pl.BlockSpec((tm, d), lambda i: (i, 0)),
                  pl.BlockSpec((d, 2 * d), lambda i: (0, 0)),
                  pl.BlockSpec((d, d), lambda i: (0, 0))],
        out_specs=[pl.BlockSpec((tm, 2 * d), lambda i: (i, 0)),
                   pl.BlockSpec((d, tm), lambda i: (0, i))],
        out_shape=[jax.ShapeDtypeStruct((n, 2 * d), BF16), jax.ShapeDtypeStruct((d, n), BF16)],
        compiler_params=_params("parallel"),
        name="na_proj",
    )(h, wqk, wvt)


NA_QROWS = 2
NA_KROWS = NA_KH + NA_QROWS
NA_EDGE = NA_KH // 4


def _na_key_start(i, rows):
    return jnp.clip(i - NA_KH // 4, 0, (rows - NA_KROWS) // 2)


def _na_kernel(q_ref, k_ref, vt_ref, bias_ref, o_ref):
    n_keys = NA_KROWS * GRID_W
    n_q = NA_QROWS * GRID_W
    lane = lax.broadcasted_iota(jnp.int32, (n_q, LANES), 1)
    low = lane < NA_HEAD_DIM
    row_low = lax.broadcasted_iota(jnp.int32, (LANES, n_q), 0) < NA_HEAD_DIM
    for p in range(NA_HEADS // 2):
        cols = slice(p * LANES, (p + 1) * LANES)
        qp = q_ref[0, :, :, cols].reshape(n_q, LANES)
        zq = jnp.zeros_like(qp)
        kp = k_ref[0, :, :, cols].reshape(n_keys, LANES)
        vtp = vt_ref[cols, :]
        outs = []
        for half in range(2):
            qh = jnp.where(low, qp, zq) if half == 0 else jnp.where(low, zq, qp)
            s = _dot_nt(kp, qh) * (NA_HEAD_DIM ** -0.5) + bias_ref[0, 2 * p + half]
            m = jnp.max(s, axis=0, keepdims=True)
            e = jnp.exp(s - m)
            l = jnp.sum(e, axis=0, keepdims=True)
            outs.append(_dot(vtp, e.astype(BF16)) / l)
        o2 = jnp.where(row_low, outs[0], outs[1]).T.astype(o_ref.dtype)
        for a in range(NA_QROWS):
            o_ref[a, :, cols] = o2[a * GRID_W:(a + 1) * GRID_W]


def _na_bias_table(rel_bias, rows):
    cols = np.arange(GRID_W)
    c0 = np.clip(cols - NA_KW // 2, 0, GRID_W - NA_KW)
    kc = np.arange(GRID_W)
    inside = (kc[None, :] >= c0[:, None]) & (kc[None, :] < c0[:, None] + NA_KW)
    dc = np.clip(kc[None, :] - cols[:, None] + (NA_KW - 1), 0, 2 * NA_KW - 2)
    b2 = jnp.where(inside[None, None], rel_bias[:, :, dc].astype(F32), MASK_NEG)
    steps = rows // NA_QROWS
    variant_steps = list(range(NA_EDGE)) + [NA_EDGE] + list(range(steps - NA_EDGE, steps))
    dr = np.zeros((len(variant_steps), NA_QROWS, NA_KROWS), np.int32)
    ok = np.zeros(dr.shape, bool)
    for v, i in enumerate(variant_steps):
        key0 = 2 * int(np.clip(i - NA_KH // 4, 0, (rows - NA_KROWS) // 2))
        for a in range(NA_QROWS):
            r = NA_QROWS * i + a
            r0 = int(np.clip(r - NA_KH // 2, 0, rows - NA_KH))
            key_rows = key0 + np.arange(NA_KROWS)
            ok[v, a] = (key_rows >= r0) & (key_rows < r0 + NA_KH)
            dr[v, a] = np.clip(key_rows - r + NA_KH - 1, 0, 2 * NA_KH - 2)
    t = b2[:, dr]
    t = jnp.where(ok[None, :, :, :, None, None], t, MASK_NEG)
    t = t.transpose(1, 0, 3, 5, 2, 4)
    return t.reshape(len(variant_steps), NA_HEADS, NA_KROWS * GRID_W, NA_QROWS * GRID_W)


def neighborhood_attention(qk, vt, rel_bias, batch, rows):
    d = D_MODEL
    steps = rows // NA_QROWS
    n_keys = NA_KROWS * GRID_W
    qk4 = qk.reshape(batch, rows, GRID_W, 2 * d)
    bias = _na_bias_table(rel_bias, rows)
    el = pl.Element
    key0 = lambda i: _na_key_start(i, rows)

    def variant(i):
        return jnp.where(i < NA_EDGE, i, jnp.where(i >= steps - NA_EDGE, i - (steps - 2 * NA_EDGE - 1), NA_EDGE))

    out = pl.pallas_call(
        _na_kernel,
        grid=(batch, steps),
        in_specs=[pl.BlockSpec((el(1), el(NA_QROWS), el(GRID_W), el(d)),
                               lambda b, i: (b, NA_QROWS * i, 0, 0)),
                  pl.BlockSpec((el(1), el(NA_KROWS), el(GRID_W), el(d)),
                               lambda b, i: (b, 2 * key0(i), 0, d)),
                  pl.BlockSpec((el(d), el(n_keys)),
                               lambda b, i: (0, (b * (rows // 2) + key0(i)) * (2 * GRID_W))),
                  pl.BlockSpec((1, NA_HEADS, n_keys, NA_QROWS * GRID_W),
                               lambda b, i: (variant(i), 0, 0, 0))],
        out_specs=pl.BlockSpec((None, NA_QROWS, GRID_W, d), lambda b, i: (b, i, 0, 0)),
        out_shape=jax.ShapeDtypeStruct((batch, rows, GRID_W, d), BF16),
        compiler_params=_params("parallel", "arbitrary"),
        name="na_attn",
    )(qk4, qk4, vt, bias)
    return out.reshape(batch * rows * GRID_W, d)


def _rms_rows(x, g):
    return x * lax.rsqrt(jnp.mean(x * x, axis=-1, keepdims=True) + NORM_EPS) * g


def _mla_proj_kernel(x_ref, win_ref, qn_ref, kvn_ref, wqa_ref, wqb_ref, wkv_ref, cos_ref, sin_ref,
                     q_ref, k_ref, v_ref):
    hd = MLA_HEADS * LANES
    hin = _dot(x_ref[...].astype(BF16), win_ref[...])
    cq = _rms_rows(hin[:, :MLA_Q_RANK], qn_ref[...]).astype(BF16)
    ckv = _rms_rows(hin[:, MLA_Q_RANK:MLA_Q_RANK + MLA_KV_RANK], kvn_ref[...]).astype(BF16)
    cos = cos_ref[...]
    sin = sin_ref[...]
    cos_t = jnp.tile(cos, (1, MLA_HEADS))
    sin_t = jnp.tile(sin, (1, MLA_HEADS))
    q = _dot(cq, wqa_ref[...]) * cos_t + _dot(cq, wqb_ref[...]) * sin_t
    q_ref[...] = (q * ((MLA_NOPE + MLA_ROPE) ** -0.5)).astype(q_ref.dtype)
    kv = _dot(ckv, wkv_ref[...])
    base = MLA_Q_RANK + MLA_KV_RANK
    kpe = hin[:, base:base + LANES] * cos + hin[:, base + LANES:base + 2 * LANES] * sin
    k_ref[...] = (kv[:, :hd] + jnp.tile(kpe, (1, MLA_HEADS))).astype(k_ref.dtype)
    v_ref[...] = kv[:, hd:].astype(v_ref.dtype)


def _mla_weights(w_in, w_q_up, w_kv_up):
    r = MLA_ROPE
    half = r // 2
    dq = MLA_NOPE + r
    nh = MLA_HEADS
    base = MLA_Q_RANK + MLA_KV_RANK
    kpe = w_in[:, base:base + r]
    zpad = lambda w, lo, hi: jnp.pad(w, ((0, 0), (lo, hi)))
    swap = lambda w: jnp.concatenate([-w[..., half:], w[..., :half]], axis=-1)
    kpe_a = zpad(kpe, MLA_NOPE, LANES - MLA_NOPE - r)
    kpe_b = zpad(swap(kpe), MLA_NOPE, LANES - MLA_NOPE - r)
    win = jnp.concatenate([w_in[:, :base], kpe_a, kpe_b], axis=1)
    wq = w_q_up.reshape(MLA_Q_RANK, nh, dq)
    pad3 = lambda w, lo, hi: jnp.pad(w, ((0, 0), (0, 0), (lo, hi)))
    wqa = pad3(wq, 0, LANES - dq).reshape(MLA_Q_RANK, nh * LANES)
    wqb = pad3(swap(wq[:, :, MLA_NOPE:]), MLA_NOPE, LANES - dq).reshape(MLA_Q_RANK, nh * LANES)
    wkv = w_kv_up.reshape(MLA_KV_RANK, nh, MLA_NOPE + MLA_V)
    wk = pad3(wkv[:, :, :MLA_NOPE], 0, LANES - MLA_NOPE).reshape(MLA_KV_RANK, nh * LANES)
    wv = wkv[:, :, MLA_NOPE:].reshape(MLA_KV_RANK, nh * MLA_V)
    return (win.astype(BF16), wqa.astype(BF16), wqb.astype(BF16),
            jnp.concatenate([wk, wv], axis=1).astype(BF16))


def _rope_tables(seq):
    half = MLA_ROPE // 2
    inv_freq = ROPE_THETA ** (-jnp.arange(half, dtype=F32) * 2.0 / MLA_ROPE)
    ang = jnp.arange(seq, dtype=F32)[:, None] * inv_freq[None, :]
    cos = jnp.cos(ang)
    sin = jnp.sin(ang)
    tail = LANES - MLA_NOPE - MLA_ROPE
    cos_p = jnp.concatenate([jnp.ones((seq, MLA_NOPE), F32), cos, cos, jnp.zeros((seq, tail), F32)], axis=1)
    sin_p = jnp.concatenate([jnp.zeros((seq, MLA_NOPE), F32), sin, sin, jnp.zeros((seq, tail), F32)], axis=1)
    return cos_p, sin_p


def mla_projections(h, w_in, q_norm, kv_norm, w_q_up, w_kv_up, seq, tm=256):
    n, d = h.shape
    win, wqa, wqb, wkv = _mla_weights(w_in, w_q_up, w_kv_up)
    cos_p, sin_p = _rope_tables(seq)
    hd = MLA_HEADS * LANES
    per_seq = seq // tm
    full = lambda a: pl.BlockSpec(a.shape, lambda i: (0,) * a.ndim)
    qn = q_norm.reshape(1, -1).astype(F32)
    kvn = kv_norm.reshape(1, -1).astype(F32)
    return pl.pallas_call(
        _mla_proj_kernel,
        grid=(n // tm,),
        in_specs=[pl.BlockSpec((tm, d), lambda i: (i, 0)), full(win), full(qn), full(kvn),
                  full(wqa), full(wqb), full(wkv),
                  pl.BlockSpec((tm, LANES), lambda i: (i % per_seq, 0)),
                  pl.BlockSpec((tm, LANES), lambda i: (i % per_seq, 0))],
        out_specs=[pl.BlockSpec((tm, hd), lambda i: (i, 0)),
                   pl.BlockSpec((tm, hd), lambda i: (i, 0)),
                   pl.BlockSpec((tm, MLA_HEADS * MLA_V), lambda i: (i, 0))],
        out_shape=[jax.ShapeDtypeStruct((n, hd), BF16),
                   jax.ShapeDtypeStruct((n, hd), BF16),
                   jax.ShapeDtypeStruct((n, MLA_HEADS * MLA_V), BF16)],
        compiler_params=_params("parallel"),
        name="mla_proj",
    )(h, win, qn, kvn, wqa, wqb, wkv, cos_p, sin_p)


def _mla_attn_kernel(q_ref, k_ref, v_ref, o_ref):
    v = v_ref[...]
    lane = lax.broadcasted_iota(jnp.int32, o_ref.shape, 1)
    halves = []
    for half in range(2):
        cols = slice(half * LANES, (half + 1) * LANES)
        s = _dot_nt(q_ref[:, cols], k_ref[:, cols])
        m = jnp.max(s, axis=-1, keepdims=True)
        e = jnp.exp(s - m)
        l = jnp.sum(e, axis=-1, keepdims=True)
        halves.append(_dot(e.astype(BF16), v) / l)
    o_ref[...] = jnp.where(lane < MLA_V, halves[0], halves[1]).astype(o_ref.dtype)


def mla_attention(q, k, v, batch, seq, tq=256):
    hd = MLA_HEADS * LANES
    q3 = q.reshape(batch, seq, hd)
    k3 = k.reshape(batch, seq, hd)
    v3 = v.reshape(batch, seq, MLA_HEADS * MLA_V)
    out = pl.pallas_call(
        _mla_attn_kernel,
        grid=(batch, MLA_HEADS // 2, seq // tq),
        in_specs=[pl.BlockSpec((None, tq, 2 * LANES), lambda b, p, i: (b, i, p)),
                  pl.BlockSpec((None, seq, 2 * LANES), lambda b, p, i: (b, 0, p)),
                  pl.BlockSpec((None, seq, LANES), lambda b, p, i: (b, 0, p))],
        out_specs=pl.BlockSpec((None, tq, LANES), lambda b, p, i: (b, i, p)),
        out_shape=jax.ShapeDtypeStruct((batch, seq, MLA_HEADS * MLA_V), BF16),
        compiler_params=_params("parallel", "parallel", "arbitrary"),
        name="mla_attn",
    )(q3, k3, v3)
    return out.reshape(batch * seq, MLA_HEADS * MLA_V)


HG_LEVELS = int(math.log2(HG_CHUNK))
HG_UNROLL = 1


def _hg_constants(reverse):
    c = HG_CHUNK
    t = np.arange(c)[:, None]
    u = np.arange(c)[None, :]
    if not reverse:
        incl = u <= t
        rest = u > t
    else:
        incl = u >= t
        rest = u < t
    mats = [incl, rest]
    masks = []
    roles = []
    for lvl in range(1, HG_LEVELS + 1):
        size = 1 << lvl
        start = (t // size) * size
        mid = start + size // 2
        upper = t >= mid
        if not reverse:
            q_side = (u >= mid) & (u <= t)
            k_side = (u > t) & (u <= mid - 1)
            is_query = upper
        else:
            q_side = (u >= t) & (u < mid)
            k_side = (u >= mid) & (u < t)
            is_query = ~upper
        mats.append(np.where(is_query, q_side, k_side))
        same = (t // size) == (u // size)
        key_row = (~is_query).T
        masks.append(same & is_query & np.broadcast_to(key_row, (c, c)))
        roles.append(np.broadcast_to(is_query, (c, LANES)))
    w = np.concatenate(mats, axis=0).astype(np.float32)
    return (jnp.asarray(w, BF16), jnp.asarray(np.stack(masks).astype(np.float32)),
            jnp.asarray(np.stack(roles).astype(np.float32)))


def _hg_chunk(q, zf, v_b, lb, w_ref, m_ref, r_ref, st, total_row):
    c = HG_CHUNK
    sg = jax.nn.sigmoid(zf)
    k = (1.0 - lb) * jax.nn.sigmoid(-zf)
    lf = jnp.log(lb + (1.0 - lb) * sg)
    hi = lf.astype(BF16)
    lo = (lf - hi.astype(F32)).astype(BF16)
    f = lf.shape[1]
    ex2 = _dot(w_ref[...], jnp.concatenate([hi, lo], axis=1))
    ex = ex2[:, :f] + ex2[:, f:]
    b_incl = ex[0:c]
    total = ex[total_row:total_row + 1]
    qd = q * jnp.exp(b_incl)
    kd = k * jnp.exp(ex[c:2 * c])
    row = lax.broadcasted_iota(jnp.int32, (c, c), 0)
    col = lax.broadcasted_iota(jnp.int32, (c, c), 1)
    a = jnp.where(row == col, jnp.sum(q * k, axis=-1, keepdims=True), 0.0)
    xs = []
    for lvl in range(HG_LEVELS):
        x = jnp.where(r_ref[lvl] > 0.5, q, k) * jnp.exp(ex[(2 + lvl) * c:(3 + lvl) * c])
        xs.append(x.astype(BF16))
    for lvl in range(0, HG_LEVELS, 2):
        pair = xs[lvl:lvl + 2]
        xb = jnp.concatenate(pair, axis=0)
        g = _dot_nt(xb, xb)
        for j in range(len(pair)):
            a = a + m_ref[lvl + j] * g[j * c:(j + 1) * c, j * c:(j + 1) * c]
    o = _dot(a.astype(BF16), v_b) + _dot_nt(qd.astype(BF16), st.astype(BF16))
    return o, st * jnp.exp(total) + _dot_tn(v_b, kd.astype(BF16))


def _hg_kernel(zq_ref, zff_ref, zfb_ref, zi_ref, zg_ref, lb_ref, g_ref,
               wf_ref, mf_ref, rf_ref, wb_ref, mb_ref, rb_ref,
               o_ref, accf_ref, accb_ref):
    c = HG_CHUNK
    seq = zq_ref.shape[0]
    n = seq // c
    lb_f = lb_ref[0:1, :]
    lb_b = lb_ref[1:2, :]

    def gated(ref, rows):
        z = ref[rows, :]
        return z * jax.nn.sigmoid(z)

    def body(i, carry):
        st_f, st_b = carry
        rf = pl.ds(pl.multiple_of(i * c, c), c)
        rb = pl.ds(pl.multiple_of((n - 1 - i) * c, c), c)
        of, st_f = _hg_chunk(gated(zq_ref, rf), zff_ref[rf, :], zi_ref[rf, :].astype(BF16), lb_f,
                             wf_ref, mf_ref, rf_ref, st_f, c - 1)
        accf_ref[rf, :] = of
        ob, st_b = _hg_chunk(gated(zq_ref, rb), zfb_ref[rb, :], zi_ref[rb, :].astype(BF16), lb_b,
                             wb_ref, mb_ref, rb_ref, st_b, 0)
        accb_ref[rb, :] = ob
        return st_f, st_b

    zero = jnp.zeros((HG_F, HG_F), F32)
    lax.fori_loop(0, n, body, (zero, zero), unroll=HG_UNROLL)

    def finish(i, carry):
        rows = pl.ds(pl.multiple_of(i * c, c), c)
        o = accf_ref[rows, :] + accb_ref[rows, :]
        o = o * lax.rsqrt(jnp.mean(o * o, axis=-1, keepdims=True) + NORM_EPS)
        o_ref[rows, :] = (o * g_ref[...] * gated(zg_ref, rows)).astype(o_ref.dtype)
        return carry

    lax.fori_loop(0, n, finish, 0)


def hgrn2_scan(z, lb, norm_g, batch, seq):
    d = D_MODEL
    z3 = z.reshape(batch, seq, 5 * d)
    wf, mf, rf = _hg_constants(False)
    wb, mb, rb = _hg_constants(True)
    zspec = lambda j: pl.BlockSpec((None, seq, HG_F), lambda b, h: (b, 0, j * HG_HEADS + h))
    full = lambda a: pl.BlockSpec(a.shape, lambda b, h: (0,) * a.ndim)
    out = pl.pallas_call(
        _hg_kernel,
        grid=(batch, HG_HEADS),
        in_specs=[zspec(0), zspec(1), zspec(2), zspec(3), zspec(4),
                  pl.BlockSpec((2, HG_F), lambda b, h: (0, h)),
                  pl.BlockSpec((1, HG_F), lambda b, h: (0, h)),
                  full(wf), full(mf), full(rf), full(wb), full(mb), full(rb)],
        out_specs=pl.BlockSpec((None, seq, HG_F), lambda b, h: (b, 0, h)),
        out_shape=jax.ShapeDtypeStruct((batch, seq, d), BF16),
        scratch_shapes=[pltpu.VMEM((seq, HG_F), F32), pltpu.VMEM((seq, HG_F), F32)],
        compiler_params=_params("parallel", "parallel"),
        name="hgrn2_scan",
    )(z3, z3, z3, z3, z3, lb.astype(F32), norm_g.reshape(1, d).astype(F32),
      wf, mf, rf, wb, mb, rb)
    return out.reshape(batch * seq, d)


def _merge_sort_network(n):
    pairs = []
    p = 1
    while p < n:
        k = p
        while k >= 1:
            for j in range(k % p, n - k, 2 * k):
                for i in range(min(k, n - j - k)):
                    if (i + j) // (2 * p) == (i + j + k) // (2 * p):
                        pairs.append((i + j, i + j + k))
            k //= 2
        p *= 2
    return pairs


def _top_values(s, count):
    rows = s.shape[0]
    n = rows // F32_SUBLANES
    size = 1 << (n - 1).bit_length()
    neg = jnp.full((F32_SUBLANES, s.shape[1]), -jnp.inf, F32)
    lists = [s[i * F32_SUBLANES:(i + 1) * F32_SUBLANES] for i in range(n)] + [neg] * (size - n)
    for i, j in _merge_sort_network(size):
        hi = jnp.maximum(lists[i], lists[j])
        lists[j] = jnp.minimum(lists[i], lists[j])
        lists[i] = hi
    lists = lists[:min(size, count)]
    vals = []
    for it in range(count):
        head = lists[0]
        m = jnp.max(head, axis=0, keepdims=True)
        vals.append(m)
        hit = head >= m
        depth = min(len(lists), count - it - 1)
        for k in range(depth):
            nxt = lists[k + 1] if k + 1 < len(lists) else neg
            lists[k] = jnp.where(hit, nxt, lists[k])
        lists = lists[:max(depth, 1)]
    return jnp.concatenate(vals, axis=0)


def _peer_route_kernel(ht_ref, wq_ref, keys_ref, cnt_ref, a0_ref, r1_ref, b1_ref):
    kk = PEER_TOPK
    ht = ht_ref[...]
    for h in range(PEER_HEADS):
        scores = []
        for c in range(2):
            g = 2 * h + c
            qt = _dot(wq_ref[g * LANES:(g + 1) * LANES, :], ht)
            scores.append(_dot(keys_ref[c], qt.astype(BF16)))
        s0, s1 = scores
        sv0 = _top_values(s0, kk)
        sv1 = _top_values(s1, kk)
        r1 = jnp.full(s1.shape, float(kk), F32)
        for b in range(kk):
            r1 = jnp.where(s1 == sv1[b:b + 1], float(b), r1)
        cands = [sv0[a:a + 1] + sv1[0:kk // (a + 1)] for a in range(kk)]
        n_cand = sum(kk // (a + 1) for a in range(kk))
        pad = (-n_cand) % 8
        if pad:
            cands.append(jnp.full((pad, ht.shape[1]), -jnp.inf, F32))
        tau = _top_values(jnp.concatenate(cands, axis=0), kk)[kk - 1:kk]
        e0 = jnp.exp(sv0 - sv0[0:1])
        e1 = jnp.exp(sv1 - sv1[0:1])
        z = jnp.zeros_like(tau)
        for a in range(kk):
            nb = kk // (a + 1)
            sel = (sv0[a:a + 1] + sv1[0:nb]) >= tau
            z = z + e0[a:a + 1] * jnp.sum(jnp.where(sel, e1[0:nb], 0.0), axis=0, keepdims=True)
        cnt = jnp.zeros(s0.shape, F32)
        for a in range(kk):
            pair_ok = (sv0[a:a + 1] + sv1) >= tau
            n_a = jnp.sum(jnp.where(pair_ok, 1.0, 0.0), axis=0, keepdims=True)
            cnt = jnp.where(s0 == sv0[a:a + 1], n_a, cnt)
        cnt_ref[h] = cnt
        a0_ref[h] = jnp.exp(s0 - sv0[0:1])
        r1_ref[h] = r1.astype(BF16)
        b1_ref[h] = (jnp.exp(s1 - sv1[0:1]) * (0.5 / z)).astype(BF16)


def peer_route(ht, wq_t, keys, tt=256):
    d, n = ht.shape
    shape = (PEER_HEADS, PEER_NKEYS, n)
    ospec = pl.BlockSpec((PEER_HEADS, PEER_NKEYS, tt), lambda i: (0, 0, i))
    return pl.pallas_call(
        _peer_route_kernel,
        grid=(n // tt,),
        in_specs=[pl.BlockSpec((d, tt), lambda i: (0, i)),
                  pl.BlockSpec(wq_t.shape, lambda i: (0, 0)),
                  pl.BlockSpec(keys.shape, lambda i: (0, 0, 0))],
        out_specs=[ospec, ospec, ospec, ospec],
        out_shape=[jax.ShapeDtypeStruct(shape, F32), jax.ShapeDtypeStruct(shape, F32),
                   jax.ShapeDtypeStruct(shape, BF16), jax.ShapeDtypeStruct(shape, BF16)],
        compiler_params=_params("parallel"),
        name="peer_route",
    )(ht, wq_t, keys)


def _peer_gate_chunk(hid, i0, cnt_ref, a0_ref, r1_ref, b1_ref):
    tt = hid.shape[1]
    zero = jnp.zeros((PEER_NKEYS, LANES), BF16)

    def row_tile(ref, h, ii, cols):
        row = jnp.broadcast_to(ref[h, ii:ii + 1, cols], (BF16_SUBLANES, LANES)).astype(BF16)
        return pltpu.repeat(row, PEER_NKEYS // BF16_SUBLANES, axis=0)

    out_rows = []
    for k in range(PEER_CHUNK_ROWS):
        tiles = []
        for tc in range(tt // LANES):
            cols = slice(tc * LANES, (tc + 1) * LANES)
            gate = zero
            for h in range(PEER_HEADS):
                cnt = row_tile(cnt_ref, h, i0 + k, cols)
                a0 = row_tile(a0_ref, h, i0 + k, cols)
                gate = gate + jnp.where(r1_ref[h, :, cols] < cnt, a0 * b1_ref[h, :, cols], zero)
            x = hid[k * PEER_NKEYS:(k + 1) * PEER_NKEYS, cols]
            act = x.astype(BF16) * (1.0 + lax.erf(x * RSQRT2)).astype(BF16)
            tiles.append(gate * act)
        out_rows.append(jnp.concatenate(tiles, axis=1))
    return jnp.concatenate(out_rows, axis=0)


def _pack_rows(x):
    x = x.astype(BF16)
    pairs = x.reshape(*x.shape[:-2], x.shape[-2] // 2, 2, x.shape[-1])
    return lax.bitcast_convert_type(jnp.swapaxes(pairs, -1, -2), jnp.uint32)


def _unpack_rows(words):
    return pltpu.bitcast(words, BF16)


def _ordered_after(x, dep):
    z = pltpu.bitcast(dep[:BF16_SUBLANES, :LANES], jnp.uint32)
    z = (z >> 16) >> 16
    zero = pltpu.bitcast(z, BF16)
    zero = pltpu.repeat(pltpu.repeat(zero, x.shape[0] // BF16_SUBLANES, axis=0), x.shape[1] // LANES, axis=1)
    return x + zero


def _peer_ffn_kernel(ht_ref, u0_ref, ub_ref, un_ref, vta_ref, vtb_ref, cnta_ref, a0a_ref, cntb_ref, a0b_ref,
                     r1_in_ref, b1_in_ref, res_ref, g_ref, b_ref,
                     o_ref, acc_ref, hida_ref, hidb_ref, r1_ref, b1_ref):
    s = pl.program_id(1)
    half = ht_ref.shape[1] // 2

    @pl.when(s == 0)
    def _():
        acc_ref[...] = jnp.zeros_like(acc_ref)
        r1_ref[...] = r1_in_ref[...]
        b1_ref[...] = b1_in_ref[...]
        hida_ref[...] = _dot(_unpack_rows(u0_ref[...]), ht_ref[...])

    chunk = PEER_CHUNK_ROWS * PEER_NKEYS
    n_chunks = PEER_EXPERT_BLOCK // chunk

    def run_block(acc, hid_ref, vt_ref, cnt_ref, a0_ref, next_u_ref, next_hid_ref):
        for c in range(n_chunks):
            span = slice(c * chunk, (c + 1) * chunk)
            w = _peer_gate_chunk(hid_ref[span, :], c * PEER_CHUNK_ROWS, cnt_ref, a0_ref, r1_ref, b1_ref)
            acc = acc + _dot(_unpack_rows(vt_ref[:, span]), w)
            if c % (n_chunks // 2) == n_chunks // 2 - 1:
                hh = c // (n_chunks // 2)
                tcols = slice(hh * half, (hh + 1) * half)
                next_hid_ref[:, tcols] = _dot(_unpack_rows(next_u_ref[...]),
                                              _ordered_after(ht_ref[:, tcols], w))
        return acc

    acc = run_block(acc_ref[...], hida_ref, vta_ref, cnta_ref, a0a_ref, ub_ref, hidb_ref)
    acc_ref[...] = run_block(acc, hidb_ref, vtb_ref, cntb_ref, a0b_ref, un_ref, hida_ref)

    @pl.when(s == pl.num_programs(1) - 1)
    def _():
        y = acc_ref[...].T
        o_ref[...] = _layer_norm_rows(DN_ALPHA * res_ref[...] + y, g_ref[...], b_ref[...])


def peer_ffn(h, ht, route, u, v_blocks, layer, g, b, tt=512):
    n, d = h.shape
    eb = PEER_EXPERT_BLOCK
    last = 2 * u.shape[1] // eb - 1
    rspec = pl.BlockSpec((PEER_HEADS, PEER_NKEYS, tt), lambda i, s: (0, 0, i))
    rowspec = lambda off: pl.BlockSpec((PEER_HEADS, eb // PEER_NKEYS, tt), lambda i, s: (0, 2 * s + off, i))
    cnt, a0, r1, b1 = route
    return pl.pallas_call(
        _peer_ffn_kernel,
        grid=(n // tt, (last + 1) // 2),
        in_specs=[pl.BlockSpec((d, tt), lambda i, s: (0, i)),
                  pl.BlockSpec((None, eb // 2, d), lambda i, s: (layer, 0, 0)),
                  pl.BlockSpec((None, eb // 2, d), lambda i, s: (layer, 2 * s + 1, 0)),
                  pl.BlockSpec((None, eb // 2, d), lambda i, s: (layer, jnp.minimum(2 * s + 2, last), 0)),
                  pl.BlockSpec((None, None, d // 2, eb), lambda i, s: (layer, 2 * s, 0, 0)),
                  pl.BlockSpec((None, None, d // 2, eb), lambda i, s: (layer, 2 * s + 1, 0, 0)),
                  rowspec(0), rowspec(0), rowspec(1), rowspec(1),
                  rspec, rspec,
                  pl.BlockSpec((tt, d), lambda i, s: (i, 0)),
                  pl.BlockSpec((1, d), lambda i, s: (0, 0)),
                  pl.BlockSpec((1, d), lambda i, s: (0, 0))],
        out_specs=pl.BlockSpec((tt, d), lambda i, s: (i, 0)),
        out_shape=jax.ShapeDtypeStruct((n, d), F32),
        scratch_shapes=[pltpu.VMEM((d, tt), F32), pltpu.VMEM((eb, tt), F32), pltpu.VMEM((eb, tt), F32),
                        pltpu.VMEM((PEER_HEADS, PEER_NKEYS, tt), BF16),
                        pltpu.VMEM((PEER_HEADS, PEER_NKEYS, tt), BF16)],
        compiler_params=_params("parallel", "arbitrary"),
        name="peer_ffn",
    )(ht, u, u, u, v_blocks, v_blocks, cnt, a0, cnt, a0, r1, b1, h, g.reshape(1, d), b.reshape(1, d))


def kernel(x, na_w_in, na_rel_bias, na_w_out, mla_w_in, mla_q_norm, mla_kv_norm, mla_w_q_up, mla_w_kv_up, mla_w_out, hg_w_in, hg_lower_bound, hg_norm, hg_w_out, peer_w_q, peer_sub_keys, peer_u, peer_v, ln_mix_g, ln_mix_b, ln_ffn_g, ln_ffn_b):
    batch, seq, d = x.shape
    rows = seq // GRID_W
    lb_w = jax.nn.softmax(hg_lower_bound.astype(F32), axis=0)
    lb_all = jnp.cumsum(lb_w, axis=0) - lb_w[0:1]
    h = x.reshape(batch * seq, d)
    u_all = _pack_rows(peer_u)
    v_all = _pack_rows(peer_v.reshape(DEPTH, -1, PEER_EXPERT_BLOCK, d).transpose(0, 1, 3, 2))
    for layer in range(DEPTH):
        kind = layer % N_MIXERS
        j = layer // N_MIXERS
        if kind == 0:
            qk, vt = na_project(h, na_w_in[j])
            mix_in = neighborhood_attention(qk, vt, na_rel_bias[j], batch, rows)
            w_out = na_w_out[j]
        elif kind == 1:
            q, k, v = mla_projections(h, mla_w_in[j], mla_q_norm[j], mla_kv_norm[j],
                                      mla_w_q_up[j], mla_w_kv_up[j], seq)
            mix_in = mla_attention(q, k, v, batch, seq)
            w_out = mla_w_out[j]
        else:
            z = matmul(h, hg_w_in[j].astype(BF16), F32)
            mix_in = hgrn2_scan(z, lb_all[layer], hg_norm[j], batch, seq)
            w_out = hg_w_out[j]
        h, ht = matmul_res_ln(mix_in, w_out.astype(BF16), h, ln_mix_g[layer], ln_mix_b[layer])
        route = peer_route(ht, peer_w_q[layer].T.astype(BF16), peer_sub_keys[layer].astype(BF16))
        h = peer_ffn(h, ht, route, u_all, v_all, layer, ln_ffn_g[layer], ln_ffn_b[layer])
    return h.reshape(batch, seq, d)
```

```python
import functools
import math

import jax
import jax.numpy as jnp
import numpy as np
from jax import lax
from jax.experimental import pallas as pl
from jax.experimental.pallas import tpu as pltpu

F32 = jnp.float32
BF16 = jnp.bfloat16

D_MODEL = 1024
DEPTH = 4
GRID_W = 64
N_MIXERS = 3

NA_HEADS = 16
NA_HEAD_DIM = 64
NA_KH = 8
NA_KW = 16

MLA_HEADS = 16
MLA_NOPE = 64
MLA_ROPE = 32
MLA_V = 64
MLA_Q_RANK = 256
MLA_KV_RANK = 256
ROPE_THETA = 10000.0

HG_HEADS = 8
HG_F = 128
HG_CHUNK = 128

PEER_HEADS = 8
PEER_NKEYS = 128
PEER_TOPK = 16
PEER_EXPERT_BLOCK = 1024
PEER_CHUNK_ROWS = 2
NORM_EPS = 1e-5
DN_ALPHA = (2.0 * DEPTH) ** 0.25

LANES = 128
F32_SUBLANES = 8
BF16_SUBLANES = 16
VMEM_LIMIT = 48 * 1024 * 1024
MASK_NEG = -1e30
RSQRT2 = 0.7071067811865476


def _params(*sem):
    return pltpu.CompilerParams(dimension_semantics=sem, vmem_limit_bytes=VMEM_LIMIT)


def _layer_norm_rows(z, g, b):
    mu = jnp.mean(z, axis=-1, keepdims=True)
    zc = z - mu
    var = jnp.mean(zc * zc, axis=-1, keepdims=True)
    return zc * lax.rsqrt(var + NORM_EPS) * g + b


def _dot(a, b):
    return jnp.dot(a, b, preferred_element_type=F32)


def _dot_nt(a, b):
    return lax.dot_general(a, b, (((1,), (1,)), ((), ())), preferred_element_type=F32)


def _dot_tn(a, b):
    return lax.dot_general(a, b, (((0,), (0,)), ((), ())), preferred_element_type=F32)


def _mm_kernel(a_ref, w_ref, o_ref):
    o_ref[...] = _dot(a_ref[...].astype(BF16), w_ref[...]).astype(o_ref.dtype)


def matmul(a, w, out_dtype, tm=512, tn=1024):
    m, k = a.shape
    n = w.shape[1]
    tn = min(tn, n)
    return pl.pallas_call(
        _mm_kernel,
        grid=(m // tm, n // tn),
        in_specs=[pl.BlockSpec((tm, k), lambda i, j: (i, 0)),
                  pl.BlockSpec((k, tn), lambda i, j: (0, j))],
        out_specs=pl.BlockSpec((tm, tn), lambda i, j: (i, j)),
        out_shape=jax.ShapeDtypeStruct((m, n), out_dtype),
        compiler_params=_params("parallel", "parallel"),
        name="proj",
    )(a, w)


def _mm_res_ln_kernel(a_ref, w_ref, res_ref, g_ref, b_ref, o_ref, ot_ref):
    y = _dot(a_ref[...], w_ref[...])
    out = _layer_norm_rows(DN_ALPHA * res_ref[...] + y, g_ref[...], b_ref[...])
    o_ref[...] = out
    ot_ref[...] = out.T.astype(BF16)


def matmul_res_ln(a, w, res, g, b, tm=256):
    m, k = a.shape
    n = w.shape[1]
    return pl.pallas_call(
        _mm_res_ln_kernel,
        grid=(m // tm,),
        in_specs=[pl.BlockSpec((tm, k), lambda i: (i, 0)),
                  pl.BlockSpec((k, n), lambda i: (0, 0)),
                  pl.BlockSpec((tm, n), lambda i: (i, 0)),
                  pl.BlockSpec((1, n), lambda i: (0, 0)),
                  pl.BlockSpec((1, n), lambda i: (0, 0))],
        out_specs=[pl.BlockSpec((tm, n), lambda i: (i, 0)),
                   pl.BlockSpec((n, tm), lambda i: (0, i))],
        out_shape=[jax.ShapeDtypeStruct((m, n), F32),
                   jax.ShapeDtypeStruct((n, m), BF16)],
        compiler_params=_params("parallel"),
        name="out_proj_ln",
    )(a, w, res, g.reshape(1, n), b.reshape(1, n))


def _na_row_start(r, rows):
    return jnp.clip(r - NA_KH // 2, 0, rows - NA_KH)


def _na_proj_kernel(h_ref, wqk_ref, wvt_ref, qk_ref, vt_ref):
    hb = h_ref[...].astype(BF16)
    qk_ref[...] = _dot(hb, wqk_ref[...]).astype(qk_ref.dtype)
    vt_ref[...] = _dot_nt(wvt_ref[...], hb).astype(vt_ref.dtype)


def na_project(h, w_in, tm=512):
    n, d = h.shape
    wqk = w_in[:, :2 * d].astype(BF16)
    wvt = w_in[:, 2 * d:].T.astype(BF16)
    return pl.pallas_call(
        _na_proj_kernel,
        grid=(n // tm,),
        in_specs=[pl.BlockSpec((tm, d), lambda i: (i, 0)),
                  pl.BlockSpec((d, 2 * d), lambda i: (0, 0)),
                  pl.BlockSpec((d, d), lambda i: (0, 0))],
        out_specs=[pl.BlockSpec((tm, 2 * d), lambda i: (i, 0)),
                   pl.BlockSpec((d, tm), lambda i: (0, i))],
        out_shape=[jax.ShapeDtypeStruct((n, 2 * d), BF16), jax.ShapeDtypeStruct((d, n), BF16)],
        compiler_params=_params("parallel"),
        name="na_proj",
    )(h, wqk, wvt)


NA_QROWS = 2
NA_KROWS = NA_KH + NA_QROWS
NA_EDGE = NA_KH // 4


def _na_key_start(i, rows):
    return jnp.clip(i - NA_KH // 4, 0, (rows - NA_KROWS) // 2)


def _na_kernel(q_ref, k_ref, vt_ref, bias_ref, o_ref):
    n_keys = NA_KROWS * GRID_W
    n_q = NA_QROWS * GRID_W
    lane = lax.broadcasted_iota(jnp.int32, (n_q, LANES), 1)
    low = lane < NA_HEAD_DIM
    row_low = lax.broadcasted_iota(jnp.int32, (LANES, n_q), 0) < NA_HEAD_DIM
    for p in range(NA_HEADS // 2):
        cols = slice(p * LANES, (p + 1) * LANES)
        qp = q_ref[0, :, :, cols].reshape(n_q, LANES)
        zq = jnp.zeros_like(qp)
        kp = k_ref[0, :, :, cols].reshape(n_keys, LANES)
        vtp = vt_ref[cols, :]
        outs = []
        for half in range(2):
            qh = jnp.where(low, qp, zq) if half == 0 else jnp.where(low, zq, qp)
            s = _dot_nt(kp, qh) * (NA_HEAD_DIM ** -0.5) + bias_ref[0, 2 * p + half]
            m = jnp.max(s, axis=0, keepdims=True)
            e = jnp.exp(s - m)
            l = jnp.sum(e, axis=0, keepdims=True)
            outs.append(_dot(vtp, e.astype(BF16)) / l)
        o2 = jnp.where(row_low, outs[0], outs[1]).T.astype(o_ref.dtype)
        for a in range(NA_QROWS):
            o_ref[a, :, cols] = o2[a * GRID_W:(a + 1) * GRID_W]


def _na_bias_table(rel_bias, rows):
    cols = np.arange(GRID_W)
    c0 = np.clip(cols - NA_KW // 2, 0, GRID_W - NA_KW)
    kc = np.arange(GRID_W)
    inside = (kc[None, :] >= c0[:, None]) & (kc[None, :] < c0[:, None] + NA_KW)
    dc = np.clip(kc[None, :] - cols[:, None] + (NA_KW - 1), 0, 2 * NA_KW - 2)
    b2 = jnp.where(inside[None, None], rel_bias[:, :, dc].astype(F32), MASK_NEG)
    steps = rows // NA_QROWS
    variant_steps = list(range(NA_EDGE)) + [NA_EDGE] + list(range(steps - NA_EDGE, steps))
    dr = np.zeros((len(variant_steps), NA_QROWS, NA_KROWS), np.int32)
    ok = np.zeros(dr.shape, bool)
    for v, i in enumerate(variant_steps):
        key0 = 2 * int(np.clip(i - NA_KH // 4, 0, (rows - NA_KROWS) // 2))
        for a in range(NA_QROWS):
            r = NA_QROWS * i + a
            r0 = int(np.clip(r - NA_KH // 2, 0, rows - NA_KH))
            key_rows = key0 + np.arange(NA_KROWS)
            ok[v, a] = (key_rows >= r0) & (key_rows < r0 + NA_KH)
            dr[v, a] = np.clip(key_rows - r + NA_KH - 1, 0, 2 * NA_KH - 2)
    t = b2[:, dr]
    t = jnp.where(ok[None, :, :, :, None, None], t, MASK_NEG)
    t = t.transpose(1, 0, 3, 5, 2, 4)
    return t.reshape(len(variant_steps), NA_HEADS, NA_KROWS * GRID_W, NA_QROWS * GRID_W)


def neighborhood_attention(qk, vt, rel_bias, batch, rows):
    d = D_MODEL
    steps = rows // NA_QROWS
    n_keys = NA_KROWS * GRID_W
    qk4 = qk.reshape(batch, rows, GRID_W, 2 * d)
    bias = _na_bias_table(rel_bias, rows)
    el = pl.Element
    key0 = lambda i: _na_key_start(i, rows)

    def variant(i):
        return jnp.where(i < NA_EDGE, i, jnp.where(i >= steps - NA_EDGE, i - (steps - 2 * NA_EDGE - 1), NA_EDGE))

    out = pl.pallas_call(
        _na_kernel,
        grid=(batch, steps),
        in_specs=[pl.BlockSpec((el(1), el(NA_QROWS), el(GRID_W), el(d)),
                               lambda b, i: (b, NA_QROWS * i, 0, 0)),
                  pl.BlockSpec((el(1), el(NA_KROWS), el(GRID_W), el(d)),
                               lambda b, i: (b, 2 * key0(i), 0, d)),
                  pl.BlockSpec((el(d), el(n_keys)),
                               lambda b, i: (0, (b * (rows // 2) + key0(i)) * (2 * GRID_W))),
                  pl.BlockSpec((1, NA_HEADS, n_keys, NA_QROWS * GRID_W),
                               lambda b, i: (variant(i), 0, 0, 0))],
        out_specs=pl.BlockSpec((None, NA_QROWS, GRID_W, d), lambda b, i: (b, i, 0, 0)),
        out_shape=jax.ShapeDtypeStruct((batch, rows, GRID_W, d), BF16),
        compiler_params=_params("parallel", "arbitrary"),
        name="na_attn",
    )(qk4, qk4, vt, bias)
    return out.reshape(batch * rows * GRID_W, d)


def _rms_rows(x, g):
    return x * lax.rsqrt(jnp.mean(x * x, axis=-1, keepdims=True) + NORM_EPS) * g


def _mla_proj_kernel(x_ref, win_ref, qn_ref, kvn_ref, wqa_ref, wqb_ref, wkv_ref, cos_ref, sin_ref,
                     q_ref, k_ref, v_ref):
    hd = MLA_HEADS * LANES
    hin = _dot(x_ref[...].astype(BF16), win_ref[...])
    cq = _rms_rows(hin[:, :MLA_Q_RANK], qn_ref[...]).astype(BF16)
    ckv = _rms_rows(hin[:, MLA_Q_RANK:MLA_Q_RANK + MLA_KV_RANK], kvn_ref[...]).astype(BF16)
    cos = cos_ref[...]
    sin = sin_ref[...]
    cos_t = jnp.tile(cos, (1, MLA_HEADS))
    sin_t = jnp.tile(sin, (1, MLA_HEADS))
    q = _dot(cq, wqa_ref[...]) * cos_t + _dot(cq, wqb_ref[...]) * sin_t
    q_ref[...] = (q * ((MLA_NOPE + MLA_ROPE) ** -0.5)).astype(q_ref.dtype)
    kv = _dot(ckv, wkv_ref[...])
    base = MLA_Q_RANK + MLA_KV_RANK
    kpe = hin[:, base:base + LANES] * cos + hin[:, base + LANES:base + 2 * LANES] * sin
    k_ref[...] = (kv[:, :hd] + jnp.tile(kpe, (1, MLA_HEADS))).astype(k_ref.dtype)
    v_ref[...] = kv[:, hd:].astype(v_ref.dtype)


def _mla_weights(w_in, w_q_up, w_kv_up):
    r = MLA_ROPE
    half = r // 2
    dq = MLA_NOPE + r
    nh = MLA_HEADS
    base = MLA_Q_RANK + MLA_KV_RANK
    kpe = w_in[:, base:base + r]
    zpad = lambda w, lo, hi: jnp.pad(w, ((0, 0), (lo, hi)))
    swap = lambda w: jnp.concatenate([-w[..., half:], w[..., :half]], axis=-1)
    kpe_a = zpad(kpe, MLA_NOPE, LANES - MLA_NOPE - r)
    kpe_b = zpad(swap(kpe), MLA_NOPE, LANES - MLA_NOPE - r)
    win = jnp.concatenate([w_in[:, :base], kpe_a, kpe_b], axis=1)
    wq = w_q_up.reshape(MLA_Q_RANK, nh, dq)
    pad3 = lambda w, lo, hi: jnp.pad(w, ((0, 0), (0, 0), (lo, hi)))
    wqa = pad3(wq, 0, LANES - dq).reshape(MLA_Q_RANK, nh * LANES)
    wqb = pad3(swap(wq[:, :, MLA_NOPE:]), MLA_NOPE, LANES - dq).reshape(MLA_Q_RANK, nh * LANES)
    wkv = w_kv_up.reshape(MLA_KV_RANK, nh, MLA_NOPE + MLA_V)
    wk = pad3(wkv[:, :, :MLA_NOPE], 0, LANES - MLA_NOPE).reshape(MLA_KV_RANK, nh * LANES)
    wv = wkv[:, :, MLA_NOPE:].reshape(MLA_KV_RANK, nh * MLA_V)
    return (win.astype(BF16), wqa.astype(BF16), wqb.astype(BF16),
            jnp.concatenate([wk, wv], axis=1).astype(BF16))


def _rope_tables(seq):
    half = MLA_ROPE // 2
    inv_freq = ROPE_THETA ** (-jnp.arange(half, dtype=F32) * 2.0 / MLA_ROPE)
    ang = jnp.arange(seq, dtype=F32)[:, None] * inv_freq[None, :]
    cos = jnp.cos(ang)
    sin = jnp.sin(ang)
    tail = LANES - MLA_NOPE - MLA_ROPE
    cos_p = jnp.concatenate([jnp.ones((seq, MLA_NOPE), F32), cos, cos, jnp.zeros((seq, tail), F32)], axis=1)
    sin_p = jnp.concatenate([jnp.zeros((seq, MLA_NOPE), F32), sin, sin, jnp.zeros((seq, tail), F32)], axis=1)
    return cos_p, sin_p


def mla_projections(h, w_in, q_norm, kv_norm, w_q_up, w_kv_up, seq, tm=256):
    n, d = h.shape
    win, wqa, wqb, wkv = _mla_weights(w_in, w_q_up, w_kv_up)
    cos_p, sin_p = _rope_tables(seq)
    hd = MLA_HEADS * LANES
    per_seq = seq // tm
    full = lambda a: pl.BlockSpec(a.shape, lambda i: (0,) * a.ndim)
    qn = q_norm.reshape(1, -1).astype(F32)
    kvn = kv_norm.reshape(1, -1).astype(F32)
    return pl.pallas_call(
        _mla_proj_kernel,
        grid=(n // tm,),
        in_specs=[pl.BlockSpec((tm, d), lambda i: (i, 0)), full(win), full(qn), full(kvn),
                  full(wqa), full(wqb), full(wkv),
                  pl.BlockSpec((tm, LANES), lambda i: (i % per_seq, 0)),
                  pl.BlockSpec((tm, LANES), lambda i: (i % per_seq, 0))],
        out_specs=[pl.BlockSpec((tm, hd), lambda i: (i, 0)),
                   pl.BlockSpec((tm, hd), lambda i: (i, 0)),
                   pl.BlockSpec((tm, MLA_HEADS * MLA_V), lambda i: (i, 0))],
        out_shape=[jax.ShapeDtypeStruct((n, hd), BF16),
                   jax.ShapeDtypeStruct((n, hd), BF16),
                   jax.ShapeDtypeStruct((n, MLA_HEADS * MLA_V), BF16)],
        compiler_params=_params("parallel"),
        name="mla_proj",
    )(h, win, qn, kvn, wqa, wqb, wkv, cos_p, sin_p)


def _mla_attn_kernel(q_ref, k_ref, v_ref, o_ref):
    v = v_ref[...]
    lane = lax.broadcasted_iota(jnp.int32, o_ref.shape, 1)
    halves = []
    for half in range(2):
        cols = slice(half * LANES, (half + 1) * LANES)
        s = _dot_nt(q_ref[:, cols], k_ref[:, cols])
        m = jnp.max(s, axis=-1, keepdims=True)
        e = jnp.exp(s - m)
        l = jnp.sum(e, axis=-1, keepdims=True)
        halves.append(_dot(e.astype(BF16), v) / l)
    o_ref[...] = jnp.where(lane < MLA_V, halves[0], halves[1]).astype(o_ref.dtype)


def mla_attention(q, k, v, batch, seq, tq=256):
    hd = MLA_HEADS * LANES
    q3 = q.reshape(batch, seq, hd)
    k3 = k.reshape(batch, seq, hd)
    v3 = v.reshape(batch, seq, MLA_HEADS * MLA_V)
    out = pl.pallas_call(
        _mla_attn_kernel,
        grid=(batch, MLA_HEADS // 2, seq // tq),
        in_specs=[pl.BlockSpec((None, tq, 2 * LANES), lambda b, p, i: (b, i, p)),
                  pl.BlockSpec((None, seq, 2 * LANES), lambda b, p, i: (b, 0, p)),
                  pl.BlockSpec((None, seq, LANES), lambda b, p, i: (b, 0, p))],
        out_specs=pl.BlockSpec((None, tq, LANES), lambda b, p, i: (b, i, p)),
        out_shape=jax.ShapeDtypeStruct((batch, seq, MLA_HEADS * MLA_V), BF16),
        compiler_params=_params("parallel", "parallel", "arbitrary"),
        name="mla_attn",
    )(q3, k3, v3)
    return out.reshape(batch * seq, MLA_HEADS * MLA_V)


HG_LEVELS = int(math.log2(HG_CHUNK))
HG_UNROLL = 1


def _hg_constants(reverse):
    c = HG_CHUNK
    t = np.arange(c)[:, None]
    u = np.arange(c)[None, :]
    if not reverse:
        incl = u <= t
        rest = u > t
    else:
        incl = u >= t
        rest = u < t
    mats = [incl, rest]
    masks = []
    roles = []
    for lvl in range(1, HG_LEVELS + 1):
        size = 1 << lvl
        start = (t // size) * size
        mid = start + size // 2
        upper = t >= mid
        if not reverse:
            q_side = (u >= mid) & (u <= t)
            k_side = (u > t) & (u <= mid - 1)
            is_query = upper
        else:
            q_side = (u >= t) & (u < mid)
            k_side = (u >= mid) & (u < t)
            is_query = ~upper
        mats.append(np.where(is_query, q_side, k_side))
        same = (t // size) == (u // size)
        key_row = (~is_query).T
        masks.append(same & is_query & np.broadcast_to(key_row, (c, c)))
        roles.append(np.broadcast_to(is_query, (c, LANES)))
    w = np.concatenate(mats, axis=0).astype(np.float32)
    return (jnp.asarray(w, BF16), jnp.asarray(np.stack(masks).astype(np.float32)),
            jnp.asarray(np.stack(roles).astype(np.float32)))


def _hg_chunk(q, zf, v_b, lb, w_ref, m_ref, r_ref, st, total_row):
    c = HG_CHUNK
    sg = jax.nn.sigmoid(zf)
    k = (1.0 - lb) * jax.nn.sigmoid(-zf)
    lf = jnp.log(lb + (1.0 - lb) * sg)
    hi = lf.astype(BF16)
    lo = (lf - hi.astype(F32)).astype(BF16)
    f = lf.shape[1]
    ex2 = _dot(w_ref[...], jnp.concatenate([hi, lo], axis=1))
    ex = ex2[:, :f] + ex2[:, f:]
    b_incl = ex[0:c]
    total = ex[total_row:total_row + 1]
    qd = q * jnp.exp(b_incl)
    kd = k * jnp.exp(ex[c:2 * c])
    row = lax.broadcasted_iota(jnp.int32, (c, c), 0)
    col = lax.broadcasted_iota(jnp.int32, (c, c), 1)
    a = jnp.where(row == col, jnp.sum(q * k, axis=-1, keepdims=True), 0.0)
    xs = []
    for lvl in range(HG_LEVELS):
        x = jnp.where(r_ref[lvl] > 0.5, q, k) * jnp.exp(ex[(2 + lvl) * c:(3 + lvl) * c])
        xs.append(x.astype(BF16))
    for lvl in range(0, HG_LEVELS, 2):
        pair = xs[lvl:lvl + 2]
        xb = jnp.concatenate(pair, axis=0)
        g = _dot_nt(xb, xb)
        for j in range(len(pair)):
            a = a + m_ref[lvl + j] * g[j * c:(j + 1) * c, j * c:(j + 1) * c]
    o = _dot(a.astype(BF16), v_b) + _dot_nt(qd.astype(BF16), st.astype(BF16))
    return o, st * jnp.exp(total) + _dot_tn(v_b, kd.astype(BF16))


def _hg_kernel(zq_ref, zff_ref, zfb_ref, zi_ref, zg_ref, lb_ref, g_ref,
               wf_ref, mf_ref, rf_ref, wb_ref, mb_ref, rb_ref,
               o_ref, accf_ref, accb_ref):
    c = HG_CHUNK
    seq = zq_ref.shape[0]
    n = seq // c
    lb_f = lb_ref[0:1, :]
    lb_b = lb_ref[1:2, :]

    def gated(ref, rows):
        z = ref[rows, :]
        return z * jax.nn.sigmoid(z)

    def body(i, carry):
        st_f, st_b = carry
        rf = pl.ds(pl.multiple_of(i * c, c), c)
        rb = pl.ds(pl.multiple_of((n - 1 - i) * c, c), c)
        of, st_f = _hg_chunk(gated(zq_ref, rf), zff_ref[rf, :], zi_ref[rf, :].astype(BF16), lb_f,
                             wf_ref, mf_ref, rf_ref, st_f, c - 1)
        accf_ref[rf, :] = of
        ob, st_b = _hg_chunk(gated(zq_ref, rb), zfb_ref[rb, :], zi_ref[rb, :].astype(BF16), lb_b,
                             wb_ref, mb_ref, rb_ref, st_b, 0)
        accb_ref[rb, :] = ob
        return st_f, st_b

    zero = jnp.zeros((HG_F, HG_F), F32)
    lax.fori_loop(0, n, body, (zero, zero), unroll=HG_UNROLL)

    def finish(i, carry):
        rows = pl.ds(pl.multiple_of(i * c, c), c)
        o = accf_ref[rows, :] + accb_ref[rows, :]
        o = o * lax.rsqrt(jnp.mean(o * o, axis=-1, keepdims=True) + NORM_EPS)
        o_ref[rows, :] = (o * g_ref[...] * gated(zg_ref, rows)).astype(o_ref.dtype)
        return carry

    lax.fori_loop(0, n, finish, 0)


def hgrn2_scan(z, lb, norm_g, batch, seq):
    d = D_MODEL
    z3 = z.reshape(batch, seq, 5 * d)
    wf, mf, rf = _hg_constants(False)
    wb, mb, rb = _hg_constants(True)
    zspec = lambda j: pl.BlockSpec((None, seq, HG_F), lambda b, h: (b, 0, j * HG_HEADS + h))
    full = lambda a: pl.BlockSpec(a.shape, lambda b, h: (0,) * a.ndim)
    out = pl.pallas_call(
        _hg_kernel,
        grid=(batch, HG_HEADS),
        in_specs=[zspec(0), zspec(1), zspec(2), zspec(3), zspec(4),
                  pl.BlockSpec((2, HG_F), lambda b, h: (0, h)),
                  pl.BlockSpec((1, HG_F), lambda b, h: (0, h)),
                  full(wf), full(mf), full(rf), full(wb), full(mb), full(rb)],
        out_specs=pl.BlockSpec((None, seq, HG_F), lambda b, h: (b, 0, h)),
        out_shape=jax.ShapeDtypeStruct((batch, seq, d), BF16),
        scratch_shapes=[pltpu.VMEM((seq, HG_F), F32), pltpu.VMEM((seq, HG_F), F32)],
        compiler_params=_params("parallel", "parallel"),
        name="hgrn2_scan",
    )(z3, z3, z3, z3, z3, lb.astype(F32), norm_g.reshape(1, d).astype(F32),
      wf, mf, rf, wb, mb, rb)
    return out.reshape(batch * seq, d)


def _merge_sort_network(n):
    pairs = []
    p = 1
    while p < n:
        k = p
        while k >= 1:
            for j in range(k % p, n - k, 2 * k):
                for i in range(min(k, n - j - k)):
                    if (i + j) // (2 * p) == (i + j + k) // (2 * p):
                        pairs.append((i + j, i + j + k))
            k //= 2
        p *= 2
    return pairs


def _top_values(s, count):
    rows = s.shape[0]
    n = rows // F32_SUBLANES
    size = 1 << (n - 1).bit_length()
    neg = jnp.full((F32_SUBLANES, s.shape[1]), -jnp.inf, F32)
    lists = [s[i * F32_SUBLANES:(i + 1) * F32_SUBLANES] for i in range(n)] + [neg] * (size - n)
    for i, j in _merge_sort_network(size):
        hi = jnp.maximum(lists[i], lists[j])
        lists[j] = jnp.minimum(lists[i], lists[j])
        lists[i] = hi
    lists = lists[:min(size, count)]
    vals = []
    for it in range(count):
        head = lists[0]
        m = jnp.max(head, axis=0, keepdims=True)
        vals.append(m)
        hit = head >= m
        depth = min(len(lists), count - it - 1)
        for k in range(depth):
            nxt = lists[k + 1] if k + 1 < len(lists) else neg
            lists[k] = jnp.where(hit, nxt, lists[k])
        lists = lists[:max(depth, 1)]
    return jnp.concatenate(vals, axis=0)


def _peer_route_kernel(ht_ref, wq_ref, keys_ref, cnt_ref, a0_ref, r1_ref, b1_ref):
    kk = PEER_TOPK
    ht = ht_ref[...]
    for h in range(PEER_HEADS):
        scores = []
        for c in range(2):
            g = 2 * h + c
            qt = _dot(wq_ref[g * LANES:(g + 1) * LANES, :], ht)
            scores.append(_dot(keys_ref[c], qt.astype(BF16)))
        s0, s1 = scores
        sv0 = _top_values(s0, kk)
        sv1 = _top_values(s1, kk)
        r1 = jnp.full(s1.shape, float(kk), F32)
        for b in range(kk):
            r1 = jnp.where(s1 == sv1[b:b + 1], float(b), r1)
        cands = [sv0[a:a + 1] + sv1[0:kk // (a + 1)] for a in range(kk)]
        n_cand = sum(kk // (a + 1) for a in range(kk))
        pad = (-n_cand) % 8
        if pad:
            cands.append(jnp.full((pad, ht.shape[1]), -jnp.inf, F32))
        tau = _top_values(jnp.concatenate(cands, axis=0), kk)[kk - 1:kk]
        e0 = jnp.exp(sv0 - sv0[0:1])
        e1 = jnp.exp(sv1 - sv1[0:1])
        z = jnp.zeros_like(tau)
        for a in range(kk):
            nb = kk // (a + 1)
            sel = (sv0[a:a + 1] + sv1[0:nb]) >= tau
            z = z + e0[a:a + 1] * jnp.sum(jnp.where(sel, e1[0:nb], 0.0), axis=0, keepdims=True)
        cnt = jnp.zeros(s0.shape, F32)
        for a in range(kk):
            pair_ok = (sv0[a:a + 1] + sv1) >= tau
            n_a = jnp.sum(jnp.where(pair_ok, 1.0, 0.0), axis=0, keepdims=True)
            cnt = jnp.where(s0 == sv0[a:a + 1], n_a, cnt)
        cnt_ref[h] = cnt
        a0_ref[h] = jnp.exp(s0 - sv0[0:1])
        r1_ref[h] = r1.astype(BF16)
        b1_ref[h] = (jnp.exp(s1 - sv1[0:1]) * (0.5 / z)).astype(BF16)


def peer_route(ht, wq_t, keys, tt=256):
    d, n = ht.shape
    shape = (PEER_HEADS, PEER_NKEYS, n)
    ospec = pl.BlockSpec((PEER_HEADS, PEER_NKEYS, tt), lambda i: (0, 0, i))
    return pl.pallas_call(
        _peer_route_kernel,
        grid=(n // tt,),
        in_specs=[pl.BlockSpec((d, tt), lambda i: (0, i)),
                  pl.BlockSpec(wq_t.shape, lambda i: (0, 0)),
                  pl.BlockSpec(keys.shape, lambda i: (0, 0, 0))],
        out_specs=[ospec, ospec, ospec, ospec],
        out_shape=[jax.ShapeDtypeStruct(shape, F32), jax.ShapeDtypeStruct(shape, F32),
                   jax.ShapeDtypeStruct(shape, BF16), jax.ShapeDtypeStruct(shape, BF16)],
        compiler_params=_params("parallel"),
        name="peer_route",
    )(ht, wq_t, keys)


def _peer_gate_chunk(hid, i0, cnt_ref, a0_ref, r1_ref, b1_ref):
    tt = hid.shape[1]
    zero = jnp.zeros((PEER_NKEYS, LANES), BF16)

    def row_tile(ref, h, ii, cols):
        row = jnp.broadcast_to(ref[h, ii:ii + 1, cols], (BF16_SUBLANES, LANES)).astype(BF16)
        return pltpu.repeat(row, PEER_NKEYS // BF16_SUBLANES, axis=0)

    out_rows = []
    for k in range(PEER_CHUNK_ROWS):
        tiles = []
        for tc in range(tt // LANES):
            cols = slice(tc * LANES, (tc + 1) * LANES)
            gate = zero
            for h in range(PEER_HEADS):
                cnt = row_tile(cnt_ref, h, i0 + k, cols)
                a0 = row_tile(a0_ref, h, i0 + k, cols)
                gate = gate + jnp.where(r1_ref[h, :, cols] < cnt, a0 * b1_ref[h, :, cols], zero)
            x = hid[k * PEER_NKEYS:(k + 1) * PEER_NKEYS, cols]
            act = x.astype(BF16) * (1.0 + lax.erf(x * RSQRT2)).astype(BF16)
            tiles.append(gate * act)
        out_rows.append(jnp.concatenate(tiles, axis=1))
    return jnp.concatenate(out_rows, axis=0)


def _pack_rows(x):
    bits = lax.bitcast_convert_type(x.astype(BF16), jnp.uint16).astype(jnp.uint32)
    pairs = bits.reshape(*bits.shape[:-2], bits.shape[-2] // 2, 2, bits.shape[-1])
    return pairs[..., 0, :] | (pairs[..., 1, :] << 16)


def _pack_rows_of_transpose(x):
    x = x.astype(BF16)
    words = lax.bitcast_convert_type(x.reshape(*x.shape[:-1], x.shape[-1] // 2, 2), jnp.uint32)
    return jnp.swapaxes(words, -1, -2)


def _unpack_rows(words):
    return pltpu.bitcast(words, BF16)


def _ordered_after(x, dep):
    z = pltpu.bitcast(dep[:BF16_SUBLANES, :LANES], jnp.uint32)
    z = (z >> 16) >> 16
    zero = pltpu.bitcast(z, BF16)
    zero = pltpu.repeat(pltpu.repeat(zero, x.shape[0] // BF16_SUBLANES, axis=0), x.shape[1] // LANES, axis=1)
    return x + zero


def _peer_ffn_kernel(ht_ref, u0_ref, ub_ref, un_ref, vta_ref, vtb_ref, cnta_ref, a0a_ref, cntb_ref, a0b_ref,
                     r1_in_ref, b1_in_ref, res_ref, g_ref, b_ref,
                     o_ref, acc_ref, hida_ref, hidb_ref, r1_ref, b1_ref):
    s = pl.program_id(1)
    half = ht_ref.shape[1] // 2

    @pl.when(s == 0)
    def _():
        acc_ref[...] = jnp.zeros_like(acc_ref)
        r1_ref[...] = r1_in_ref[...]
        b1_ref[...] = b1_in_ref[...]
        hida_ref[...] = _dot(_unpack_rows(u0_ref[...]), ht_ref[...])

    chunk = PEER_CHUNK_ROWS * PEER_NKEYS
    n_chunks = PEER_EXPERT_BLOCK // chunk

    def run_block(acc, hid_ref, vt_ref, cnt_ref, a0_ref, next_u_ref, next_hid_ref):
        for c in range(n_chunks):
            span = slice(c * chunk, (c + 1) * chunk)
            w = _peer_gate_chunk(hid_ref[span, :], c * PEER_CHUNK_ROWS, cnt_ref, a0_ref, r1_ref, b1_ref)
            acc = acc + _dot(_unpack_rows(vt_ref[:, span]), w)
            if c % (n_chunks // 2) == n_chunks // 2 - 1:
                hh = c // (n_chunks // 2)
                tcols = slice(hh * half, (hh + 1) * half)
                next_hid_ref[:, tcols] = _dot(_unpack_rows(next_u_ref[...]),
                                              _ordered_after(ht_ref[:, tcols], w))
        return acc

    acc = run_block(acc_ref[...], hida_ref, vta_ref, cnta_ref, a0a_ref, ub_ref, hidb_ref)
    acc_ref[...] = run_block(acc, hidb_ref, vtb_ref, cntb_ref, a0b_ref, un_ref, hida_ref)

    @pl.when(s == pl.num_programs(1) - 1)
    def _():
        y = acc_ref[...].T
        o_ref[...] = _layer_norm_rows(DN_ALPHA * res_ref[...] + y, g_ref[...], b_ref[...])


def peer_ffn(h, ht, route, u, v_blocks, layer, g, b, tt=512):
    n, d = h.shape
    eb = PEER_EXPERT_BLOCK
    last = 2 * u.shape[1] // eb - 1
    rspec = pl.BlockSpec((PEER_HEADS, PEER_NKEYS, tt), lambda i, s: (0, 0, i))
    rowspec = lambda off: pl.BlockSpec((PEER_HEADS, eb // PEER_NKEYS, tt), lambda i, s: (0, 2 * s + off, i))
    cnt, a0, r1, b1 = route
    return pl.pallas_call(
        _peer_ffn_kernel,
        grid=(n // tt, (last + 1) // 2),
        in_specs=[pl.BlockSpec((d, tt), lambda i, s: (0, i)),
                  pl.BlockSpec((None, eb // 2, d), lambda i, s: (layer, 0, 0)),
                  pl.BlockSpec((None, eb // 2, d), lambda i, s: (layer, 2 * s + 1, 0)),
                  pl.BlockSpec((None, eb // 2, d), lambda i, s: (layer, jnp.minimum(2 * s + 2, last), 0)),
                  pl.BlockSpec((None, None, d // 2, eb), lambda i, s: (layer, 2 * s, 0, 0)),
                  pl.BlockSpec((None, None, d // 2, eb), lambda i, s: (layer, 2 * s + 1, 0, 0)),
                  rowspec(0), rowspec(0), rowspec(1), rowspec(1),
                  rspec, rspec,
                  pl.BlockSpec((tt, d), lambda i, s: (i, 0)),
                  pl.BlockSpec((1, d), lambda i, s: (0, 0)),
                  pl.BlockSpec((1, d), lambda i, s: (0, 0))],
        out_specs=pl.BlockSpec((tt, d), lambda i, s: (i, 0)),
        out_shape=jax.ShapeDtypeStruct((n, d), F32),
        scratch_shapes=[pltpu.VMEM((d, tt), F32), pltpu.VMEM((eb, tt), F32), pltpu.VMEM((eb, tt), F32),
                        pltpu.VMEM((PEER_HEADS, PEER_NKEYS, tt), BF16),
                        pltpu.VMEM((PEER_HEADS, PEER_NKEYS, tt), BF16)],
        compiler_params=_params("parallel", "arbitrary"),
        name="peer_ffn",
    )(ht, u, u, u, v_blocks, v_blocks, cnt, a0, cnt, a0, r1, b1, h, g.reshape(1, d), b.reshape(1, d))


def kernel(x, na_w_in, na_rel_bias, na_w_out, mla_w_in, mla_q_norm, mla_kv_norm, mla_w_q_up, mla_w_kv_up, mla_w_out, hg_w_in, hg_lower_bound, hg_norm, hg_w_out, peer_w_q, peer_sub_keys, peer_u, peer_v, ln_mix_g, ln_mix_b, ln_ffn_g, ln_ffn_b):
    batch, seq, d = x.shape
    rows = seq // GRID_W
    lb_w = jax.nn.softmax(hg_lower_bound.astype(F32), axis=0)
    lb_all = jnp.cumsum(lb_w, axis=0) - lb_w[0:1]
    h = x.reshape(batch * seq, d)
    u_all = _pack_rows(peer_u)
    v_all = _pack_rows_of_transpose(peer_v.reshape(DEPTH, -1, PEER_EXPERT_BLOCK, d))
    for layer in range(DEPTH):
        kind = layer % N_MIXERS
        j = layer // N_MIXERS
        if kind == 0:
            qk, vt = na_project(h, na_w_in[j])
            mix_in = neighborhood_attention(qk, vt, na_rel_bias[j], batch, rows)
            w_out = na_w_out[j]
        elif kind == 1:
            q, k, v = mla_projections(h, mla_w_in[j], mla_q_norm[j], mla_kv_norm[j],
                                      mla_w_q_up[j], mla_w_kv_up[j], seq)
            mix_in = mla_attention(q, k, v, batch, seq)
            w_out = mla_w_out[j]
        else:
            z = matmul(h, hg_w_in[j].astype(BF16), F32)
            mix_in = hgrn2_scan(z, lb_all[layer], hg_norm[j], batch, seq)
            w_out = hg_w_out[j]
        h, ht = matmul_res_ln(mix_in, w_out.astype(BF16), h, ln_mix_g[layer], ln_mix_b[layer])
        route = peer_route(ht, peer_w_q[layer].T.astype(BF16), peer_sub_keys[layer].astype(BF16))
        h = peer_ffn(h, ht, route, u_all, v_all, layer, ln_ffn_g[layer], ln_ffn_b[layer])
    return h.reshape(batch, seq, d)
```

```python
import functools
import math

import jax
import jax.numpy as jnp
import numpy as np
from jax import lax
from jax.experimental import pallas as pl
from jax.experimental.pallas import tpu as pltpu

F32 = jnp.float32
BF16 = jnp.bfloat16

D_MODEL = 1024
DEPTH = 4
GRID_W = 64
N_MIXERS = 3

NA_HEADS = 16
NA_HEAD_DIM = 64
NA_KH = 8
NA_KW = 16

MLA_HEADS = 16
MLA_NOPE = 64
MLA_ROPE = 32
MLA_V = 64
MLA_Q_RANK = 256
MLA_KV_RANK = 256
ROPE_THETA = 10000.0

HG_HEADS = 8
HG_F = 128
HG_CHUNK = 128

PEER_HEADS = 8
PEER_NKEYS = 128
PEER_TOPK = 16
PEER_EXPERT_BLOCK = 1024
PEER_CHUNK_ROWS = 2
NORM_EPS = 1e-5
DN_ALPHA = (2.0 * DEPTH) ** 0.25

LANES = 128
F32_SUBLANES = 8
BF16_SUBLANES = 16
VMEM_LIMIT = 48 * 1024 * 1024
MASK_NEG = -1e30
RSQRT2 = 0.7071067811865476


def _params(*sem):
    return pltpu.CompilerParams(dimension_semantics=sem, vmem_limit_bytes=VMEM_LIMIT)


def _layer_norm_rows(z, g, b):
    mu = jnp.mean(z, axis=-1, keepdims=True)
    zc = z - mu
    var = jnp.mean(zc * zc, axis=-1, keepdims=True)
    return zc * lax.rsqrt(var + NORM_EPS) * g + b


def _dot(a, b):
    return jnp.dot(a, b, preferred_element_type=F32)


def _dot_nt(a, b):
    return lax.dot_general(a, b, (((1,), (1,)), ((), ())), preferred_element_type=F32)


def _dot_tn(a, b):
    return lax.dot_general(a, b, (((0,), (0,)), ((), ())), preferred_element_type=F32)


def _mm_kernel(a_ref, w_ref, o_ref):
    o_ref[...] = _dot(a_ref[...].astype(BF16), w_ref[...]).astype(o_ref.dtype)


def matmul(a, w, out_dtype, tm=512, tn=1024):
    m, k = a.shape
    n = w.shape[1]
    tn = min(tn, n)
    return pl.pallas_call(
        _mm_kernel,
        grid=(m // tm, n // tn),
        in_specs=[pl.BlockSpec((tm, k), lambda i, j: (i, 0)),
                  pl.BlockSpec((k, tn), lambda i, j: (0, j))],
        out_specs=pl.BlockSpec((tm, tn), lambda i, j: (i, j)),
        out_shape=jax.ShapeDtypeStruct((m, n), out_dtype),
        compiler_params=_params("parallel", "parallel"),
        name="proj",
    )(a, w)


def _mm_res_ln_kernel(a_ref, w_ref, res_ref, g_ref, b_ref, o_ref, ot_ref):
    y = _dot(a_ref[...], w_ref[...])
    out = _layer_norm_rows(DN_ALPHA * res_ref[...] + y, g_ref[...], b_ref[...])
    o_ref[...] = out
    ot_ref[...] = out.T.astype(BF16)


def matmul_res_ln(a, w, res, g, b, tm=256):
    m, k = a.shape
    n = w.shape[1]
    return pl.pallas_call(
        _mm_res_ln_kernel,
        grid=(m // tm,),
        in_specs=[pl.BlockSpec((tm, k), lambda i: (i, 0)),
                  pl.BlockSpec((k, n), lambda i: (0, 0)),
                  pl.BlockSpec((tm, n), lambda i: (i, 0)),
                  pl.BlockSpec((1, n), lambda i: (0, 0)),
                  pl.BlockSpec((1, n), lambda i: (0, 0))],
        out_specs=[pl.BlockSpec((tm, n), lambda i: (i, 0)),
                   pl.BlockSpec((n, tm), lambda i: (0, i))],
        out_shape=[jax.ShapeDtypeStruct((m, n), F32),
                   jax.ShapeDtypeStruct((n, m), BF16)],
        compiler_params=_params("parallel"),
        name="out_proj_ln",
    )(a, w, res, g.reshape(1, n), b.reshape(1, n))


def _na_row_start(r, rows):
    return jnp.clip(r - NA_KH // 2, 0, rows - NA_KH)


def _na_proj_kernel(h_ref, wqk_ref, wvt_ref, qk_ref, vt_ref):
    hb = h_ref[...].astype(BF16)
    qk_ref[...] = _dot(hb, wqk_ref[...]).astype(qk_ref.dtype)
    vt_ref[...] = _dot_nt(wvt_ref[...], hb).astype(vt_ref.dtype)


def na_project(h, w_in, tm=512):
    n, d = h.shape
    wqk = w_in[:, :2 * d].astype(BF16)
    wvt = w_in[:, 2 * d:].T.astype(BF16)
    return pl.pallas_call(
        _na_proj_kernel,
        grid=(n // tm,),
        in_specs=[pl.BlockSpec((tm, d), lambda i: (i, 0)),
                  pl.BlockSpec((d, 2 * d), lambda i: (0, 0)),
                  pl.BlockSpec((d, d), lambda i: (0, 0))],
        out_specs=[pl.BlockSpec((tm, 2 * d), lambda i: (i, 0)),
                   pl.BlockSpec((d, tm), lambda i: (0, i))],
        out_shape=[jax.ShapeDtypeStruct((n, 2 * d), BF16), jax.ShapeDtypeStruct((d, n), BF16)],
        compiler_params=_params("parallel"),
        name="na_proj",
    )(h, wqk, wvt)


NA_QROWS = 2
NA_KROWS = NA_KH + NA_QROWS
NA_EDGE = NA_KH // 4


def _na_key_start(i, rows):
    return jnp.clip(i - NA_KH // 4, 0, (rows - NA_KROWS) // 2)


def _na_kernel(q_ref, k_ref, vt_ref, bias_ref, o_ref):
    n_keys = NA_KROWS * GRID_W
    n_q = NA_QROWS * GRID_W
    lane = lax.broadcasted_iota(jnp.int32, (n_q, LANES), 1)
    low = lane < NA_HEAD_DIM
    row_low = lax.broadcasted_iota(jnp.int32, (LANES, n_q), 0) < NA_HEAD_DIM
    for p in range(NA_HEADS // 2):
        cols = slice(p * LANES, (p + 1) * LANES)
        qp = q_ref[0, :, :, cols].reshape(n_q, LANES)
        zq = jnp.zeros_like(qp)
        kp = k_ref[0, :, :, cols].reshape(n_keys, LANES)
        vtp = vt_ref[cols, :]
        outs = []
        for half in range(2):
            qh = jnp.where(low, qp, zq) if half == 0 else jnp.where(low, zq, qp)
            s = _dot_nt(kp, qh) * (NA_HEAD_DIM ** -0.5) + bias_ref[0, 2 * p + half]
            m = jnp.max(s, axis=0, keepdims=True)
            e = jnp.exp(s - m)
            l = jnp.sum(e, axis=0, keepdims=True)
            outs.append(_dot(vtp, e.astype(BF16)) / l)
        o2 = jnp.where(row_low, outs[0], outs[1]).T.astype(o_ref.dtype)
        for a in range(NA_QROWS):
            o_ref[a, :, cols] = o2[a * GRID_W:(a + 1) * GRID_W]


def _na_bias_table(rel_bias, rows):
    cols = np.arange(GRID_W)
    c0 = np.clip(cols - NA_KW // 2, 0, GRID_W - NA_KW)
    kc = np.arange(GRID_W)
    inside = (kc[None, :] >= c0[:, None]) & (kc[None, :] < c0[:, None] + NA_KW)
    dc = np.clip(kc[None, :] - cols[:, None] + (NA_KW - 1), 0, 2 * NA_KW - 2)
    b2 = jnp.where(inside[None, None], rel_bias[:, :, dc].astype(F32), MASK_NEG)
    steps = rows // NA_QROWS
    variant_steps = list(range(NA_EDGE)) + [NA_EDGE] + list(range(steps - NA_EDGE, steps))
    dr = np.zeros((len(variant_steps), NA_QROWS, NA_KROWS), np.int32)
    ok = np.zeros(dr.shape, bool)
    for v, i in enumerate(variant_steps):
        key0 = 2 * int(np.clip(i - NA_KH // 4, 0, (rows - NA_KROWS) // 2))
        for a in range(NA_QROWS):
            r = NA_QROWS * i + a
            r0 = int(np.clip(r - NA_KH // 2, 0, rows - NA_KH))
            key_rows = key0 + np.arange(NA_KROWS)
            ok[v, a] = (key_rows >= r0) & (key_rows < r0 + NA_KH)
            dr[v, a] = np.clip(key_rows - r + NA_KH - 1, 0, 2 * NA_KH - 2)
    t = b2[:, dr]
    t = jnp.where(ok[None, :, :, :, None, None], t, MASK_NEG)
    t = t.transpose(1, 0, 3, 5, 2, 4)
    return t.reshape(len(variant_steps), NA_HEADS, NA_KROWS * GRID_W, NA_QROWS * GRID_W)


def neighborhood_attention(qk, vt, rel_bias, batch, rows):
    d = D_MODEL
    steps = rows // NA_QROWS
    n_keys = NA_KROWS * GRID_W
    qk4 = qk.reshape(batch, rows, GRID_W, 2 * d)
    bias = _na_bias_table(rel_bias, rows)
    el = pl.Element
    key0 = lambda i: _na_key_start(i, rows)

    def variant(i):
        return jnp.where(i < NA_EDGE, i, jnp.where(i >= steps - NA_EDGE, i - (steps - 2 * NA_EDGE - 1), NA_EDGE))

    out = pl.pallas_call(
        _na_kernel,
        grid=(batch, steps),
        in_specs=[pl.BlockSpec((el(1), el(NA_QROWS), el(GRID_W), el(d)),
                               lambda b, i: (b, NA_QROWS * i, 0, 0)),
                  pl.BlockSpec((el(1), el(NA_KROWS), el(GRID_W), el(d)),
                               lambda b, i: (b, 2 * key0(i), 0, d)),
                  pl.BlockSpec((el(d), el(n_keys)),
                               lambda b, i: (0, (b * (rows // 2) + key0(i)) * (2 * GRID_W))),
                  pl.BlockSpec((1, NA_HEADS, n_keys, NA_QROWS * GRID_W),
                               lambda b, i: (variant(i), 0, 0, 0))],
        out_specs=pl.BlockSpec((None, NA_QROWS, GRID_W, d), lambda b, i: (b, i, 0, 0)),
        out_shape=jax.ShapeDtypeStruct((batch, rows, GRID_W, d), BF16),
        compiler_params=_params("parallel", "arbitrary"),
        name="na_attn",
    )(qk4, qk4, vt, bias)
    return out.reshape(batch * rows * GRID_W, d)


def _rms_rows(x, g):
    return x * lax.rsqrt(jnp.mean(x * x, axis=-1, keepdims=True) + NORM_EPS) * g


def _mla_proj_kernel(x_ref, win_ref, qn_ref, kvn_ref, wqa_ref, wqb_ref, wkv_ref, cos_ref, sin_ref,
                     q_ref, k_ref, v_ref):
    hd = MLA_HEADS * LANES
    hin = _dot(x_ref[...].astype(BF16), win_ref[...])
    cq = _rms_rows(hin[:, :MLA_Q_RANK], qn_ref[...]).astype(BF16)
    ckv = _rms_rows(hin[:, MLA_Q_RANK:MLA_Q_RANK + MLA_KV_RANK], kvn_ref[...]).astype(BF16)
    cos = cos_ref[...]
    sin = sin_ref[...]
    cos_t = jnp.tile(cos, (1, MLA_HEADS))
    sin_t = jnp.tile(sin, (1, MLA_HEADS))
    q = _dot(cq, wqa_ref[...]) * cos_t + _dot(cq, wqb_ref[...]) * sin_t
    q_ref[...] = (q * ((MLA_NOPE + MLA_ROPE) ** -0.5)).astype(q_ref.dtype)
    kv = _dot(ckv, wkv_ref[...])
    base = MLA_Q_RANK + MLA_KV_RANK
    kpe = hin[:, base:base + LANES] * cos + hin[:, base + LANES:base + 2 * LANES] * sin
    k_ref[...] = (kv[:, :hd] + jnp.tile(kpe, (1, MLA_HEADS))).astype(k_ref.dtype)
    v_ref[...] = kv[:, hd:].astype(v_ref.dtype)


def _mla_weights(w_in, w_q_up, w_kv_up):
    r = MLA_ROPE
    half = r // 2
    dq = MLA_NOPE + r
    nh = MLA_HEADS
    base = MLA_Q_RANK + MLA_KV_RANK
    kpe = w_in[:, base:base + r]
    zpad = lambda w, lo, hi: jnp.pad(w, ((0, 0), (lo, hi)))
    swap = lambda w: jnp.concatenate([-w[..., half:], w[..., :half]], axis=-1)
    kpe_a = zpad(kpe, MLA_NOPE, LANES - MLA_NOPE - r)
    kpe_b = zpad(swap(kpe), MLA_NOPE, LANES - MLA_NOPE - r)
    win = jnp.concatenate([w_in[:, :base], kpe_a, kpe_b], axis=1)
    wq = w_q_up.reshape(MLA_Q_RANK, nh, dq)
    pad3 = lambda w, lo, hi: jnp.pad(w, ((0, 0), (0, 0), (lo, hi)))
    wqa = pad3(wq, 0, LANES - dq).reshape(MLA_Q_RANK, nh * LANES)
    wqb = pad3(swap(wq[:, :, MLA_NOPE:]), MLA_NOPE, LANES - dq).reshape(MLA_Q_RANK, nh * LANES)
    wkv = w_kv_up.reshape(MLA_KV_RANK, nh, MLA_NOPE + MLA_V)
    wk = pad3(wkv[:, :, :MLA_NOPE], 0, LANES - MLA_NOPE).reshape(MLA_KV_RANK, nh * LANES)
    wv = wkv[:, :, MLA_NOPE:].reshape(MLA_KV_RANK, nh * MLA_V)
    return (win.astype(BF16), wqa.astype(BF16), wqb.astype(BF16),
            jnp.concatenate([wk, wv], axis=1).astype(BF16))


def _rope_tables(seq):
    half = MLA_ROPE // 2
    inv_freq = ROPE_THETA ** (-jnp.arange(half, dtype=F32) * 2.0 / MLA_ROPE)
    ang = jnp.arange(seq, dtype=F32)[:, None] * inv_freq[None, :]
    cos = jnp.cos(ang)
    sin = jnp.sin(ang)
    tail = LANES - MLA_NOPE - MLA_ROPE
    cos_p = jnp.concatenate([jnp.ones((seq, MLA_NOPE), F32), cos, cos, jnp.zeros((seq, tail), F32)], axis=1)
    sin_p = jnp.concatenate([jnp.zeros((seq, MLA_NOPE), F32), sin, sin, jnp.zeros((seq, tail), F32)], axis=1)
    return cos_p, sin_p


def mla_projections(h, w_in, q_norm, kv_norm, w_q_up, w_kv_up, seq, tm=256):
    n, d = h.shape
    win, wqa, wqb, wkv = _mla_weights(w_in, w_q_up, w_kv_up)
    cos_p, sin_p = _rope_tables(seq)
    hd = MLA_HEADS * LANES
    per_seq = seq // tm
    full = lambda a: pl.BlockSpec(a.shape, lambda i: (0,) * a.ndim)
    qn = q_norm.reshape(1, -1).astype(F32)
    kvn = kv_norm.reshape(1, -1).astype(F32)
    return pl.pallas_call(
        _mla_proj_kernel,
        grid=(n // tm,),
        in_specs=[pl.BlockSpec((tm, d), lambda i: (i, 0)), full(win), full(qn), full(kvn),
                  full(wqa), full(wqb), full(wkv),
                  pl.BlockSpec((tm, LANES), lambda i: (i % per_seq, 0)),
                  pl.BlockSpec((tm, LANES), lambda i: (i % per_seq, 0))],
        out_specs=[pl.BlockSpec((tm, hd), lambda i: (i, 0)),
                   pl.BlockSpec((tm, hd), lambda i: (i, 0)),
                   pl.BlockSpec((tm, MLA_HEADS * MLA_V), lambda i: (i, 0))],
        out_shape=[jax.ShapeDtypeStruct((n, hd), BF16),
                   jax.ShapeDtypeStruct((n, hd), BF16),
                   jax.ShapeDtypeStruct((n, MLA_HEADS * MLA_V), BF16)],
        compiler_params=_params("parallel"),
        name="mla_proj",
    )(h, win, qn, kvn, wqa, wqb, wkv, cos_p, sin_p)


def _mla_attn_kernel(q_ref, k_ref, v_ref, o_ref):
    v = v_ref[...]
    lane = lax.broadcasted_iota(jnp.int32, o_ref.shape, 1)
    halves = []
    for half in range(2):
        cols = slice(half * LANES, (half + 1) * LANES)
        s = _dot_nt(q_ref[:, cols], k_ref[:, cols])
        m = jnp.max(s, axis=-1, keepdims=True)
        e = jnp.exp(s - m)
        l = jnp.sum(e, axis=-1, keepdims=True)
        halves.append(_dot(e.astype(BF16), v) / l)
    o_ref[...] = jnp.where(lane < MLA_V, halves[0], halves[1]).astype(o_ref.dtype)


def mla_attention(q, k, v, batch, seq, tq=256):
    hd = MLA_HEADS * LANES
    q3 = q.reshape(batch, seq, hd)
    k3 = k.reshape(batch, seq, hd)
    v3 = v.reshape(batch, seq, MLA_HEADS * MLA_V)
    out = pl.pallas_call(
        _mla_attn_kernel,
        grid=(batch, MLA_HEADS // 2, seq // tq),
        in_specs=[pl.BlockSpec((None, tq, 2 * LANES), lambda b, p, i: (b, i, p)),
                  pl.BlockSpec((None, seq, 2 * LANES), lambda b, p, i: (b, 0, p)),
                  pl.BlockSpec((None, seq, LANES), lambda b, p, i: (b, 0, p))],
        out_specs=pl.BlockSpec((None, tq, LANES), lambda b, p, i: (b, i, p)),
        out_shape=jax.ShapeDtypeStruct((batch, seq, MLA_HEADS * MLA_V), BF16),
        compiler_params=_params("parallel", "parallel", "arbitrary"),
        name="mla_attn",
    )(q3, k3, v3)
    return out.reshape(batch * seq, MLA_HEADS * MLA_V)


HG_LEVELS = int(math.log2(HG_CHUNK))
HG_UNROLL = 1


def _hg_constants(reverse):
    c = HG_CHUNK
    t = np.arange(c)[:, None]
    u = np.arange(c)[None, :]
    if not reverse:
        incl = u <= t
        rest = u > t
    else:
        incl = u >= t
        rest = u < t
    mats = [incl, rest]
    masks = []
    roles = []
    for lvl in range(1, HG_LEVELS + 1):
        size = 1 << lvl
        start = (t // size) * size
        mid = start + size // 2
        upper = t >= mid
        if not reverse:
            q_side = (u >= mid) & (u <= t)
            k_side = (u > t) & (u <= mid - 1)
            is_query = upper
        else:
            q_side = (u >= t) & (u < mid)
            k_side = (u >= mid) & (u < t)
            is_query = ~upper
        mats.append(np.where(is_query, q_side, k_side))
        same = (t // size) == (u // size)
        key_row = (~is_query).T
        masks.append(same & is_query & np.broadcast_to(key_row, (c, c)))
        roles.append(np.broadcast_to(is_query, (c, LANES)))
    w = np.concatenate(mats, axis=0).astype(np.float32)
    return (jnp.asarray(w, BF16), jnp.asarray(np.stack(masks).astype(np.float32)),
            jnp.asarray(np.stack(roles).astype(np.float32)))


def _hg_chunk(q, zf, v_b, lb, w_ref, m_ref, r_ref, st, total_row):
    c = HG_CHUNK
    sg = jax.nn.sigmoid(zf)
    k = (1.0 - lb) * jax.nn.sigmoid(-zf)
    lf = jnp.log(lb + (1.0 - lb) * sg)
    hi = lf.astype(BF16)
    lo = (lf - hi.astype(F32)).astype(BF16)
    f = lf.shape[1]
    ex2 = _dot(w_ref[...], jnp.concatenate([hi, lo], axis=1))
    ex = ex2[:, :f] + ex2[:, f:]
    b_incl = ex[0:c]
    total = ex[total_row:total_row + 1]
    qd = q * jnp.exp(b_incl)
    kd = k * jnp.exp(ex[c:2 * c])
    row = lax.broadcasted_iota(jnp.int32, (c, c), 0)
    col = lax.broadcasted_iota(jnp.int32, (c, c), 1)
    a = jnp.where(row == col, jnp.sum(q * k, axis=-1, keepdims=True), 0.0)
    xs = []
    for lvl in range(HG_LEVELS):
        x = jnp.where(r_ref[lvl] > 0.5, q, k) * jnp.exp(ex[(2 + lvl) * c:(3 + lvl) * c])
        xs.append(x.astype(BF16))
    for lvl in range(0, HG_LEVELS, 2):
        pair = xs[lvl:lvl + 2]
        xb = jnp.concatenate(pair, axis=0)
        g = _dot_nt(xb, xb)
        for j in range(len(pair)):
            a = a + m_ref[lvl + j] * g[j * c:(j + 1) * c, j * c:(j + 1) * c]
    o = _dot(a.astype(BF16), v_b) + _dot_nt(qd.astype(BF16), st.astype(BF16))
    return o, st * jnp.exp(total) + _dot_tn(v_b, kd.astype(BF16))


def _hg_kernel(zq_ref, zff_ref, zfb_ref, zi_ref, zg_ref, lb_ref, g_ref,
               wf_ref, mf_ref, rf_ref, wb_ref, mb_ref, rb_ref,
               o_ref, accf_ref, accb_ref):
    c = HG_CHUNK
    seq = zq_ref.shape[0]
    n = seq // c
    lb_f = lb_ref[0:1, :]
    lb_b = lb_ref[1:2, :]

    def gated(ref, rows):
        z = ref[rows, :]
        return z * jax.nn.sigmoid(z)

    def body(i, carry):
        st_f, st_b = carry
        rf = pl.ds(pl.multiple_of(i * c, c), c)
        rb = pl.ds(pl.multiple_of((n - 1 - i) * c, c), c)
        of, st_f = _hg_chunk(gated(zq_ref, rf), zff_ref[rf, :], zi_ref[rf, :].astype(BF16), lb_f,
                             wf_ref, mf_ref, rf_ref, st_f, c - 1)
        accf_ref[rf, :] = of
        ob, st_b = _hg_chunk(gated(zq_ref, rb), zfb_ref[rb, :], zi_ref[rb, :].astype(BF16), lb_b,
                             wb_ref, mb_ref, rb_ref, st_b, 0)
        accb_ref[rb, :] = ob
        return st_f, st_b

    zero = jnp.zeros((HG_F, HG_F), F32)
    lax.fori_loop(0, n, body, (zero, zero), unroll=HG_UNROLL)

    def finish(i, carry):
        rows = pl.ds(pl.multiple_of(i * c, c), c)
        o = accf_ref[rows, :] + accb_ref[rows, :]
        o = o * lax.rsqrt(jnp.mean(o * o, axis=-1, keepdims=True) + NORM_EPS)
        o_ref[rows, :] = (o * g_ref[...] * gated(zg_ref, rows)).astype(o_ref.dtype)
        return carry

    lax.fori_loop(0, n, finish, 0)


def hgrn2_scan(z, lb, norm_g, batch, seq):
    d = D_MODEL
    z3 = z.reshape(batch, seq, 5 * d)
    wf, mf, rf = _hg_constants(False)
    wb, mb, rb = _hg_constants(True)
    zspec = lambda j: pl.BlockSpec((None, seq, HG_F), lambda b, h: (b, 0, j * HG_HEADS + h))
    full = lambda a: pl.BlockSpec(a.shape, lambda b, h: (0,) * a.ndim)
    out = pl.pallas_call(
        _hg_kernel,
        grid=(batch, HG_HEADS),
        in_specs=[zspec(0), zspec(1), zspec(2), zspec(3), zspec(4),
                  pl.BlockSpec((2, HG_F), lambda b, h: (0, h)),
                  pl.BlockSpec((1, HG_F), lambda b, h: (0, h)),
                  full(wf), full(mf), full(rf), full(wb), full(mb), full(rb)],
        out_specs=pl.BlockSpec((None, seq, HG_F), lambda b, h: (b, 0, h)),
        out_shape=jax.ShapeDtypeStruct((batch, seq, d), BF16),
        scratch_shapes=[pltpu.VMEM((seq, HG_F), F32), pltpu.VMEM((seq, HG_F), F32)],
        compiler_params=_params("parallel", "parallel"),
        name="hgrn2_scan",
    )(z3, z3, z3, z3, z3, lb.astype(F32), norm_g.reshape(1, d).astype(F32),
      wf, mf, rf, wb, mb, rb)
    return out.reshape(batch * seq, d)


def _merge_sort_network(n):
    pairs = []
    p = 1
    while p < n:
        k = p
        while k >= 1:
            for j in range(k % p, n - k, 2 * k):
                for i in range(min(k, n - j - k)):
                    if (i + j) // (2 * p) == (i + j + k) // (2 * p):
                        pairs.append((i + j, i + j + k))
            k //= 2
        p *= 2
    return pairs


def _top_values(s, count):
    rows = s.shape[0]
    n = rows // F32_SUBLANES
    size = 1 << (n - 1).bit_length()
    neg = jnp.full((F32_SUBLANES, s.shape[1]), -jnp.inf, F32)
    lists = [s[i * F32_SUBLANES:(i + 1) * F32_SUBLANES] for i in range(n)] + [neg] * (size - n)
    for i, j in _merge_sort_network(size):
        hi = jnp.maximum(lists[i], lists[j])
        lists[j] = jnp.minimum(lists[i], lists[j])
        lists[i] = hi
    lists = lists[:min(size, count)]
    vals = []
    for it in range(count):
        head = lists[0]
        m = jnp.max(head, axis=0, keepdims=True)
        vals.append(m)
        hit = head >= m
        depth = min(len(lists), count - it - 1)
        for k in range(depth):
            nxt = lists[k + 1] if k + 1 < len(lists) else neg
            lists[k] = jnp.where(hit, nxt, lists[k])
        lists = lists[:max(depth, 1)]
    return jnp.concatenate(vals, axis=0)


def _peer_route_kernel(ht_ref, wq_ref, keys_ref, cnt_ref, a0_ref, r1_ref, b1_ref):
    kk = PEER_TOPK
    ht = ht_ref[...]
    for h in range(PEER_HEADS):
        scores = []
        for c in range(2):
            g = 2 * h + c
            qt = _dot(wq_ref[g * LANES:(g + 1) * LANES, :], ht)
            scores.append(_dot(keys_ref[c], qt.astype(BF16)))
        s0, s1 = scores
        sv0 = _top_values(s0, kk)
        sv1 = _top_values(s1, kk)
        r1 = jnp.full(s1.shape, float(kk), F32)
        for b in range(kk):
            r1 = jnp.where(s1 == sv1[b:b + 1], float(b), r1)
        cands = [sv0[a:a + 1] + sv1[0:kk // (a + 1)] for a in range(kk)]
        n_cand = sum(kk // (a + 1) for a in range(kk))
        pad = (-n_cand) % 8
        if pad:
            cands.append(jnp.full((pad, ht.shape[1]), -jnp.inf, F32))
        tau = _top_values(jnp.concatenate(cands, axis=0), kk)[kk - 1:kk]
        e0 = jnp.exp(sv0 - sv0[0:1])
        e1 = jnp.exp(sv1 - sv1[0:1])
        z = jnp.zeros_like(tau)
        for a in range(kk):
            nb = kk // (a + 1)
            sel = (sv0[a:a + 1] + sv1[0:nb]) >= tau
            z = z + e0[a:a + 1] * jnp.sum(jnp.where(sel, e1[0:nb], 0.0), axis=0, keepdims=True)
        cnt = jnp.zeros(s0.shape, F32)
        for a in range(kk):
            pair_ok = (sv0[a:a + 1] + sv1) >= tau
            n_a = jnp.sum(jnp.where(pair_ok, 1.0, 0.0), axis=0, keepdims=True)
            cnt = jnp.where(s0 == sv0[a:a + 1], n_a, cnt)
        cnt_ref[h] = cnt
        a0_ref[h] = jnp.exp(s0 - sv0[0:1])
        r1_ref[h] = r1.astype(BF16)
        b1_ref[h] = (jnp.exp(s1 - sv1[0:1]) * (0.5 / z)).astype(BF16)


def peer_route(ht, wq_t, keys, tt=256):
    d, n = ht.shape
    shape = (PEER_HEADS, PEER_NKEYS, n)
    ospec = pl.BlockSpec((PEER_HEADS, PEER_NKEYS, tt), lambda i: (0, 0, i))
    return pl.pallas_call(
        _peer_route_kernel,
        grid=(n // tt,),
        in_specs=[pl.BlockSpec((d, tt), lambda i: (0, i)),
                  pl.BlockSpec(wq_t.shape, lambda i: (0, 0)),
                  pl.BlockSpec(keys.shape, lambda i: (0, 0, 0))],
        out_specs=[ospec, ospec, ospec, ospec],
        out_shape=[jax.ShapeDtypeStruct(shape, F32), jax.ShapeDtypeStruct(shape, F32),
                   jax.ShapeDtypeStruct(shape, BF16), jax.ShapeDtypeStruct(shape, BF16)],
        compiler_params=_params("parallel"),
        name="peer_route",
    )(ht, wq_t, keys)


def _peer_gate_chunk(hid, i0, cnt_ref, a0_ref, r1_ref, b1_ref):
    tt = hid.shape[1]
    zero = jnp.zeros((PEER_NKEYS, LANES), BF16)

    def row_tile(ref, h, ii, cols):
        row = jnp.broadcast_to(ref[h, ii:ii + 1, cols], (BF16_SUBLANES, LANES)).astype(BF16)
        return pltpu.repeat(row, PEER_NKEYS // BF16_SUBLANES, axis=0)

    out_rows = []
    for k in range(PEER_CHUNK_ROWS):
        tiles = []
        for tc in range(tt // LANES):
            cols = slice(tc * LANES, (tc + 1) * LANES)
            gate = zero
            for h in range(PEER_HEADS):
                cnt = row_tile(cnt_ref, h, i0 + k, cols)
                a0 = row_tile(a0_ref, h, i0 + k, cols)
                gate = gate + jnp.where(r1_ref[h, :, cols] < cnt, a0 * b1_ref[h, :, cols], zero)
            x = hid[k * PEER_NKEYS:(k + 1) * PEER_NKEYS, cols]
            act = x.astype(BF16) * (1.0 + lax.erf(x * RSQRT2)).astype(BF16)
            tiles.append(gate * act)
        out_rows.append(jnp.concatenate(tiles, axis=1))
    return jnp.concatenate(out_rows, axis=0)


def _pack_rows_kernel(x_ref, o_ref):
    o_ref[...] = pltpu.bitcast(x_ref[...].astype(BF16), jnp.uint32)


def _pack_rows_t_kernel(x_ref, o_ref):
    o_ref[...] = pltpu.bitcast(x_ref[...].T.astype(BF16), jnp.uint32)


def pack_expert_tables(u, v):
    layers, n_exp, d = u.shape
    eb = PEER_EXPERT_BLOCK
    nb = n_exp // eb
    u_packed = pl.pallas_call(
        _pack_rows_kernel,
        grid=(layers, nb),
        in_specs=[pl.BlockSpec((None, eb, d), lambda l, e: (l, e, 0))],
        out_specs=pl.BlockSpec((None, eb // 2, d), lambda l, e: (l, e, 0)),
        out_shape=jax.ShapeDtypeStruct((layers, n_exp // 2, d), jnp.uint32),
        compiler_params=_params("parallel", "parallel"),
        name="pack_u",
    )(u)
    v_packed = pl.pallas_call(
        _pack_rows_t_kernel,
        grid=(layers, nb),
        in_specs=[pl.BlockSpec((None, eb, d), lambda l, e: (l, e, 0))],
        out_specs=pl.BlockSpec((None, None, d // 2, eb), lambda l, e: (l, e, 0, 0)),
        out_shape=jax.ShapeDtypeStruct((layers, nb, d // 2, eb), jnp.uint32),
        compiler_params=_params("parallel", "parallel"),
        name="pack_vt",
    )(v)
    return u_packed, v_packed


def _unpack_rows(words):
    return pltpu.bitcast(words, BF16)


def _ordered_after(x, dep):
    z = pltpu.bitcast(dep[:BF16_SUBLANES, :LANES], jnp.uint32)
    z = (z >> 16) >> 16
    zero = pltpu.bitcast(z, BF16)
    zero = pltpu.repeat(pltpu.repeat(zero, x.shape[0] // BF16_SUBLANES, axis=0), x.shape[1] // LANES, axis=1)
    return x + zero


def _peer_ffn_kernel(ht_ref, u0_ref, ub_ref, un_ref, vta_ref, vtb_ref, cnta_ref, a0a_ref, cntb_ref, a0b_ref,
                     r1_in_ref, b1_in_ref, res_ref, g_ref, b_ref,
                     o_ref, acc_ref, hida_ref, hidb_ref, r1_ref, b1_ref):
    s = pl.program_id(1)
    half = ht_ref.shape[1] // 2

    @pl.when(s == 0)
    def _():
        acc_ref[...] = jnp.zeros_like(acc_ref)
        r1_ref[...] = r1_in_ref[...]
        b1_ref[...] = b1_in_ref[...]
        hida_ref[...] = _dot(_unpack_rows(u0_ref[...]), ht_ref[...])

    chunk = PEER_CHUNK_ROWS * PEER_NKEYS
    n_chunks = PEER_EXPERT_BLOCK // chunk

    def run_block(acc, hid_ref, vt_ref, cnt_ref, a0_ref, next_u_ref, next_hid_ref):
        for c in range(n_chunks):
            span = slice(c * chunk, (c + 1) * chunk)
            w = _peer_gate_chunk(hid_ref[span, :], c * PEER_CHUNK_ROWS, cnt_ref, a0_ref, r1_ref, b1_ref)
            acc = acc + _dot(_unpack_rows(vt_ref[:, span]), w)
            if c % (n_chunks // 2) == n_chunks // 2 - 1:
                hh = c // (n_chunks // 2)
                tcols = slice(hh * half, (hh + 1) * half)
                next_hid_ref[:, tcols] = _dot(_unpack_rows(next_u_ref[...]),
                                              _ordered_after(ht_ref[:, tcols], w))
        return acc

    acc = run_block(acc_ref[...], hida_ref, vta_ref, cnta_ref, a0a_ref, ub_ref, hidb_ref)
    acc_ref[...] = run_block(acc, hidb_ref, vtb_ref, cntb_ref, a0b_ref, un_ref, hida_ref)

    @pl.when(s == pl.num_programs(1) - 1)
    def _():
        y = acc_ref[...].T
        o_ref[...] = _layer_norm_rows(DN_ALPHA * res_ref[...] + y, g_ref[...], b_ref[...])


def peer_ffn(h, ht, route, u, v_blocks, layer, g, b, tt=512):
    n, d = h.shape
    eb = PEER_EXPERT_BLOCK
    last = 2 * u.shape[1] // eb - 1
    rspec = pl.BlockSpec((PEER_HEADS, PEER_NKEYS, tt), lambda i, s: (0, 0, i))
    rowspec = lambda off: pl.BlockSpec((PEER_HEADS, eb // PEER_NKEYS, tt), lambda i, s: (0, 2 * s + off, i))
    cnt, a0, r1, b1 = route
    return pl.pallas_call(
        _peer_ffn_kernel,
        grid=(n // tt, (last + 1) // 2),
        in_specs=[pl.BlockSpec((d, tt), lambda i, s: (0, i)),
                  pl.BlockSpec((None, eb // 2, d), lambda i, s: (layer, 0, 0)),
                  pl.BlockSpec((None, eb // 2, d), lambda i, s: (layer, 2 * s + 1, 0)),
                  pl.BlockSpec((None, eb // 2, d), lambda i, s: (layer, jnp.minimum(2 * s + 2, last), 0)),
                  pl.BlockSpec((None, None, d // 2, eb), lambda i, s: (layer, 2 * s, 0, 0)),
                  pl.BlockSpec((None, None, d // 2, eb), lambda i, s: (layer, 2 * s + 1, 0, 0)),
                  rowspec(0), rowspec(0), rowspec(1), rowspec(1),
                  rspec, rspec,
                  pl.BlockSpec((tt, d), lambda i, s: (i, 0)),
                  pl.BlockSpec((1, d), lambda i, s: (0, 0)),
                  pl.BlockSpec((1, d), lambda i, s: (0, 0))],
        out_specs=pl.BlockSpec((tt, d), lambda i, s: (i, 0)),
        out_shape=jax.ShapeDtypeStruct((n, d), F32),
        scratch_shapes=[pltpu.VMEM((d, tt), F32), pltpu.VMEM((eb, tt), F32), pltpu.VMEM((eb, tt), F32),
                        pltpu.VMEM((PEER_HEADS, PEER_NKEYS, tt), BF16),
                        pltpu.VMEM((PEER_HEADS, PEER_NKEYS, tt), BF16)],
        compiler_params=_params("parallel", "arbitrary"),
        name="peer_ffn",
    )(ht, u, u, u, v_blocks, v_blocks, cnt, a0, cnt, a0, r1, b1, h, g.reshape(1, d), b.reshape(1, d))


def kernel(x, na_w_in, na_rel_bias, na_w_out, mla_w_in, mla_q_norm, mla_kv_norm, mla_w_q_up, mla_w_kv_up, mla_w_out, hg_w_in, hg_lower_bound, hg_norm, hg_w_out, peer_w_q, peer_sub_keys, peer_u, peer_v, ln_mix_g, ln_mix_b, ln_ffn_g, ln_ffn_b):
    batch, seq, d = x.shape
    rows = seq // GRID_W
    lb_w = jax.nn.softmax(hg_lower_bound.astype(F32), axis=0)
    lb_all = jnp.cumsum(lb_w, axis=0) - lb_w[0:1]
    h = x.reshape(batch * seq, d)
    u_all, v_all = pack_expert_tables(peer_u, peer_v)
    for layer in range(DEPTH):
        kind = layer % N_MIXERS
        j = layer // N_MIXERS
        if kind == 0:
            qk, vt = na_project(h, na_w_in[j])
            mix_in = neighborhood_attention(qk, vt, na_rel_bias[j], batch, rows)
            w_out = na_w_out[j]
        elif kind == 1:
            q, k, v = mla_projections(h, mla_w_in[j], mla_q_norm[j], mla_kv_norm[j],
                                      mla_w_q_up[j], mla_w_kv_up[j], seq)
            mix_in = mla_attention(q, k, v, batch, seq)
            w_out = mla_w_out[j]
        else:
            z = matmul(h, hg_w_in[j].astype(BF16), F32)
            mix_in = hgrn2_scan(z, lb_all[layer], hg_norm[j], batch, seq)
            w_out = hg_w_out[j]
        h, ht = matmul_res_ln(mix_in, w_out.astype(BF16), h, ln_mix_g[layer], ln_mix_b[layer])
        route = peer_route(ht, peer_w_q[layer].T.astype(BF16), peer_sub_keys[layer].astype(BF16))
        h = peer_ffn(h, ht, route, u_all, v_all, layer, ln_ffn_g[layer], ln_ffn_b[layer])
    return h.reshape(batch, seq, d)
```

```python
import functools
import math

import jax
import jax.numpy as jnp
import numpy as np
from jax import lax
from jax.experimental import pallas as pl
from jax.experimental.pallas import tpu as pltpu

F32 = jnp.float32
BF16 = jnp.bfloat16

D_MODEL = 1024
DEPTH = 4
GRID_W = 64
N_MIXERS = 3

NA_HEADS = 16
NA_HEAD_DIM = 64
NA_KH = 8
NA_KW = 16

MLA_HEADS = 16
MLA_NOPE = 64
MLA_ROPE = 32
MLA_V = 64
MLA_Q_RANK = 256
MLA_KV_RANK = 256
ROPE_THETA = 10000.0

HG_HEADS = 8
HG_F = 128
HG_CHUNK = 128

PEER_HEADS = 8
PEER_NKEYS = 128
PEER_TOPK = 16
PEER_EXPERT_BLOCK = 1024
PEER_CHUNK_ROWS = 2
NORM_EPS = 1e-5
DN_ALPHA = (2.0 * DEPTH) ** 0.25

LANES = 128
F32_SUBLANES = 8
BF16_SUBLANES = 16
VMEM_LIMIT = 48 * 1024 * 1024
MASK_NEG = -1e30
RSQRT2 = 0.7071067811865476


def _params(*sem):
    return pltpu.CompilerParams(dimension_semantics=sem, vmem_limit_bytes=VMEM_LIMIT)


def _layer_norm_rows(z, g, b):
    mu = jnp.mean(z, axis=-1, keepdims=True)
    zc = z - mu
    var = jnp.mean(zc * zc, axis=-1, keepdims=True)
    return zc * lax.rsqrt(var + NORM_EPS) * g + b


def _dot(a, b):
    return jnp.dot(a, b, preferred_element_type=F32)


def _dot_nt(a, b):
    return lax.dot_general(a, b, (((1,), (1,)), ((), ())), preferred_element_type=F32)


def _dot_tn(a, b):
    return lax.dot_general(a, b, (((0,), (0,)), ((), ())), preferred_element_type=F32)


def _mm_kernel(a_ref, w_ref, o_ref):
    o_ref[...] = _dot(a_ref[...].astype(BF16), w_ref[...]).astype(o_ref.dtype)


def matmul(a, w, out_dtype, tm=512, tn=1024):
    m, k = a.shape
    n = w.shape[1]
    tn = min(tn, n)
    return pl.pallas_call(
        _mm_kernel,
        grid=(m // tm, n // tn),
        in_specs=[pl.BlockSpec((tm, k), lambda i, j: (i, 0)),
                  pl.BlockSpec((k, tn), lambda i, j: (0, j))],
        out_specs=pl.BlockSpec((tm, tn), lambda i, j: (i, j)),
        out_shape=jax.ShapeDtypeStruct((m, n), out_dtype),
        compiler_params=_params("parallel", "parallel"),
        name="proj",
    )(a, w)


def _mm_res_ln_kernel(a_ref, w_ref, res_ref, g_ref, b_ref, o_ref, ot_ref):
    y = _dot(a_ref[...], w_ref[...])
    out = _layer_norm_rows(DN_ALPHA * res_ref[...] + y, g_ref[...], b_ref[...])
    o_ref[...] = out
    ot_ref[...] = out.T.astype(BF16)


def matmul_res_ln(a, w, res, g, b, tm=512):
    m, k = a.shape
    n = w.shape[1]
    return pl.pallas_call(
        _mm_res_ln_kernel,
        grid=(m // tm,),
        in_specs=[pl.BlockSpec((tm, k), lambda i: (i, 0)),
                  pl.BlockSpec((k, n), lambda i: (0, 0)),
                  pl.BlockSpec((tm, n), lambda i: (i, 0)),
                  pl.BlockSpec((1, n), lambda i: (0, 0)),
                  pl.BlockSpec((1, n), lambda i: (0, 0))],
        out_specs=[pl.BlockSpec((tm, n), lambda i: (i, 0)),
                   pl.BlockSpec((n, tm), lambda i: (0, i))],
        out_shape=[jax.ShapeDtypeStruct((m, n), F32),
                   jax.ShapeDtypeStruct((n, m), BF16)],
        compiler_params=_params("parallel"),
        name="out_proj_ln",
    )(a, w, res, g.reshape(1, n), b.reshape(1, n))


def _na_row_start(r, rows):
    return jnp.clip(r - NA_KH // 2, 0, rows - NA_KH)


def _na_proj_kernel(h_ref, wqk_ref, wvt_ref, qk_ref, vt_ref):
    hb = h_ref[...].astype(BF16)
    qk_ref[...] = _dot(hb, wqk_ref[...]).astype(qk_ref.dtype).reshape(qk_ref.shape)
    vt_ref[...] = _dot_nt(wvt_ref[...], hb).astype(vt_ref.dtype)


def na_project(h, w_in, batch, tm=512):
    n, d = h.shape
    rows = n // (batch * GRID_W)
    tile_rows = tm // GRID_W
    tiles = rows // tile_rows
    wqk = w_in[:, :2 * d].astype(BF16)
    wvt = w_in[:, 2 * d:].T.astype(BF16)
    return pl.pallas_call(
        _na_proj_kernel,
        grid=(n // tm,),
        in_specs=[pl.BlockSpec((tm, d), lambda i: (i, 0)),
                  pl.BlockSpec((d, 2 * d), lambda i: (0, 0)),
                  pl.BlockSpec((d, d), lambda i: (0, 0))],
        out_specs=[pl.BlockSpec((None, tile_rows, GRID_W, 2 * d), lambda i: (i // tiles, i % tiles, 0, 0)),
                   pl.BlockSpec((d, tm), lambda i: (0, i))],
        out_shape=[jax.ShapeDtypeStruct((batch, rows, GRID_W, 2 * d), BF16),
                   jax.ShapeDtypeStruct((d, n), BF16)],
        compiler_params=_params("parallel"),
        name="na_proj",
    )(h, wqk, wvt)


NA_QROWS = 2
NA_KROWS = NA_KH + NA_QROWS
NA_EDGE = NA_KH // 4


def _na_key_start(i, rows):
    return jnp.clip(i - NA_KH // 4, 0, (rows - NA_KROWS) // 2)


def _na_kernel(q_ref, k_ref, vt_ref, bias_ref, o_ref):
    n_keys = NA_KROWS * GRID_W
    n_q = NA_QROWS * GRID_W
    lane = lax.broadcasted_iota(jnp.int32, (n_q, LANES), 1)
    low = lane < NA_HEAD_DIM
    row_low = lax.broadcasted_iota(jnp.int32, (LANES, n_q), 0) < NA_HEAD_DIM
    for p in range(NA_HEADS // 2):
        cols = slice(p * LANES, (p + 1) * LANES)
        qp = q_ref[0, :, :, cols].reshape(n_q, LANES)
        zq = jnp.zeros_like(qp)
        kp = k_ref[0, :, :, cols].reshape(n_keys, LANES)
        vtp = vt_ref[cols, :]
        outs = []
        for half in range(2):
            qh = jnp.where(low, qp, zq) if half == 0 else jnp.where(low, zq, qp)
            s = _dot_nt(kp, qh) * (NA_HEAD_DIM ** -0.5) + bias_ref[0, 2 * p + half]
            m = jnp.max(s, axis=0, keepdims=True)
            e = jnp.exp(s - m)
            l = jnp.sum(e, axis=0, keepdims=True)
            outs.append(_dot(vtp, e.astype(BF16)) / l)
        o2 = jnp.where(row_low, outs[0], outs[1]).T.astype(o_ref.dtype)
        o_ref[:, cols] = o2


def _na_bias_table(rel_bias, rows):
    cols = np.arange(GRID_W)
    c0 = np.clip(cols - NA_KW // 2, 0, GRID_W - NA_KW)
    kc = np.arange(GRID_W)
    inside = (kc[None, :] >= c0[:, None]) & (kc[None, :] < c0[:, None] + NA_KW)
    dc = np.clip(kc[None, :] - cols[:, None] + (NA_KW - 1), 0, 2 * NA_KW - 2)
    b2 = jnp.where(inside[None, None], rel_bias[:, :, dc].astype(F32), MASK_NEG)
    steps = rows // NA_QROWS
    variant_steps = list(range(NA_EDGE)) + [NA_EDGE] + list(range(steps - NA_EDGE, steps))
    dr = np.zeros((len(variant_steps), NA_QROWS, NA_KROWS), np.int32)
    ok = np.zeros(dr.shape, bool)
    for v, i in enumerate(variant_steps):
        key0 = 2 * int(np.clip(i - NA_KH // 4, 0, (rows - NA_KROWS) // 2))
        for a in range(NA_QROWS):
            r = NA_QROWS * i + a
            r0 = int(np.clip(r - NA_KH // 2, 0, rows - NA_KH))
            key_rows = key0 + np.arange(NA_KROWS)
            ok[v, a] = (key_rows >= r0) & (key_rows < r0 + NA_KH)
            dr[v, a] = np.clip(key_rows - r + NA_KH - 1, 0, 2 * NA_KH - 2)
    t = b2[:, dr]
    t = jnp.where(ok[None, :, :, :, None, None], t, MASK_NEG)
    t = t.transpose(1, 0, 3, 5, 2, 4)
    return t.reshape(len(variant_steps), NA_HEADS, NA_KROWS * GRID_W, NA_QROWS * GRID_W)


def neighborhood_attention(qk4, vt, rel_bias):
    d = D_MODEL
    batch, rows = qk4.shape[:2]
    steps = rows // NA_QROWS
    n_keys = NA_KROWS * GRID_W
    bias = _na_bias_table(rel_bias, rows)
    el = pl.Element
    key0 = lambda i: _na_key_start(i, rows)

    def variant(i):
        return jnp.where(i < NA_EDGE, i, jnp.where(i >= steps - NA_EDGE, i - (steps - 2 * NA_EDGE - 1), NA_EDGE))

    return pl.pallas_call(
        _na_kernel,
        grid=(batch, steps),
        in_specs=[pl.BlockSpec((el(1), el(NA_QROWS), el(GRID_W), el(d)),
                               lambda b, i: (b, NA_QROWS * i, 0, 0)),
                  pl.BlockSpec((el(1), el(NA_KROWS), el(GRID_W), el(d)),
                               lambda b, i: (b, 2 * key0(i), 0, d)),
                  pl.BlockSpec((el(d), el(n_keys)),
                               lambda b, i: (0, (b * (rows // 2) + key0(i)) * (2 * GRID_W))),
                  pl.BlockSpec((1, NA_HEADS, n_keys, NA_QROWS * GRID_W),
                               lambda b, i: (variant(i), 0, 0, 0))],
        out_specs=pl.BlockSpec((NA_QROWS * GRID_W, d), lambda b, i: (b * steps + i, 0)),
        out_shape=jax.ShapeDtypeStruct((batch * rows * GRID_W, d), BF16),
        compiler_params=_params("parallel", "arbitrary"),
        name="na_attn",
    )(qk4, qk4, vt, bias)


def _rms_rows(x, g):
    return x * lax.rsqrt(jnp.mean(x * x, axis=-1, keepdims=True) + NORM_EPS) * g


def _mla_proj_kernel(x_ref, win_ref, qn_ref, kvn_ref, wqa_ref, wqb_ref, wkv_ref, cos_ref, sin_ref,
                     q_ref, k_ref, v_ref):
    hd = MLA_HEADS * LANES
    hin = _dot(x_ref[...].astype(BF16), win_ref[...])
    cq = _rms_rows(hin[:, :MLA_Q_RANK], qn_ref[...]).astype(BF16)
    ckv = _rms_rows(hin[:, MLA_Q_RANK:MLA_Q_RANK + MLA_KV_RANK], kvn_ref[...]).astype(BF16)
    cos = cos_ref[...]
    sin = sin_ref[...]
    cos_t = jnp.tile(cos, (1, MLA_HEADS))
    sin_t = jnp.tile(sin, (1, MLA_HEADS))
    q = _dot(cq, wqa_ref[...]) * cos_t + _dot(cq, wqb_ref[...]) * sin_t
    q_ref[...] = (q * ((MLA_NOPE + MLA_ROPE) ** -0.5)).astype(q_ref.dtype)
    kv = _dot(ckv, wkv_ref[...])
    base = MLA_Q_RANK + MLA_KV_RANK
    kpe = hin[:, base:base + LANES] * cos + hin[:, base + LANES:base + 2 * LANES] * sin
    k_ref[...] = (kv[:, :hd] + jnp.tile(kpe, (1, MLA_HEADS))).astype(k_ref.dtype)
    v_ref[...] = kv[:, hd:].astype(v_ref.dtype)


def _mla_weights(w_in, w_q_up, w_kv_up):
    r = MLA_ROPE
    half = r // 2
    dq = MLA_NOPE + r
    nh = MLA_HEADS
    base = MLA_Q_RANK + MLA_KV_RANK
    kpe = w_in[:, base:base + r]
    zpad = lambda w, lo, hi: jnp.pad(w, ((0, 0), (lo, hi)))
    swap = lambda w: jnp.concatenate([-w[..., half:], w[..., :half]], axis=-1)
    kpe_a = zpad(kpe, MLA_NOPE, LANES - MLA_NOPE - r)
    kpe_b = zpad(swap(kpe), MLA_NOPE, LANES - MLA_NOPE - r)
    win = jnp.concatenate([w_in[:, :base], kpe_a, kpe_b], axis=1)
    wq = w_q_up.reshape(MLA_Q_RANK, nh, dq)
    pad3 = lambda w, lo, hi: jnp.pad(w, ((0, 0), (0, 0), (lo, hi)))
    wqa = pad3(wq, 0, LANES - dq).reshape(MLA_Q_RANK, nh * LANES)
    wqb = pad3(swap(wq[:, :, MLA_NOPE:]), MLA_NOPE, LANES - dq).reshape(MLA_Q_RANK, nh * LANES)
    wkv = w_kv_up.reshape(MLA_KV_RANK, nh, MLA_NOPE + MLA_V)
    wk = pad3(wkv[:, :, :MLA_NOPE], 0, LANES - MLA_NOPE).reshape(MLA_KV_RANK, nh * LANES)
    wv = wkv[:, :, MLA_NOPE:].reshape(MLA_KV_RANK, nh * MLA_V)
    return (win.astype(BF16), wqa.astype(BF16), wqb.astype(BF16),
            jnp.concatenate([wk, wv], axis=1).astype(BF16))


def _rope_tables(seq):
    half = MLA_ROPE // 2
    inv_freq = ROPE_THETA ** (-jnp.arange(half, dtype=F32) * 2.0 / MLA_ROPE)
    ang = jnp.arange(seq, dtype=F32)[:, None] * inv_freq[None, :]
    cos = jnp.cos(ang)
    sin = jnp.sin(ang)
    tail = LANES - MLA_NOPE - MLA_ROPE
    cos_p = jnp.concatenate([jnp.ones((seq, MLA_NOPE), F32), cos, cos, jnp.zeros((seq, tail), F32)], axis=1)
    sin_p = jnp.concatenate([jnp.zeros((seq, MLA_NOPE), F32), sin, sin, jnp.zeros((seq, tail), F32)], axis=1)
    return cos_p, sin_p


def mla_projections(h, w_in, q_norm, kv_norm, w_q_up, w_kv_up, seq, tm=256):
    n, d = h.shape
    win, wqa, wqb, wkv = _mla_weights(w_in, w_q_up, w_kv_up)
    cos_p, sin_p = _rope_tables(seq)
    hd = MLA_HEADS * LANES
    per_seq = seq // tm
    full = lambda a: pl.BlockSpec(a.shape, lambda i: (0,) * a.ndim)
    qn = q_norm.reshape(1, -1).astype(F32)
    kvn = kv_norm.reshape(1, -1).astype(F32)
    return pl.pallas_call(
        _mla_proj_kernel,
        grid=(n // tm,),
        in_specs=[pl.BlockSpec((tm, d), lambda i: (i, 0)), full(win), full(qn), full(kvn),
                  full(wqa), full(wqb), full(wkv),
                  pl.BlockSpec((tm, LANES), lambda i: (i % per_seq, 0)),
                  pl.BlockSpec((tm, LANES), lambda i: (i % per_seq, 0))],
        out_specs=[pl.BlockSpec((tm, hd), lambda i: (i, 0)),
                   pl.BlockSpec((tm, hd), lambda i: (i, 0)),
                   pl.BlockSpec((tm, MLA_HEADS * MLA_V), lambda i: (i, 0))],
        out_shape=[jax.ShapeDtypeStruct((n, hd), BF16),
                   jax.ShapeDtypeStruct((n, hd), BF16),
                   jax.ShapeDtypeStruct((n, MLA_HEADS * MLA_V), BF16)],
        compiler_params=_params("parallel"),
        name="mla_proj",
    )(h, win, qn, kvn, wqa, wqb, wkv, cos_p, sin_p)


def _mla_attn_kernel(q_ref, k_ref, v_ref, o_ref):
    v = v_ref[...]
    lane = lax.broadcasted_iota(jnp.int32, o_ref.shape, 1)
    halves = []
    for half in range(2):
        cols = slice(half * LANES, (half + 1) * LANES)
        s = _dot_nt(q_ref[:, cols], k_ref[:, cols])
        m = jnp.max(s, axis=-1, keepdims=True)
        e = jnp.exp(s - m)
        l = jnp.sum(e, axis=-1, keepdims=True)
        halves.append(_dot(e.astype(BF16), v) / l)
    o_ref[...] = jnp.where(lane < MLA_V, halves[0], halves[1]).astype(o_ref.dtype)


def mla_attention(q, k, v, batch, seq, tq=256):
    hd = MLA_HEADS * LANES
    q3 = q.reshape(batch, seq, hd)
    k3 = k.reshape(batch, seq, hd)
    v3 = v.reshape(batch, seq, MLA_HEADS * MLA_V)
    out = pl.pallas_call(
        _mla_attn_kernel,
        grid=(batch, MLA_HEADS // 2, seq // tq),
        in_specs=[pl.BlockSpec((None, tq, 2 * LANES), lambda b, p, i: (b, i, p)),
                  pl.BlockSpec((None, seq, 2 * LANES), lambda b, p, i: (b, 0, p)),
                  pl.BlockSpec((None, seq, LANES), lambda b, p, i: (b, 0, p))],
        out_specs=pl.BlockSpec((None, tq, LANES), lambda b, p, i: (b, i, p)),
        out_shape=jax.ShapeDtypeStruct((batch, seq, MLA_HEADS * MLA_V), BF16),
        compiler_params=_params("parallel", "parallel", "arbitrary"),
        name="mla_attn",
    )(q3, k3, v3)
    return out.reshape(batch * seq, MLA_HEADS * MLA_V)


HG_LEVELS = int(math.log2(HG_CHUNK))
HG_UNROLL = 1


def _hg_constants(reverse):
    c = HG_CHUNK
    t = np.arange(c)[:, None]
    u = np.arange(c)[None, :]
    if not reverse:
        incl = u <= t
        rest = u > t
    else:
        incl = u >= t
        rest = u < t
    mats = [incl, rest]
    masks = []
    roles = []
    for lvl in range(1, HG_LEVELS + 1):
        size = 1 << lvl
        start = (t // size) * size
        mid = start + size // 2
        upper = t >= mid
        if not reverse:
            q_side = (u >= mid) & (u <= t)
            k_side = (u > t) & (u <= mid - 1)
            is_query = upper
        else:
            q_side = (u >= t) & (u < mid)
            k_side = (u >= mid) & (u < t)
            is_query = ~upper
        mats.append(np.where(is_query, q_side, k_side))
        same = (t // size) == (u // size)
        key_row = (~is_query).T
        masks.append(same & is_query & np.broadcast_to(key_row, (c, c)))
        roles.append(np.broadcast_to(is_query, (c, LANES)))
    w = np.concatenate(mats, axis=0).astype(np.float32)
    return (jnp.asarray(w, BF16), jnp.asarray(np.stack(masks).astype(np.float32)),
            jnp.asarray(np.stack(roles).astype(np.float32)))


def _hg_chunk(q, zf, v_b, lb, w_ref, m_ref, r_ref, st, total_row):
    c = HG_CHUNK
    sg = jax.nn.sigmoid(zf)
    k = (1.0 - lb) * jax.nn.sigmoid(-zf)
    lf = jnp.log(lb + (1.0 - lb) * sg)
    hi = lf.astype(BF16)
    lo = (lf - hi.astype(F32)).astype(BF16)
    f = lf.shape[1]
    ex2 = _dot(w_ref[...], jnp.concatenate([hi, lo], axis=1))
    ex = ex2[:, :f] + ex2[:, f:]
    b_incl = ex[0:c]
    total = ex[total_row:total_row + 1]
    qd = q * jnp.exp(b_incl)
    kd = k * jnp.exp(ex[c:2 * c])
    row = lax.broadcasted_iota(jnp.int32, (c, c), 0)
    col = lax.broadcasted_iota(jnp.int32, (c, c), 1)
    a = jnp.where(row == col, jnp.sum(q * k, axis=-1, keepdims=True), 0.0)
    xs = []
    for lvl in range(HG_LEVELS):
        x = jnp.where(r_ref[lvl] > 0.5, q, k) * jnp.exp(ex[(2 + lvl) * c:(3 + lvl) * c])
        xs.append(x.astype(BF16))
    for lvl in range(0, HG_LEVELS, 2):
        pair = xs[lvl:lvl + 2]
        xb = jnp.concatenate(pair, axis=0)
        g = _dot_nt(xb, xb)
        for j in range(len(pair)):
            a = a + m_ref[lvl + j] * g[j * c:(j + 1) * c, j * c:(j + 1) * c]
    o = _dot(a.astype(BF16), v_b) + _dot_nt(qd.astype(BF16), st.astype(BF16))
    return o, st * jnp.exp(total) + _dot_tn(v_b, kd.astype(BF16))


def _hg_kernel(zq_ref, zff_ref, zfb_ref, zi_ref, zg_ref, lb_ref, g_ref,
               wf_ref, mf_ref, rf_ref, wb_ref, mb_ref, rb_ref,
               o_ref, accf_ref, accb_ref):
    c = HG_CHUNK
    seq = zq_ref.shape[0]
    n = seq // c
    lb_f = lb_ref[0:1, :]
    lb_b = lb_ref[1:2, :]

    def gated(ref, rows):
        z = ref[rows, :]
        return z * jax.nn.sigmoid(z)

    def body(i, carry):
        st_f, st_b = carry
        rf = pl.ds(pl.multiple_of(i * c, c), c)
        rb = pl.ds(pl.multiple_of((n - 1 - i) * c, c), c)
        of, st_f = _hg_chunk(gated(zq_ref, rf), zff_ref[rf, :], zi_ref[rf, :].astype(BF16), lb_f,
                             wf_ref, mf_ref, rf_ref, st_f, c - 1)
        accf_ref[rf, :] = of
        ob, st_b = _hg_chunk(gated(zq_ref, rb), zfb_ref[rb, :], zi_ref[rb, :].astype(BF16), lb_b,
                             wb_ref, mb_ref, rb_ref, st_b, 0)
        accb_ref[rb, :] = ob
        return st_f, st_b

    zero = jnp.zeros((HG_F, HG_F), F32)
    lax.fori_loop(0, n, body, (zero, zero), unroll=HG_UNROLL)

    def finish(i, carry):
        rows = pl.ds(pl.multiple_of(i * c, c), c)
        o = accf_ref[rows, :] + accb_ref[rows, :]
        o = o * lax.rsqrt(jnp.mean(o * o, axis=-1, keepdims=True) + NORM_EPS)
        o_ref[rows, :] = (o * g_ref[...] * gated(zg_ref, rows)).astype(o_ref.dtype)
        return carry

    lax.fori_loop(0, n, finish, 0)


def hgrn2_scan(z, lb, norm_g, batch, seq):
    d = D_MODEL
    z3 = z.reshape(batch, seq, 5 * d)
    wf, mf, rf = _hg_constants(False)
    wb, mb, rb = _hg_constants(True)
    zspec = lambda j: pl.BlockSpec((None, seq, HG_F), lambda b, h: (b, 0, j * HG_HEADS + h))
    full = lambda a: pl.BlockSpec(a.shape, lambda b, h: (0,) * a.ndim)
    out = pl.pallas_call(
        _hg_kernel,
        grid=(batch, HG_HEADS),
        in_specs=[zspec(0), zspec(1), zspec(2), zspec(3), zspec(4),
                  pl.BlockSpec((2, HG_F), lambda b, h: (0, h)),
                  pl.BlockSpec((1, HG_F), lambda b, h: (0, h)),
                  full(wf), full(mf), full(rf), full(wb), full(mb), full(rb)],
        out_specs=pl.BlockSpec((None, seq, HG_F), lambda b, h: (b, 0, h)),
        out_shape=jax.ShapeDtypeStruct((batch, seq, d), BF16),
        scratch_shapes=[pltpu.VMEM((seq, HG_F), F32), pltpu.VMEM((seq, HG_F), F32)],
        compiler_params=_params("parallel", "parallel"),
        name="hgrn2_scan",
    )(z3, z3, z3, z3, z3, lb.astype(F32), norm_g.reshape(1, d).astype(F32),
      wf, mf, rf, wb, mb, rb)
    return out.reshape(batch * seq, d)


def _merge_sort_network(n):
    pairs = []
    p = 1
    while p < n:
        k = p
        while k >= 1:
            for j in range(k % p, n - k, 2 * k):
                for i in range(min(k, n - j - k)):
                    if (i + j) // (2 * p) == (i + j + k) // (2 * p):
                        pairs.append((i + j, i + j + k))
            k //= 2
        p *= 2
    return pairs


def _top_values(s, count):
    rows = s.shape[0]
    n = rows // F32_SUBLANES
    size = 1 << (n - 1).bit_length()
    neg = jnp.full((F32_SUBLANES, s.shape[1]), -jnp.inf, F32)
    lists = [s[i * F32_SUBLANES:(i + 1) * F32_SUBLANES] for i in range(n)] + [neg] * (size - n)
    for i, j in _merge_sort_network(size):
        hi = jnp.maximum(lists[i], lists[j])
        lists[j] = jnp.minimum(lists[i], lists[j])
        lists[i] = hi
    lists = lists[:min(size, count)]
    vals = []
    for it in range(count):
        head = lists[0]
        m = jnp.max(head, axis=0, keepdims=True)
        vals.append(m)
        hit = head >= m
        depth = min(len(lists), count - it - 1)
        for k in range(depth):
            nxt = lists[k + 1] if k + 1 < len(lists) else neg
            lists[k] = jnp.where(hit, nxt, lists[k])
        lists = lists[:max(depth, 1)]
    return jnp.concatenate(vals, axis=0)


def _peer_route_kernel(ht_ref, wq_ref, keys_ref, cnt_ref, a0_ref, r1_ref, b1_ref):
    kk = PEER_TOPK
    ht = ht_ref[...]
    for h in range(PEER_HEADS):
        scores = []
        for c in range(2):
            g = 2 * h + c
            qt = _dot(wq_ref[g * LANES:(g + 1) * LANES, :], ht)
            scores.append(_dot(keys_ref[c], qt.astype(BF16)))
        s0, s1 = scores
        sv0 = _top_values(s0, kk)
        sv1 = _top_values(s1, kk)
        r1 = jnp.full(s1.shape, float(kk), F32)
        for b in range(kk):
            r1 = jnp.where(s1 == sv1[b:b + 1], float(b), r1)
        cands = [sv0[a:a + 1] + sv1[0:kk // (a + 1)] for a in range(kk)]
        n_cand = sum(kk // (a + 1) for a in range(kk))
        pad = (-n_cand) % 8
        if pad:
            cands.append(jnp.full((pad, ht.shape[1]), -jnp.inf, F32))
        tau = _top_values(jnp.concatenate(cands, axis=0), kk)[kk - 1:kk]
        e0 = jnp.exp(sv0 - sv0[0:1])
        e1 = jnp.exp(sv1 - sv1[0:1])
        z = jnp.zeros_like(tau)
        for a in range(kk):
            nb = kk // (a + 1)
            sel = (sv0[a:a + 1] + sv1[0:nb]) >= tau
            z = z + e0[a:a + 1] * jnp.sum(jnp.where(sel, e1[0:nb], 0.0), axis=0, keepdims=True)
        cnt = jnp.zeros(s0.shape, F32)
        for a in range(kk):
            pair_ok = (sv0[a:a + 1] + sv1) >= tau
            n_a = jnp.sum(jnp.where(pair_ok, 1.0, 0.0), axis=0, keepdims=True)
            cnt = jnp.where(s0 == sv0[a:a + 1], n_a, cnt)
        cnt_ref[h] = cnt
        a0_ref[h] = jnp.exp(s0 - sv0[0:1])
        r1_ref[h] = r1.astype(BF16)
        b1_ref[h] = (jnp.exp(s1 - sv1[0:1]) * (0.5 / z)).astype(BF16)


def peer_route(ht, wq_t, keys, tt=256):
    d, n = ht.shape
    shape = (PEER_HEADS, PEER_NKEYS, n)
    ospec = pl.BlockSpec((PEER_HEADS, PEER_NKEYS, tt), lambda i: (0, 0, i))
    return pl.pallas_call(
        _peer_route_kernel,
        grid=(n // tt,),
        in_specs=[pl.BlockSpec((d, tt), lambda i: (0, i)),
                  pl.BlockSpec(wq_t.shape, lambda i: (0, 0)),
                  pl.BlockSpec(keys.shape, lambda i: (0, 0, 0))],
        out_specs=[ospec, ospec, ospec, ospec],
        out_shape=[jax.ShapeDtypeStruct(shape, F32), jax.ShapeDtypeStruct(shape, F32),
                   jax.ShapeDtypeStruct(shape, BF16), jax.ShapeDtypeStruct(shape, BF16)],
        compiler_params=_params("parallel"),
        name="peer_route",
    )(ht, wq_t, keys)


def _peer_gate_chunk(hid, i0, cnt_ref, a0_ref, r1_ref, b1_ref):
    tt = hid.shape[1]
    zero = jnp.zeros((PEER_NKEYS, LANES), BF16)

    def row_tile(ref, h, ii, cols):
        row = jnp.broadcast_to(ref[h, ii:ii + 1, cols], (BF16_SUBLANES, LANES)).astype(BF16)
        return pltpu.repeat(row, PEER_NKEYS // BF16_SUBLANES, axis=0)

    out_rows = []
    for k in range(PEER_CHUNK_ROWS):
        tiles = []
        for tc in range(tt // LANES):
            cols = slice(tc * LANES, (tc + 1) * LANES)
            gate = zero
            for h in range(PEER_HEADS):
                cnt = row_tile(cnt_ref, h, i0 + k, cols)
                a0 = row_tile(a0_ref, h, i0 + k, cols)
                gate = gate + jnp.where(r1_ref[h, :, cols] < cnt, a0 * b1_ref[h, :, cols], zero)
            x = hid[k * PEER_NKEYS:(k + 1) * PEER_NKEYS, cols]
            act = x.astype(BF16) * (1.0 + lax.erf(x * RSQRT2)).astype(BF16)
            tiles.append(gate * act)
        out_rows.append(jnp.concatenate(tiles, axis=1))
    return jnp.concatenate(out_rows, axis=0)


def _pack_rows_kernel(x_ref, o_ref):
    o_ref[...] = pltpu.bitcast(x_ref[...].astype(BF16), jnp.uint32)


def _pack_rows_t_kernel(x_ref, o_ref):
    o_ref[...] = pltpu.bitcast(x_ref[...].T.astype(BF16), jnp.uint32)


def pack_expert_tables(u, v):
    layers, n_exp, d = u.shape
    eb = PEER_EXPERT_BLOCK
    nb = n_exp // eb
    u_packed = pl.pallas_call(
        _pack_rows_kernel,
        grid=(layers, nb),
        in_specs=[pl.BlockSpec((None, eb, d), lambda l, e: (l, e, 0))],
        out_specs=pl.BlockSpec((None, eb // 2, d), lambda l, e: (l, e, 0)),
        out_shape=jax.ShapeDtypeStruct((layers, n_exp // 2, d), jnp.uint32),
        compiler_params=_params("parallel", "parallel"),
        name="pack_u",
    )(u)
    v_packed = pl.pallas_call(
        _pack_rows_t_kernel,
        grid=(layers, nb),
        in_specs=[pl.BlockSpec((None, eb, d), lambda l, e: (l, e, 0))],
        out_specs=pl.BlockSpec((None, None, d // 2, eb), lambda l, e: (l, e, 0, 0)),
        out_shape=jax.ShapeDtypeStruct((layers, nb, d // 2, eb), jnp.uint32),
        compiler_params=_params("parallel", "parallel"),
        name="pack_vt",
    )(v)
    return u_packed, v_packed


def _unpack_rows(words):
    return pltpu.bitcast(words, BF16)


def _ordered_after(x, dep):
    z = pltpu.bitcast(dep[:BF16_SUBLANES, :LANES], jnp.uint32)
    z = (z >> 16) >> 16
    zero = pltpu.bitcast(z, BF16)
    zero = pltpu.repeat(pltpu.repeat(zero, x.shape[0] // BF16_SUBLANES, axis=0), x.shape[1] // LANES, axis=1)
    return x + zero


def _peer_ffn_kernel(ht_ref, u0_ref, ub_ref, un_ref, vta_ref, vtb_ref, cnta_ref, a0a_ref, cntb_ref, a0b_ref,
                     r1_in_ref, b1_in_ref, res_ref, g_ref, b_ref,
                     o_ref, acc_ref, hida_ref, hidb_ref, r1_ref, b1_ref):
    s = pl.program_id(1)
    half = ht_ref.shape[1] // 2

    @pl.when(s == 0)
    def _():
        acc_ref[...] = jnp.zeros_like(acc_ref)
        r1_ref[...] = r1_in_ref[...]
        b1_ref[...] = b1_in_ref[...]
        hida_ref[...] = _dot(_unpack_rows(u0_ref[...]), ht_ref[...])

    chunk = PEER_CHUNK_ROWS * PEER_NKEYS
    n_chunks = PEER_EXPERT_BLOCK // chunk

    def run_block(acc, hid_ref, vt_ref, cnt_ref, a0_ref, next_u_ref, next_hid_ref):
        for c in range(n_chunks):
            span = slice(c * chunk, (c + 1) * chunk)
            w = _peer_gate_chunk(hid_ref[span, :], c * PEER_CHUNK_ROWS, cnt_ref, a0_ref, r1_ref, b1_ref)
            acc = acc + _dot(_unpack_rows(vt_ref[:, span]), w)
            if c % (n_chunks // 2) == n_chunks // 2 - 1:
                hh = c // (n_chunks // 2)
                tcols = slice(hh * half, (hh + 1) * half)
                next_hid_ref[:, tcols] = _dot(_unpack_rows(next_u_ref[...]),
                                              _ordered_after(ht_ref[:, tcols], w))
        return acc

    acc = run_block(acc_ref[...], hida_ref, vta_ref, cnta_ref, a0a_ref, ub_ref, hidb_ref)
    acc_ref[...] = run_block(acc, hidb_ref, vtb_ref, cntb_ref, a0b_ref, un_ref, hida_ref)

    @pl.when(s == pl.num_programs(1) - 1)
    def _():
        y = acc_ref[...].T
        o_ref[...] = _layer_norm_rows(DN_ALPHA * res_ref[...] + y, g_ref[...], b_ref[...])


def peer_ffn(h, ht, route, u, v_blocks, layer, g, b, tt=512):
    n, d = h.shape
    eb = PEER_EXPERT_BLOCK
    last = 2 * u.shape[1] // eb - 1
    rspec = pl.BlockSpec((PEER_HEADS, PEER_NKEYS, tt), lambda i, s: (0, 0, i))
    rowspec = lambda off: pl.BlockSpec((PEER_HEADS, eb // PEER_NKEYS, tt), lambda i, s: (0, 2 * s + off, i))
    cnt, a0, r1, b1 = route
    return pl.pallas_call(
        _peer_ffn_kernel,
        grid=(n // tt, (last + 1) // 2),
        in_specs=[pl.BlockSpec((d, tt), lambda i, s: (0, i)),
                  pl.BlockSpec((None, eb // 2, d), lambda i, s: (layer, 0, 0)),
                  pl.BlockSpec((None, eb // 2, d), lambda i, s: (layer, 2 * s + 1, 0)),
                  pl.BlockSpec((None, eb // 2, d), lambda i, s: (layer, jnp.minimum(2 * s + 2, last), 0)),
                  pl.BlockSpec((None, None, d // 2, eb), lambda i, s: (layer, 2 * s, 0, 0)),
                  pl.BlockSpec((None, None, d // 2, eb), lambda i, s: (layer, 2 * s + 1, 0, 0)),
                  rowspec(0), rowspec(0), rowspec(1), rowspec(1),
                  rspec, rspec,
                  pl.BlockSpec((tt, d), lambda i, s: (i, 0)),
                  pl.BlockSpec((1, d), lambda i, s: (0, 0)),
                  pl.BlockSpec((1, d), lambda i, s: (0, 0))],
        out_specs=pl.BlockSpec((tt, d), lambda i, s: (i, 0)),
        out_shape=jax.ShapeDtypeStruct((n, d), F32),
        scratch_shapes=[pltpu.VMEM((d, tt), F32), pltpu.VMEM((eb, tt), F32), pltpu.VMEM((eb, tt), F32),
                        pltpu.VMEM((PEER_HEADS, PEER_NKEYS, tt), BF16),
                        pltpu.VMEM((PEER_HEADS, PEER_NKEYS, tt), BF16)],
        compiler_params=_params("parallel", "arbitrary"),
        name="peer_ffn",
    )(ht, u, u, u, v_blocks, v_blocks, cnt, a0, cnt, a0, r1, b1, h, g.reshape(1, d), b.reshape(1, d))


def kernel(x, na_w_in, na_rel_bias, na_w_out, mla_w_in, mla_q_norm, mla_kv_norm, mla_w_q_up, mla_w_kv_up, mla_w_out, hg_w_in, hg_lower_bound, hg_norm, hg_w_out, peer_w_q, peer_sub_keys, peer_u, peer_v, ln_mix_g, ln_mix_b, ln_ffn_g, ln_ffn_b):
    batch, seq, d = x.shape
    lb_w = jax.nn.softmax(hg_lower_bound.astype(F32), axis=0)
    lb_all = jnp.cumsum(lb_w, axis=0) - lb_w[0:1]
    h = x.reshape(batch * seq, d)
    u_all, v_all = pack_expert_tables(peer_u, peer_v)
    for layer in range(DEPTH):
        kind = layer % N_MIXERS
        j = layer // N_MIXERS
        if kind == 0:
            qk, vt = na_project(h, na_w_in[j], batch)
            mix_in = neighborhood_attention(qk, vt, na_rel_bias[j])
            w_out = na_w_out[j]
        elif kind == 1:
            q, k, v = mla_projections(h, mla_w_in[j], mla_q_norm[j], mla_kv_norm[j],
                                      mla_w_q_up[j], mla_w_kv_up[j], seq)
            mix_in = mla_attention(q, k, v, batch, seq)
            w_out = mla_w_out[j]
        else:
            z = matmul(h, hg_w_in[j].astype(BF16), F32)
            mix_in = hgrn2_scan(z, lb_all[layer], hg_norm[j], batch, seq)
            w_out = hg_w_out[j]
        h, ht = matmul_res_ln(mix_in, w_out.astype(BF16), h, ln_mix_g[layer], ln_mix_b[layer])
        route = peer_route(ht, peer_w_q[layer].T.astype(BF16), peer_sub_keys[layer].astype(BF16))
        h = peer_ffn(h, ht, route, u_all, v_all, layer, ln_ffn_g[layer], ln_ffn_b[layer])
    return h.reshape(batch, seq, d)
```

```python
import functools
import math

import jax
import jax.numpy as jnp
import numpy as np
from jax import lax
from jax.experimental import pallas as pl
from jax.experimental.pallas import tpu as pltpu

F32 = jnp.float32
BF16 = jnp.bfloat16

D_MODEL = 1024
DEPTH = 4
GRID_W = 64
N_MIXERS = 3

NA_HEADS = 16
NA_HEAD_DIM = 64
NA_KH = 8
NA_KW = 16

MLA_HEADS = 16
MLA_NOPE = 64
MLA_ROPE = 32
MLA_V = 64
MLA_Q_RANK = 256
MLA_KV_RANK = 256
ROPE_THETA = 10000.0

HG_HEADS = 8
HG_F = 128
HG_CHUNK = 128

PEER_HEADS = 8
PEER_NKEYS = 128
PEER_TOPK = 16
PEER_EXPERT_BLOCK = 1024
PEER_CHUNK_ROWS = 2
NORM_EPS = 1e-5
DN_ALPHA = (2.0 * DEPTH) ** 0.25

LANES = 128
F32_SUBLANES = 8
BF16_SUBLANES = 16
VMEM_LIMIT = 48 * 1024 * 1024
MASK_NEG = -1e30
RSQRT2 = 0.7071067811865476


def _params(*sem):
    return pltpu.CompilerParams(dimension_semantics=sem, vmem_limit_bytes=VMEM_LIMIT)


def _layer_norm_rows(z, g, b):
    mu = jnp.mean(z, axis=-1, keepdims=True)
    zc = z - mu
    var = jnp.mean(zc * zc, axis=-1, keepdims=True)
    return zc * lax.rsqrt(var + NORM_EPS) * g + b


def _dot(a, b):
    return jnp.dot(a, b, preferred_element_type=F32)


def _dot_nt(a, b):
    return lax.dot_general(a, b, (((1,), (1,)), ((), ())), preferred_element_type=F32)


def _dot_tn(a, b):
    return lax.dot_general(a, b, (((0,), (0,)), ((), ())), preferred_element_type=F32)


def _mm_kernel(a_ref, w_ref, o_ref):
    o_ref[...] = _dot(a_ref[...].astype(BF16), w_ref[...]).astype(o_ref.dtype)


def matmul(a, w, out_dtype, tm=512, tn=1024):
    m, k = a.shape
    n = w.shape[1]
    tn = min(tn, n)
    return pl.pallas_call(
        _mm_kernel,
        grid=(m // tm, n // tn),
        in_specs=[pl.BlockSpec((tm, k), lambda i, j: (i, 0)),
                  pl.BlockSpec((k, tn), lambda i, j: (0, j))],
        out_specs=pl.BlockSpec((tm, tn), lambda i, j: (i, j)),
        out_shape=jax.ShapeDtypeStruct((m, n), out_dtype),
        compiler_params=_params("parallel", "parallel"),
        name="proj",
    )(a, w)


def _mm_res_ln_kernel(a_ref, w_ref, res_ref, g_ref, b_ref, o_ref, ot_ref):
    y = _dot(a_ref[...], w_ref[...])
    out = _layer_norm_rows(DN_ALPHA * res_ref[...] + y, g_ref[...], b_ref[...])
    o_ref[...] = out
    ot_ref[...] = out.T.astype(BF16)


def matmul_res_ln(a, w, res, g, b, tm=512):
    m, k = a.shape
    n = w.shape[1]
    return pl.pallas_call(
        _mm_res_ln_kernel,
        grid=(m // tm,),
        in_specs=[pl.BlockSpec((tm, k), lambda i: (i, 0)),
                  pl.BlockSpec((k, n), lambda i: (0, 0)),
                  pl.BlockSpec((tm, n), lambda i: (i, 0)),
                  pl.BlockSpec((1, n), lambda i: (0, 0)),
                  pl.BlockSpec((1, n), lambda i: (0, 0))],
        out_specs=[pl.BlockSpec((tm, n), lambda i: (i, 0)),
                   pl.BlockSpec((n, tm), lambda i: (0, i))],
        out_shape=[jax.ShapeDtypeStruct((m, n), F32),
                   jax.ShapeDtypeStruct((n, m), BF16)],
        compiler_params=_params("parallel"),
        name="out_proj_ln",
    )(a, w, res, g.reshape(1, n), b.reshape(1, n))


def _na_row_start(r, rows):
    return jnp.clip(r - NA_KH // 2, 0, rows - NA_KH)


def _na_proj_kernel(h_ref, wqk_ref, wvt_ref, qk_ref, vt_ref):
    hb = h_ref[...].astype(BF16)
    qk_ref[...] = _dot(hb, wqk_ref[...]).astype(qk_ref.dtype).reshape(qk_ref.shape)
    vt_ref[...] = _dot_nt(wvt_ref[...], hb).astype(vt_ref.dtype)


def na_project(h, w_in, batch, tm=512):
    n, d = h.shape
    rows = n // (batch * GRID_W)
    tile_rows = tm // GRID_W
    tiles = rows // tile_rows
    wqk = w_in[:, :2 * d].astype(BF16)
    wvt = w_in[:, 2 * d:].T.astype(BF16)
    return pl.pallas_call(
        _na_proj_kernel,
        grid=(n // tm,),
        in_specs=[pl.BlockSpec((tm, d), lambda i: (i, 0)),
                  pl.BlockSpec((d, 2 * d), lambda i: (0, 0)),
                  pl.BlockSpec((d, d), lambda i: (0, 0))],
        out_specs=[pl.BlockSpec((None, tile_rows, GRID_W, 2 * d), lambda i: (i // tiles, i % tiles, 0, 0)),
                   pl.BlockSpec((d, tm), lambda i: (0, i))],
        out_shape=[jax.ShapeDtypeStruct((batch, rows, GRID_W, 2 * d), BF16),
                   jax.ShapeDtypeStruct((d, n), BF16)],
        compiler_params=_params("parallel"),
        name="na_proj",
    )(h, wqk, wvt)


NA_QROWS = 2
NA_KROWS = NA_KH + NA_QROWS
NA_EDGE = NA_KH // 4


def _na_key_start(i, rows):
    return jnp.clip(i - NA_KH // 4, 0, (rows - NA_KROWS) // 2)


def _na_kernel(q_ref, k_ref, vt_ref, bias_ref, o_ref):
    n_keys = NA_KROWS * GRID_W
    n_q = NA_QROWS * GRID_W
    lane = lax.broadcasted_iota(jnp.int32, (n_q, LANES), 1)
    low = lane < NA_HEAD_DIM
    row_low = lax.broadcasted_iota(jnp.int32, (LANES, n_q), 0) < NA_HEAD_DIM
    for p in range(NA_HEADS // 2):
        cols = slice(p * LANES, (p + 1) * LANES)
        qp = q_ref[0, :, :, cols].reshape(n_q, LANES)
        zq = jnp.zeros_like(qp)
        kp = k_ref[0, :, :, cols].reshape(n_keys, LANES)
        vtp = vt_ref[cols, :]
        outs = []
        for half in range(2):
            qh = jnp.where(low, qp, zq) if half == 0 else jnp.where(low, zq, qp)
            s = _dot_nt(kp, qh) * (NA_HEAD_DIM ** -0.5) + bias_ref[2 * p + half]
            m = jnp.max(s, axis=0, keepdims=True)
            e = jnp.exp(s - m)
            l = jnp.sum(e, axis=0, keepdims=True)
            outs.append(_dot(vtp, e.astype(BF16)) / l)
        o2 = jnp.where(row_low, outs[0], outs[1]).T.astype(o_ref.dtype)
        o_ref[:, cols] = o2


def _na_bias_table(rel_bias, rows):
    cols = np.arange(GRID_W)
    c0 = np.clip(cols - NA_KW // 2, 0, GRID_W - NA_KW)
    kc = np.arange(GRID_W)
    inside = (kc[None, :] >= c0[:, None]) & (kc[None, :] < c0[:, None] + NA_KW)
    dc = np.clip(kc[None, :] - cols[:, None] + (NA_KW - 1), 0, 2 * NA_KW - 2)
    b2 = jnp.where(inside[None, None], rel_bias[:, :, dc].astype(F32), MASK_NEG)
    steps = rows // NA_QROWS
    variant_steps = list(range(NA_EDGE)) + [NA_EDGE] + list(range(steps - NA_EDGE, steps))
    dr = np.zeros((len(variant_steps), NA_QROWS, NA_KROWS), np.int32)
    ok = np.zeros(dr.shape, bool)
    for v, i in enumerate(variant_steps):
        key0 = 2 * int(np.clip(i - NA_KH // 4, 0, (rows - NA_KROWS) // 2))
        for a in range(NA_QROWS):
            r = NA_QROWS * i + a
            r0 = int(np.clip(r - NA_KH // 2, 0, rows - NA_KH))
            key_rows = key0 + np.arange(NA_KROWS)
            ok[v, a] = (key_rows >= r0) & (key_rows < r0 + NA_KH)
            dr[v, a] = np.clip(key_rows - r + NA_KH - 1, 0, 2 * NA_KH - 2)
    b2t = jnp.swapaxes(b2, -1, -2)
    per_row = []
    for a in range(NA_QROWS):
        t = b2t[:, dr[:, a]]
        per_row.append(jnp.where(ok[None, :, a, :, None, None], t, MASK_NEG))
    t = jnp.concatenate(per_row, axis=-1)
    return t.reshape(NA_HEADS, len(variant_steps), NA_KROWS * GRID_W, NA_QROWS * GRID_W)


def neighborhood_attention(qk4, vt, rel_bias):
    d = D_MODEL
    batch, rows = qk4.shape[:2]
    steps = rows // NA_QROWS
    n_keys = NA_KROWS * GRID_W
    bias = _na_bias_table(rel_bias, rows)
    el = pl.Element
    key0 = lambda i: _na_key_start(i, rows)

    def variant(i):
        return jnp.where(i < NA_EDGE, i, jnp.where(i >= steps - NA_EDGE, i - (steps - 2 * NA_EDGE - 1), NA_EDGE))

    return pl.pallas_call(
        _na_kernel,
        grid=(batch, steps),
        in_specs=[pl.BlockSpec((el(1), el(NA_QROWS), el(GRID_W), el(d)),
                               lambda b, i: (b, NA_QROWS * i, 0, 0)),
                  pl.BlockSpec((el(1), el(NA_KROWS), el(GRID_W), el(d)),
                               lambda b, i: (b, 2 * key0(i), 0, d)),
                  pl.BlockSpec((el(d), el(n_keys)),
                               lambda b, i: (0, (b * (rows // 2) + key0(i)) * (2 * GRID_W))),
                  pl.BlockSpec((NA_HEADS, None, n_keys, NA_QROWS * GRID_W),
                               lambda b, i: (0, variant(i), 0, 0))],
        out_specs=pl.BlockSpec((NA_QROWS * GRID_W, d), lambda b, i: (b * steps + i, 0)),
        out_shape=jax.ShapeDtypeStruct((batch * rows * GRID_W, d), BF16),
        compiler_params=_params("parallel", "arbitrary"),
        name="na_attn",
    )(qk4, qk4, vt, bias)


def _rms_rows(x, g):
    return x * lax.rsqrt(jnp.mean(x * x, axis=-1, keepdims=True) + NORM_EPS) * g


def _mla_proj_kernel(x_ref, win_ref, qn_ref, kvn_ref, wqa_ref, wqb_ref, wkv_ref, cos_ref, sin_ref,
                     q_ref, k_ref, v_ref):
    hd = MLA_HEADS * LANES
    hin = _dot(x_ref[...].astype(BF16), win_ref[...])
    cq = _rms_rows(hin[:, :MLA_Q_RANK], qn_ref[...]).astype(BF16)
    ckv = _rms_rows(hin[:, MLA_Q_RANK:MLA_Q_RANK + MLA_KV_RANK], kvn_ref[...]).astype(BF16)
    cos = cos_ref[...]
    sin = sin_ref[...]
    cos_t = jnp.tile(cos, (1, MLA_HEADS))
    sin_t = jnp.tile(sin, (1, MLA_HEADS))
    q = _dot(cq, wqa_ref[...]) * cos_t + _dot(cq, wqb_ref[...]) * sin_t
    q_ref[...] = (q * ((MLA_NOPE + MLA_ROPE) ** -0.5)).astype(q_ref.dtype)
    kv = _dot(ckv, wkv_ref[...])
    base = MLA_Q_RANK + MLA_KV_RANK
    kpe = hin[:, base:base + LANES] * cos + hin[:, base + LANES:base + 2 * LANES] * sin
    k_ref[...] = (kv[:, :hd] + jnp.tile(kpe, (1, MLA_HEADS))).astype(k_ref.dtype)
    v_ref[...] = kv[:, hd:].astype(v_ref.dtype)


def _mla_weights(w_in, w_q_up, w_kv_up):
    r = MLA_ROPE
    half = r // 2
    dq = MLA_NOPE + r
    nh = MLA_HEADS
    base = MLA_Q_RANK + MLA_KV_RANK
    kpe = w_in[:, base:base + r]
    zpad = lambda w, lo, hi: jnp.pad(w, ((0, 0), (lo, hi)))
    swap = lambda w: jnp.concatenate([-w[..., half:], w[..., :half]], axis=-1)
    kpe_a = zpad(kpe, MLA_NOPE, LANES - MLA_NOPE - r)
    kpe_b = zpad(swap(kpe), MLA_NOPE, LANES - MLA_NOPE - r)
    win = jnp.concatenate([w_in[:, :base], kpe_a, kpe_b], axis=1)
    wq = w_q_up.reshape(MLA_Q_RANK, nh, dq)
    pad3 = lambda w, lo, hi: jnp.pad(w, ((0, 0), (0, 0), (lo, hi)))
    wqa = pad3(wq, 0, LANES - dq).reshape(MLA_Q_RANK, nh * LANES)
    wqb = pad3(swap(wq[:, :, MLA_NOPE:]), MLA_NOPE, LANES - dq).reshape(MLA_Q_RANK, nh * LANES)
    wkv = w_kv_up.reshape(MLA_KV_RANK, nh, MLA_NOPE + MLA_V)
    wk = pad3(wkv[:, :, :MLA_NOPE], 0, LANES - MLA_NOPE).reshape(MLA_KV_RANK, nh * LANES)
    wv = wkv[:, :, MLA_NOPE:].reshape(MLA_KV_RANK, nh * MLA_V)
    return (win.astype(BF16), wqa.astype(BF16), wqb.astype(BF16),
            jnp.concatenate([wk, wv], axis=1).astype(BF16))


def _rope_tables(seq):
    half = MLA_ROPE // 2
    inv_freq = ROPE_THETA ** (-jnp.arange(half, dtype=F32) * 2.0 / MLA_ROPE)
    ang = jnp.arange(seq, dtype=F32)[:, None] * inv_freq[None, :]
    cos = jnp.cos(ang)
    sin = jnp.sin(ang)
    tail = LANES - MLA_NOPE - MLA_ROPE
    cos_p = jnp.concatenate([jnp.ones((seq, MLA_NOPE), F32), cos, cos, jnp.zeros((seq, tail), F32)], axis=1)
    sin_p = jnp.concatenate([jnp.zeros((seq, MLA_NOPE), F32), sin, sin, jnp.zeros((seq, tail), F32)], axis=1)
    return cos_p, sin_p


def mla_projections(h, w_in, q_norm, kv_norm, w_q_up, w_kv_up, seq, tm=256):
    n, d = h.shape
    win, wqa, wqb, wkv = _mla_weights(w_in, w_q_up, w_kv_up)
    cos_p, sin_p = _rope_tables(seq)
    hd = MLA_HEADS * LANES
    per_seq = seq // tm
    full = lambda a: pl.BlockSpec(a.shape, lambda i: (0,) * a.ndim)
    qn = q_norm.reshape(1, -1).astype(F32)
    kvn = kv_norm.reshape(1, -1).astype(F32)
    return pl.pallas_call(
        _mla_proj_kernel,
        grid=(n // tm,),
        in_specs=[pl.BlockSpec((tm, d), lambda i: (i, 0)), full(win), full(qn), full(kvn),
                  full(wqa), full(wqb), full(wkv),
                  pl.BlockSpec((tm, LANES), lambda i: (i % per_seq, 0)),
                  pl.BlockSpec((tm, LANES), lambda i: (i % per_seq, 0))],
        out_specs=[pl.BlockSpec((tm, hd), lambda i: (i, 0)),
                   pl.BlockSpec((tm, hd), lambda i: (i, 0)),
                   pl.BlockSpec((tm, MLA_HEADS * MLA_V), lambda i: (i, 0))],
        out_shape=[jax.ShapeDtypeStruct((n, hd), BF16),
                   jax.ShapeDtypeStruct((n, hd), BF16),
                   jax.ShapeDtypeStruct((n, MLA_HEADS * MLA_V), BF16)],
        compiler_params=_params("parallel"),
        name="mla_proj",
    )(h, win, qn, kvn, wqa, wqb, wkv, cos_p, sin_p)


def _mla_attn_kernel(q_ref, k_ref, v_ref, o_ref):
    v = v_ref[...]
    lane = lax.broadcasted_iota(jnp.int32, o_ref.shape, 1)
    halves = []
    for half in range(2):
        cols = slice(half * LANES, (half + 1) * LANES)
        s = _dot_nt(q_ref[:, cols], k_ref[:, cols])
        m = jnp.max(s, axis=-1, keepdims=True)
        e = jnp.exp(s - m)
        l = jnp.sum(e, axis=-1, keepdims=True)
        halves.append(_dot(e.astype(BF16), v) / l)
    o_ref[...] = jnp.where(lane < MLA_V, halves[0], halves[1]).astype(o_ref.dtype)


def mla_attention(q, k, v, batch, seq, tq=256):
    hd = MLA_HEADS * LANES
    q3 = q.reshape(batch, seq, hd)
    k3 = k.reshape(batch, seq, hd)
    v3 = v.reshape(batch, seq, MLA_HEADS * MLA_V)
    out = pl.pallas_call(
        _mla_attn_kernel,
        grid=(batch, MLA_HEADS // 2, seq // tq),
        in_specs=[pl.BlockSpec((None, tq, 2 * LANES), lambda b, p, i: (b, i, p)),
                  pl.BlockSpec((None, seq, 2 * LANES), lambda b, p, i: (b, 0, p)),
                  pl.BlockSpec((None, seq, LANES), lambda b, p, i: (b, 0, p))],
        out_specs=pl.BlockSpec((None, tq, LANES), lambda b, p, i: (b, i, p)),
        out_shape=jax.ShapeDtypeStruct((batch, seq, MLA_HEADS * MLA_V), BF16),
        compiler_params=_params("parallel", "parallel", "arbitrary"),
        name="mla_attn",
    )(q3, k3, v3)
    return out.reshape(batch * seq, MLA_HEADS * MLA_V)


HG_LEVELS = int(math.log2(HG_CHUNK))
HG_UNROLL = 1


def _hg_constants(reverse):
    c = HG_CHUNK
    t = np.arange(c)[:, None]
    u = np.arange(c)[None, :]
    if not reverse:
        incl = u <= t
        rest = u > t
    else:
        incl = u >= t
        rest = u < t
    mats = [incl, rest]
    masks = []
    roles = []
    for lvl in range(1, HG_LEVELS + 1):
        size = 1 << lvl
        start = (t // size) * size
        mid = start + size // 2
        upper = t >= mid
        if not reverse:
            q_side = (u >= mid) & (u <= t)
            k_side = (u > t) & (u <= mid - 1)
            is_query = upper
        else:
            q_side = (u >= t) & (u < mid)
            k_side = (u >= mid) & (u < t)
            is_query = ~upper
        mats.append(np.where(is_query, q_side, k_side))
        same = (t // size) == (u // size)
        key_row = (~is_query).T
        masks.append(same & is_query & np.broadcast_to(key_row, (c, c)))
        roles.append(np.broadcast_to(is_query, (c, LANES)))
    w = np.concatenate(mats, axis=0).astype(np.float32)
    return (jnp.asarray(w, BF16), jnp.asarray(np.stack(masks).astype(np.float32)),
            jnp.asarray(np.stack(roles).astype(np.float32)))


def _hg_chunk(q, zf, v_b, lb, w_ref, m_ref, r_ref, st, total_row):
    c = HG_CHUNK
    sg = jax.nn.sigmoid(zf)
    k = (1.0 - lb) * jax.nn.sigmoid(-zf)
    lf = jnp.log(lb + (1.0 - lb) * sg)
    hi = lf.astype(BF16)
    lo = (lf - hi.astype(F32)).astype(BF16)
    f = lf.shape[1]
    ex2 = _dot(w_ref[...], jnp.concatenate([hi, lo], axis=1))
    ex = ex2[:, :f] + ex2[:, f:]
    b_incl = ex[0:c]
    total = ex[total_row:total_row + 1]
    qd = q * jnp.exp(b_incl)
    kd = k * jnp.exp(ex[c:2 * c])
    row = lax.broadcasted_iota(jnp.int32, (c, c), 0)
    col = lax.broadcasted_iota(jnp.int32, (c, c), 1)
    a = jnp.where(row == col, jnp.sum(q * k, axis=-1, keepdims=True), 0.0)
    xs = []
    for lvl in range(HG_LEVELS):
        x = jnp.where(r_ref[lvl] > 0.5, q, k) * jnp.exp(ex[(2 + lvl) * c:(3 + lvl) * c])
        xs.append(x.astype(BF16))
    for lvl in range(0, HG_LEVELS, 2):
        pair = xs[lvl:lvl + 2]
        xb = jnp.concatenate(pair, axis=0)
        g = _dot_nt(xb, xb)
        for j in range(len(pair)):
            a = a + m_ref[lvl + j] * g[j * c:(j + 1) * c, j * c:(j + 1) * c]
    o = _dot(a.astype(BF16), v_b) + _dot_nt(qd.astype(BF16), st.astype(BF16))
    return o, st * jnp.exp(total) + _dot_tn(v_b, kd.astype(BF16))


def _hg_kernel(zq_ref, zff_ref, zfb_ref, zi_ref, zg_ref, lb_ref, g_ref,
               wf_ref, mf_ref, rf_ref, wb_ref, mb_ref, rb_ref,
               o_ref, accf_ref, accb_ref):
    c = HG_CHUNK
    seq = zq_ref.shape[0]
    n = seq // c
    lb_f = lb_ref[0:1, :]
    lb_b = lb_ref[1:2, :]

    def gated(ref, rows):
        z = ref[rows, :]
        return z * jax.nn.sigmoid(z)

    def body(i, carry):
        st_f, st_b = carry
        rf = pl.ds(pl.multiple_of(i * c, c), c)
        rb = pl.ds(pl.multiple_of((n - 1 - i) * c, c), c)
        of, st_f = _hg_chunk(gated(zq_ref, rf), zff_ref[rf, :], zi_ref[rf, :].astype(BF16), lb_f,
                             wf_ref, mf_ref, rf_ref, st_f, c - 1)
        accf_ref[rf, :] = of
        ob, st_b = _hg_chunk(gated(zq_ref, rb), zfb_ref[rb, :], zi_ref[rb, :].astype(BF16), lb_b,
                             wb_ref, mb_ref, rb_ref, st_b, 0)
        accb_ref[rb, :] = ob
        return st_f, st_b

    zero = jnp.zeros((HG_F, HG_F), F32)
    lax.fori_loop(0, n, body, (zero, zero), unroll=HG_UNROLL)

    def finish(i, carry):
        rows = pl.ds(pl.multiple_of(i * c, c), c)
        o = accf_ref[rows, :] + accb_ref[rows, :]
        o = o * lax.rsqrt(jnp.mean(o * o, axis=-1, keepdims=True) + NORM_EPS)
        o_ref[rows, :] = (o * g_ref[...] * gated(zg_ref, rows)).astype(o_ref.dtype)
        return carry

    lax.fori_loop(0, n, finish, 0)


def hgrn2_scan(z, lb, norm_g, batch, seq):
    d = D_MODEL
    z3 = z.reshape(batch, seq, 5 * d)
    wf, mf, rf = _hg_constants(False)
    wb, mb, rb = _hg_constants(True)
    zspec = lambda j: pl.BlockSpec((None, seq, HG_F), lambda b, h: (b, 0, j * HG_HEADS + h))
    full = lambda a: pl.BlockSpec(a.shape, lambda b, h: (0,) * a.ndim)
    out = pl.pallas_call(
        _hg_kernel,
        grid=(batch, HG_HEADS),
        in_specs=[zspec(0), zspec(1), zspec(2), zspec(3), zspec(4),
                  pl.BlockSpec((2, HG_F), lambda b, h: (0, h)),
                  pl.BlockSpec((1, HG_F), lambda b, h: (0, h)),
                  full(wf), full(mf), full(rf), full(wb), full(mb), full(rb)],
        out_specs=pl.BlockSpec((None, seq, HG_F), lambda b, h: (b, 0, h)),
        out_shape=jax.ShapeDtypeStruct((batch, seq, d), BF16),
        scratch_shapes=[pltpu.VMEM((seq, HG_F), F32), pltpu.VMEM((seq, HG_F), F32)],
        compiler_params=_params("parallel", "parallel"),
        name="hgrn2_scan",
    )(z3, z3, z3, z3, z3, lb.astype(F32), norm_g.reshape(1, d).astype(F32),
      wf, mf, rf, wb, mb, rb)
    return out.reshape(batch * seq, d)


def _merge_sort_network(n):
    pairs = []
    p = 1
    while p < n:
        k = p
        while k >= 1:
            for j in range(k % p, n - k, 2 * k):
                for i in range(min(k, n - j - k)):
                    if (i + j) // (2 * p) == (i + j + k) // (2 * p):
                        pairs.append((i + j, i + j + k))
            k //= 2
        p *= 2
    return pairs


def _top_values(s, count):
    rows = s.shape[0]
    n = rows // F32_SUBLANES
    size = 1 << (n - 1).bit_length()
    neg = jnp.full((F32_SUBLANES, s.shape[1]), -jnp.inf, F32)
    lists = [s[i * F32_SUBLANES:(i + 1) * F32_SUBLANES] for i in range(n)] + [neg] * (size - n)
    for i, j in _merge_sort_network(size):
        hi = jnp.maximum(lists[i], lists[j])
        lists[j] = jnp.minimum(lists[i], lists[j])
        lists[i] = hi
    lists = lists[:min(size, count)]
    vals = []
    for it in range(count):
        head = lists[0]
        m = jnp.max(head, axis=0, keepdims=True)
        vals.append(m)
        hit = head >= m
        depth = min(len(lists), count - it - 1)
        for k in range(depth):
            nxt = lists[k + 1] if k + 1 < len(lists) else neg
            lists[k] = jnp.where(hit, nxt, lists[k])
        lists = lists[:max(depth, 1)]
    return jnp.concatenate(vals, axis=0)


def _peer_route_kernel(ht_ref, wq_ref, keys_ref, cnt_ref, a0_ref, r1_ref, b1_ref):
    kk = PEER_TOPK
    ht = ht_ref[...]
    for h in range(PEER_HEADS):
        scores = []
        for c in range(2):
            g = 2 * h + c
            qt = _dot(wq_ref[g * LANES:(g + 1) * LANES, :], ht)
            scores.append(_dot(keys_ref[c], qt.astype(BF16)))
        s0, s1 = scores
        sv0 = _top_values(s0, kk)
        sv1 = _top_values(s1, kk)
        r1 = jnp.full(s1.shape, float(kk), F32)
        for b in range(kk):
            r1 = jnp.where(s1 == sv1[b:b + 1], float(b), r1)
        cands = [sv0[a:a + 1] + sv1[0:kk // (a + 1)] for a in range(kk)]
        n_cand = sum(kk // (a + 1) for a in range(kk))
        pad = (-n_cand) % 8
        if pad:
            cands.append(jnp.full((pad, ht.shape[1]), -jnp.inf, F32))
        tau = _top_values(jnp.concatenate(cands, axis=0), kk)[kk - 1:kk]
        e0 = jnp.exp(sv0 - sv0[0:1])
        e1 = jnp.exp(sv1 - sv1[0:1])
        z = jnp.zeros_like(tau)
        for a in range(kk):
            nb = kk // (a + 1)
            sel = (sv0[a:a + 1] + sv1[0:nb]) >= tau
            z = z + e0[a:a + 1] * jnp.sum(jnp.where(sel, e1[0:nb], 0.0), axis=0, keepdims=True)
        cnt = jnp.zeros(s0.shape, F32)
        for a in range(kk):
            pair_ok = (sv0[a:a + 1] + sv1) >= tau
            n_a = jnp.sum(jnp.where(pair_ok, 1.0, 0.0), axis=0, keepdims=True)
            cnt = jnp.where(s0 == sv0[a:a + 1], n_a, cnt)
        cnt_ref[h] = cnt
        a0_ref[h] = jnp.exp(s0 - sv0[0:1])
        r1_ref[h] = r1.astype(BF16)
        b1_ref[h] = (jnp.exp(s1 - sv1[0:1]) * (0.5 / z)).astype(BF16)


def peer_route(ht, wq_t, keys, tt=256):
    d, n = ht.shape
    shape = (PEER_HEADS, PEER_NKEYS, n)
    ospec = pl.BlockSpec((PEER_HEADS, PEER_NKEYS, tt), lambda i: (0, 0, i))
    return pl.pallas_call(
        _peer_route_kernel,
        grid=(n // tt,),
        in_specs=[pl.BlockSpec((d, tt), lambda i: (0, i)),
                  pl.BlockSpec(wq_t.shape, lambda i: (0, 0)),
                  pl.BlockSpec(keys.shape, lambda i: (0, 0, 0))],
        out_specs=[ospec, ospec, ospec, ospec],
        out_shape=[jax.ShapeDtypeStruct(shape, F32), jax.ShapeDtypeStruct(shape, F32),
                   jax.ShapeDtypeStruct(shape, BF16), jax.ShapeDtypeStruct(shape, BF16)],
        compiler_params=_params("parallel"),
        name="peer_route",
    )(ht, wq_t, keys)


def _peer_gate_chunk(hid, i0, cnt_ref, a0_ref, r1_ref, b1_ref):
    tt = hid.shape[1]
    zero = jnp.zeros((PEER_NKEYS, LANES), BF16)

    def row_tile(ref, h, ii, cols):
        row = jnp.broadcast_to(ref[h, ii:ii + 1, cols], (BF16_SUBLANES, LANES)).astype(BF16)
        return pltpu.repeat(row, PEER_NKEYS // BF16_SUBLANES, axis=0)

    out_rows = []
    for k in range(PEER_CHUNK_ROWS):
        tiles = []
        for tc in range(tt // LANES):
            cols = slice(tc * LANES, (tc + 1) * LANES)
            gate = zero
            for h in range(PEER_HEADS):
                cnt = row_tile(cnt_ref, h, i0 + k, cols)
                a0 = row_tile(a0_ref, h, i0 + k, cols)
                gate = gate + jnp.where(r1_ref[h, :, cols] < cnt, a0 * b1_ref[h, :, cols], zero)
            x = hid[k * PEER_NKEYS:(k + 1) * PEER_NKEYS, cols]
            act = x.astype(BF16) * (1.0 + lax.erf(x * RSQRT2)).astype(BF16)
            tiles.append(gate * act)
        out_rows.append(jnp.concatenate(tiles, axis=1))
    return jnp.concatenate(out_rows, axis=0)


def _pack_rows_kernel(x_ref, o_ref):
    o_ref[...] = pltpu.bitcast(x_ref[...].astype(BF16), jnp.uint32)


def _pack_rows_t_kernel(x_ref, o_ref):
    o_ref[...] = pltpu.bitcast(x_ref[...].T.astype(BF16), jnp.uint32)


def pack_expert_tables(u, v):
    layers, n_exp, d = u.shape
    eb = PEER_EXPERT_BLOCK
    nb = n_exp // eb
    u_packed = pl.pallas_call(
        _pack_rows_kernel,
        grid=(layers, nb),
        in_specs=[pl.BlockSpec((None, eb, d), lambda l, e: (l, e, 0))],
        out_specs=pl.BlockSpec((None, eb // 2, d), lambda l, e: (l, e, 0)),
        out_shape=jax.ShapeDtypeStruct((layers, n_exp // 2, d), jnp.uint32),
        compiler_params=_params("parallel", "parallel"),
        name="pack_u",
    )(u)
    v_packed = pl.pallas_call(
        _pack_rows_t_kernel,
        grid=(layers, nb),
        in_specs=[pl.BlockSpec((None, eb, d), lambda l, e: (l, e, 0))],
        out_specs=pl.BlockSpec((None, None, d // 2, eb), lambda l, e: (l, e, 0, 0)),
        out_shape=jax.ShapeDtypeStruct((layers, nb, d // 2, eb), jnp.uint32),
        compiler_params=_params("parallel", "parallel"),
        name="pack_vt",
    )(v)
    return u_packed, v_packed


def _unpack_rows(words):
    return pltpu.bitcast(words, BF16)


def _ordered_after(x, dep):
    z = pltpu.bitcast(dep[:BF16_SUBLANES, :LANES], jnp.uint32)
    z = (z >> 16) >> 16
    zero = pltpu.bitcast(z, BF16)
    zero = pltpu.repeat(pltpu.repeat(zero, x.shape[0] // BF16_SUBLANES, axis=0), x.shape[1] // LANES, axis=1)
    return x + zero


def _peer_ffn_kernel(ht_ref, u0_ref, ub_ref, un_ref, vta_ref, vtb_ref, cnta_ref, a0a_ref, cntb_ref, a0b_ref,
                     r1_in_ref, b1_in_ref, res_ref, g_ref, b_ref,
                     o_ref, acc_ref, hida_ref, hidb_ref, r1_ref, b1_ref):
    s = pl.program_id(1)
    half = ht_ref.shape[1] // 2

    @pl.when(s == 0)
    def _():
        acc_ref[...] = jnp.zeros_like(acc_ref)
        r1_ref[...] = r1_in_ref[...]
        b1_ref[...] = b1_in_ref[...]
        hida_ref[...] = _dot(_unpack_rows(u0_ref[...]), ht_ref[...])

    chunk = PEER_CHUNK_ROWS * PEER_NKEYS
    n_chunks = PEER_EXPERT_BLOCK // chunk

    def run_block(acc, hid_ref, vt_ref, cnt_ref, a0_ref, next_u_ref, next_hid_ref):
        for c in range(n_chunks):
            span = slice(c * chunk, (c + 1) * chunk)
            w = _peer_gate_chunk(hid_ref[span, :], c * PEER_CHUNK_ROWS, cnt_ref, a0_ref, r1_ref, b1_ref)
            acc = acc + _dot(_unpack_rows(vt_ref[:, span]), w)
            if c % (n_chunks // 2) == n_chunks // 2 - 1:
                hh = c // (n_chunks // 2)
                tcols = slice(hh * half, (hh + 1) * half)
                next_hid_ref[:, tcols] = _dot(_unpack_rows(next_u_ref[...]),
                                              _ordered_after(ht_ref[:, tcols], w))
        return acc

    acc = run_block(acc_ref[...], hida_ref, vta_ref, cnta_ref, a0a_ref, ub_ref, hidb_ref)
    acc_ref[...] = run_block(acc, hidb_ref, vtb_ref, cntb_ref, a0b_ref, un_ref, hida_ref)

    @pl.when(s == pl.num_programs(1) - 1)
    def _():
        y = acc_ref[...].T
        o_ref[...] = _layer_norm_rows(DN_ALPHA * res_ref[...] + y, g_ref[...], b_ref[...])


def peer_ffn(h, ht, route, u, v_blocks, layer, g, b, tt=512):
    n, d = h.shape
    eb = PEER_EXPERT_BLOCK
    last = 2 * u.shape[1] // eb - 1
    rspec = pl.BlockSpec((PEER_HEADS, PEER_NKEYS, tt), lambda i, s: (0, 0, i))
    rowspec = lambda off: pl.BlockSpec((PEER_HEADS, eb // PEER_NKEYS, tt), lambda i, s: (0, 2 * s + off, i))
    cnt, a0, r1, b1 = route
    return pl.pallas_call(
        _peer_ffn_kernel,
        grid=(n // tt, (last + 1) // 2),
        in_specs=[pl.BlockSpec((d, tt), lambda i, s: (0, i)),
                  pl.BlockSpec((None, eb // 2, d), lambda i, s: (layer, 0, 0)),
                  pl.BlockSpec((None, eb // 2, d), lambda i, s: (layer, 2 * s + 1, 0)),
                  pl.BlockSpec((None, eb // 2, d), lambda i, s: (layer, jnp.minimum(2 * s + 2, last), 0)),
                  pl.BlockSpec((None, None, d // 2, eb), lambda i, s: (layer, 2 * s, 0, 0)),
                  pl.BlockSpec((None, None, d // 2, eb), lambda i, s: (layer, 2 * s + 1, 0, 0)),
                  rowspec(0), rowspec(0), rowspec(1), rowspec(1),
                  rspec, rspec,
                  pl.BlockSpec((tt, d), lambda i, s: (i, 0)),
                  pl.BlockSpec((1, d), lambda i, s: (0, 0)),
                  pl.BlockSpec((1, d), lambda i, s: (0, 0))],
        out_specs=pl.BlockSpec((tt, d), lambda i, s: (i, 0)),
        out_shape=jax.ShapeDtypeStruct((n, d), F32),
        scratch_shapes=[pltpu.VMEM((d, tt), F32), pltpu.VMEM((eb, tt), F32), pltpu.VMEM((eb, tt), F32),
                        pltpu.VMEM((PEER_HEADS, PEER_NKEYS, tt), BF16),
                        pltpu.VMEM((PEER_HEADS, PEER_NKEYS, tt), BF16)],
        compiler_params=_params("parallel", "arbitrary"),
        name="peer_ffn",
    )(ht, u, u, u, v_blocks, v_blocks, cnt, a0, cnt, a0, r1, b1, h, g.reshape(1, d), b.reshape(1, d))


def kernel(x, na_w_in, na_rel_bias, na_w_out, mla_w_in, mla_q_norm, mla_kv_norm, mla_w_q_up, mla_w_kv_up, mla_w_out, hg_w_in, hg_lower_bound, hg_norm, hg_w_out, peer_w_q, peer_sub_keys, peer_u, peer_v, ln_mix_g, ln_mix_b, ln_ffn_g, ln_ffn_b):
    batch, seq, d = x.shape
    lb_w = jax.nn.softmax(hg_lower_bound.astype(F32), axis=0)
    lb_all = jnp.cumsum(lb_w, axis=0) - lb_w[0:1]
    h = x.reshape(batch * seq, d)
    u_all, v_all = pack_expert_tables(peer_u, peer_v)
    for layer in range(DEPTH):
        kind = layer % N_MIXERS
        j = layer // N_MIXERS
        if kind == 0:
            qk, vt = na_project(h, na_w_in[j], batch)
            mix_in = neighborhood_attention(qk, vt, na_rel_bias[j])
            w_out = na_w_out[j]
        elif kind == 1:
            q, k, v = mla_projections(h, mla_w_in[j], mla_q_norm[j], mla_kv_norm[j],
                                      mla_w_q_up[j], mla_w_kv_up[j], seq)
            mix_in = mla_attention(q, k, v, batch, seq)
            w_out = mla_w_out[j]
        else:
            z = matmul(h, hg_w_in[j].astype(BF16), F32)
            mix_in = hgrn2_scan(z, lb_all[layer], hg_norm[j], batch, seq)
            w_out = hg_w_out[j]
        h, ht = matmul_res_ln(mix_in, w_out.astype(BF16), h, ln_mix_g[layer], ln_mix_b[layer])
        route = peer_route(ht, peer_w_q[layer].T.astype(BF16), peer_sub_keys[layer].astype(BF16))
        h = peer_ffn(h, ht, route, u_all, v_all, layer, ln_ffn_g[layer], ln_ffn_b[layer])
    return h.reshape(batch, seq, d)
```

```python
import math

import jax
import jax.numpy as jnp
import numpy as np
from jax import lax
from jax.experimental import pallas as pl
from jax.experimental.pallas import tpu as pltpu

F32 = jnp.float32
BF16 = jnp.bfloat16

D_MODEL = 1024
DEPTH = 4
GRID_W = 64
N_MIXERS = 3

NA_HEADS = 16
NA_HEAD_DIM = 64
NA_KH = 8
NA_KW = 16

MLA_HEADS = 16
MLA_NOPE = 64
MLA_ROPE = 32
MLA_V = 64
MLA_Q_RANK = 256
MLA_KV_RANK = 256
ROPE_THETA = 10000.0

HG_HEADS = 8
HG_F = 128
HG_CHUNK = 128

PEER_HEADS = 8
PEER_NKEYS = 128
PEER_TOPK = 16
PEER_EXPERT_BLOCK = 1024
PEER_CHUNK_ROWS = 2
NORM_EPS = 1e-5
DN_ALPHA = (2.0 * DEPTH) ** 0.25

LANES = 128
F32_SUBLANES = 8
BF16_SUBLANES = 16
VMEM_LIMIT = 48 * 1024 * 1024
MASK_NEG = -1e30
RSQRT2 = 0.7071067811865476


def _params(*sem):
    return pltpu.CompilerParams(dimension_semantics=sem, vmem_limit_bytes=VMEM_LIMIT)


def _layer_norm_rows(z, g, b):
    mu = jnp.mean(z, axis=-1, keepdims=True)
    zc = z - mu
    var = jnp.mean(zc * zc, axis=-1, keepdims=True)
    return zc * lax.rsqrt(var + NORM_EPS) * g + b


def _dot(a, b):
    return jnp.dot(a, b, preferred_element_type=F32)


def _dot_nt(a, b):
    return lax.dot_general(a, b, (((1,), (1,)), ((), ())), preferred_element_type=F32)


def _dot_tn(a, b):
    return lax.dot_general(a, b, (((0,), (0,)), ((), ())), preferred_element_type=F32)


def _mm_kernel(a_ref, w_ref, o_ref):
    o_ref[...] = _dot(a_ref[...].astype(BF16), w_ref[...]).astype(o_ref.dtype)


def matmul(a, w, out_dtype, tm=512, tn=1024):
    m, k = a.shape
    n = w.shape[1]
    tn = min(tn, n)
    return pl.pallas_call(
        _mm_kernel,
        grid=(m // tm, n // tn),
        in_specs=[pl.BlockSpec((tm, k), lambda i, j: (i, 0)),
                  pl.BlockSpec((k, tn), lambda i, j: (0, j))],
        out_specs=pl.BlockSpec((tm, tn), lambda i, j: (i, j)),
        out_shape=jax.ShapeDtypeStruct((m, n), out_dtype),
        compiler_params=_params("parallel", "parallel"),
        name="proj",
    )(a, w)


def _mm_res_ln_kernel(a_ref, w_ref, res_ref, g_ref, b_ref, o_ref, ot_ref):
    y = _dot(a_ref[...], w_ref[...])
    out = _layer_norm_rows(DN_ALPHA * res_ref[...] + y, g_ref[...], b_ref[...])
    o_ref[...] = out
    ot_ref[...] = out.T.astype(BF16)


def matmul_res_ln(a, w, res, g, b, tm=512):
    m, k = a.shape
    n = w.shape[1]
    return pl.pallas_call(
        _mm_res_ln_kernel,
        grid=(m // tm,),
        in_specs=[pl.BlockSpec((tm, k), lambda i: (i, 0)),
                  pl.BlockSpec((k, n), lambda i: (0, 0)),
                  pl.BlockSpec((tm, n), lambda i: (i, 0)),
                  pl.BlockSpec((1, n), lambda i: (0, 0)),
                  pl.BlockSpec((1, n), lambda i: (0, 0))],
        out_specs=[pl.BlockSpec((tm, n), lambda i: (i, 0)),
                   pl.BlockSpec((n, tm), lambda i: (0, i))],
        out_shape=[jax.ShapeDtypeStruct((m, n), F32),
                   jax.ShapeDtypeStruct((n, m), BF16)],
        compiler_params=_params("parallel"),
        name="out_proj_ln",
    )(a, w, res, g.reshape(1, n), b.reshape(1, n))


def _na_proj_kernel(h_ref, wqk_ref, wvt_ref, qk_ref, vt_ref):
    hb = h_ref[...].astype(BF16)
    qk_ref[...] = _dot(hb, wqk_ref[...]).astype(qk_ref.dtype).reshape(qk_ref.shape)
    vt_ref[...] = _dot_nt(wvt_ref[...], hb).astype(vt_ref.dtype)


def na_project(h, w_in, batch, tm=512):
    n, d = h.shape
    rows = n // (batch * GRID_W)
    tile_rows = tm // GRID_W
    tiles = rows // tile_rows
    wqk = w_in[:, :2 * d].astype(BF16)
    wvt = w_in[:, 2 * d:].T.astype(BF16)
    return pl.pallas_call(
        _na_proj_kernel,
        grid=(n // tm,),
        in_specs=[pl.BlockSpec((tm, d), lambda i: (i, 0)),
                  pl.BlockSpec((d, 2 * d), lambda i: (0, 0)),
                  pl.BlockSpec((d, d), lambda i: (0, 0))],
        out_specs=[pl.BlockSpec((None, tile_rows, GRID_W, 2 * d), lambda i: (i // tiles, i % tiles, 0, 0)),
                   pl.BlockSpec((d, tm), lambda i: (0, i))],
        out_shape=[jax.ShapeDtypeStruct((batch, rows, GRID_W, 2 * d), BF16),
                   jax.ShapeDtypeStruct((d, n), BF16)],
        compiler_params=_params("parallel"),
        name="na_proj",
    )(h, wqk, wvt)


NA_QROWS = 2
NA_KROWS = NA_KH + NA_QROWS
NA_EDGE = NA_KH // 4


def _na_key_start(i, rows):
    return jnp.clip(i - NA_KH // 4, 0, (rows - NA_KROWS) // 2)


def _na_kernel(q_ref, k_ref, vt_ref, bias_ref, o_ref):
    n_keys = NA_KROWS * GRID_W
    n_q = NA_QROWS * GRID_W
    lane = lax.broadcasted_iota(jnp.int32, (n_q, LANES), 1)
    low = lane < NA_HEAD_DIM
    row_low = lax.broadcasted_iota(jnp.int32, (LANES, n_q), 0) < NA_HEAD_DIM
    for p in range(NA_HEADS // 2):
        cols = slice(p * LANES, (p + 1) * LANES)
        qp = q_ref[0, :, :, cols].reshape(n_q, LANES)
        zq = jnp.zeros_like(qp)
        kp = k_ref[0, :, :, cols].reshape(n_keys, LANES)
        vtp = vt_ref[cols, :]
        outs = []
        for half in range(2):
            qh = jnp.where(low, qp, zq) if half == 0 else jnp.where(low, zq, qp)
            s = _dot_nt(kp, qh) * (NA_HEAD_DIM ** -0.5) + bias_ref[2 * p + half]
            m = jnp.max(s, axis=0, keepdims=True)
            e = jnp.exp(s - m)
            l = jnp.sum(e, axis=0, keepdims=True)
            outs.append(_dot(vtp, e.astype(BF16)) / l)
        o2 = jnp.where(row_low, outs[0], outs[1]).T.astype(o_ref.dtype)
        o_ref[:, cols] = o2


def _na_bias_table(rel_bias, rows):
    cols = np.arange(GRID_W)
    c0 = np.clip(cols - NA_KW // 2, 0, GRID_W - NA_KW)
    kc = np.arange(GRID_W)
    inside = (kc[None, :] >= c0[:, None]) & (kc[None, :] < c0[:, None] + NA_KW)
    dc = np.clip(kc[None, :] - cols[:, None] + (NA_KW - 1), 0, 2 * NA_KW - 2)
    b2 = jnp.where(inside[None, None], rel_bias[:, :, dc].astype(F32), MASK_NEG)
    steps = rows // NA_QROWS
    variant_steps = list(range(NA_EDGE)) + [NA_EDGE] + list(range(steps - NA_EDGE, steps))
    dr = np.zeros((len(variant_steps), NA_QROWS, NA_KROWS), np.int32)
    ok = np.zeros(dr.shape, bool)
    for v, i in enumerate(variant_steps):
        key0 = 2 * int(np.clip(i - NA_KH // 4, 0, (rows - NA_KROWS) // 2))
        for a in range(NA_QROWS):
            r = NA_QROWS * i + a
            r0 = int(np.clip(r - NA_KH // 2, 0, rows - NA_KH))
            key_rows = key0 + np.arange(NA_KROWS)
            ok[v, a] = (key_rows >= r0) & (key_rows < r0 + NA_KH)
            dr[v, a] = np.clip(key_rows - r + NA_KH - 1, 0, 2 * NA_KH - 2)
    b2t = jnp.swapaxes(b2, -1, -2)
    per_row = []
    for a in range(NA_QROWS):
        t = b2t[:, dr[:, a]]
        per_row.append(jnp.where(ok[None, :, a, :, None, None], t, MASK_NEG))
    t = jnp.concatenate(per_row, axis=-1)
    return t.reshape(NA_HEADS, len(variant_steps), NA_KROWS * GRID_W, NA_QROWS * GRID_W)


def neighborhood_attention(qk4, vt, rel_bias):
    d = D_MODEL
    batch, rows = qk4.shape[:2]
    steps = rows // NA_QROWS
    n_keys = NA_KROWS * GRID_W
    bias = _na_bias_table(rel_bias, rows)
    el = pl.Element
    key0 = lambda i: _na_key_start(i, rows)

    def variant(i):
        return jnp.where(i < NA_EDGE, i, jnp.where(i >= steps - NA_EDGE, i - (steps - 2 * NA_EDGE - 1), NA_EDGE))

    return pl.pallas_call(
        _na_kernel,
        grid=(batch, steps),
        in_specs=[pl.BlockSpec((el(1), el(NA_QROWS), el(GRID_W), el(d)),
                               lambda b, i: (b, NA_QROWS * i, 0, 0)),
                  pl.BlockSpec((el(1), el(NA_KROWS), el(GRID_W), el(d)),
                               lambda b, i: (b, 2 * key0(i), 0, d)),
                  pl.BlockSpec((el(d), el(n_keys)),
                               lambda b, i: (0, (b * (rows // 2) + key0(i)) * (2 * GRID_W))),
                  pl.BlockSpec((NA_HEADS, None, n_keys, NA_QROWS * GRID_W),
                               lambda b, i: (0, variant(i), 0, 0))],
        out_specs=pl.BlockSpec((NA_QROWS * GRID_W, d), lambda b, i: (b * steps + i, 0)),
        out_shape=jax.ShapeDtypeStruct((batch * rows * GRID_W, d), BF16),
        compiler_params=_params("parallel", "arbitrary"),
        name="na_attn",
    )(qk4, qk4, vt, bias)


def _rms_rows(x, g):
    return x * lax.rsqrt(jnp.mean(x * x, axis=-1, keepdims=True) + NORM_EPS) * g


def _mla_proj_kernel(x_ref, win_ref, qn_ref, kvn_ref, wqa_ref, wqb_ref, wkv_ref, cos_ref, sin_ref,
                     q_ref, k_ref, v_ref):
    hd = MLA_HEADS * LANES
    hin = _dot(x_ref[...].astype(BF16), win_ref[...])
    cq = _rms_rows(hin[:, :MLA_Q_RANK], qn_ref[...]).astype(BF16)
    ckv = _rms_rows(hin[:, MLA_Q_RANK:MLA_Q_RANK + MLA_KV_RANK], kvn_ref[...]).astype(BF16)
    cos = cos_ref[...]
    sin = sin_ref[...]
    cos_t = jnp.tile(cos, (1, MLA_HEADS))
    sin_t = jnp.tile(sin, (1, MLA_HEADS))
    q = _dot(cq, wqa_ref[...]) * cos_t + _dot(cq, wqb_ref[...]) * sin_t
    q_ref[...] = (q * ((MLA_NOPE + MLA_ROPE) ** -0.5)).astype(q_ref.dtype)
    kv = _dot(ckv, wkv_ref[...])
    base = MLA_Q_RANK + MLA_KV_RANK
    kpe = hin[:, base:base + LANES] * cos + hin[:, base + LANES:base + 2 * LANES] * sin
    k_ref[...] = (kv[:, :hd] + jnp.tile(kpe, (1, MLA_HEADS))).astype(k_ref.dtype)
    v_ref[...] = kv[:, hd:].astype(v_ref.dtype)


def _mla_weights(w_in, w_q_up, w_kv_up):
    r = MLA_ROPE
    half = r // 2
    dq = MLA_NOPE + r
    nh = MLA_HEADS
    base = MLA_Q_RANK + MLA_KV_RANK
    kpe = w_in[:, base:base + r]
    zpad = lambda w, lo, hi: jnp.pad(w, ((0, 0), (lo, hi)))
    swap = lambda w: jnp.concatenate([-w[..., half:], w[..., :half]], axis=-1)
    kpe_a = zpad(kpe, MLA_NOPE, LANES - MLA_NOPE - r)
    kpe_b = zpad(swap(kpe), MLA_NOPE, LANES - MLA_NOPE - r)
    win = jnp.concatenate([w_in[:, :base], kpe_a, kpe_b], axis=1)
    wq = w_q_up.reshape(MLA_Q_RANK, nh, dq)
    pad3 = lambda w, lo, hi: jnp.pad(w, ((0, 0), (0, 0), (lo, hi)))
    wqa = pad3(wq, 0, LANES - dq).reshape(MLA_Q_RANK, nh * LANES)
    wqb = pad3(swap(wq[:, :, MLA_NOPE:]), MLA_NOPE, LANES - dq).reshape(MLA_Q_RANK, nh * LANES)
    wkv = w_kv_up.reshape(MLA_KV_RANK, nh, MLA_NOPE + MLA_V)
    wk = pad3(wkv[:, :, :MLA_NOPE], 0, LANES - MLA_NOPE).reshape(MLA_KV_RANK, nh * LANES)
    wv = wkv[:, :, MLA_NOPE:].reshape(MLA_KV_RANK, nh * MLA_V)
    return (win.astype(BF16), wqa.astype(BF16), wqb.astype(BF16),
            jnp.concatenate([wk, wv], axis=1).astype(BF16))


def _rope_tables(seq):
    half = MLA_ROPE // 2
    inv_freq = ROPE_THETA ** (-jnp.arange(half, dtype=F32) * 2.0 / MLA_ROPE)
    ang = jnp.arange(seq, dtype=F32)[:, None] * inv_freq[None, :]
    cos = jnp.cos(ang)
    sin = jnp.sin(ang)
    tail = LANES - MLA_NOPE - MLA_ROPE
    cos_p = jnp.concatenate([jnp.ones((seq, MLA_NOPE), F32), cos, cos, jnp.zeros((seq, tail), F32)], axis=1)
    sin_p = jnp.concatenate([jnp.zeros((seq, MLA_NOPE), F32), sin, sin, jnp.zeros((seq, tail), F32)], axis=1)
    return cos_p, sin_p


def mla_projections(h, w_in, q_norm, kv_norm, w_q_up, w_kv_up, seq, tm=256):
    n, d = h.shape
    win, wqa, wqb, wkv = _mla_weights(w_in, w_q_up, w_kv_up)
    cos_p, sin_p = _rope_tables(seq)
    hd = MLA_HEADS * LANES
    per_seq = seq // tm
    full = lambda a: pl.BlockSpec(a.shape, lambda i: (0,) * a.ndim)
    qn = q_norm.reshape(1, -1).astype(F32)
    kvn = kv_norm.reshape(1, -1).astype(F32)
    return pl.pallas_call(
        _mla_proj_kernel,
        grid=(n // tm,),
        in_specs=[pl.BlockSpec((tm, d), lambda i: (i, 0)), full(win), full(qn), full(kvn),
                  full(wqa), full(wqb), full(wkv),
                  pl.BlockSpec((tm, LANES), lambda i: (i % per_seq, 0)),
                  pl.BlockSpec((tm, LANES), lambda i: (i % per_seq, 0))],
        out_specs=[pl.BlockSpec((tm, hd), lambda i: (i, 0)),
                   pl.BlockSpec((tm, hd), lambda i: (i, 0)),
                   pl.BlockSpec((tm, MLA_HEADS * MLA_V), lambda i: (i, 0))],
        out_shape=[jax.ShapeDtypeStruct((n, hd), BF16),
                   jax.ShapeDtypeStruct((n, hd), BF16),
                   jax.ShapeDtypeStruct((n, MLA_HEADS * MLA_V), BF16)],
        compiler_params=_params("parallel"),
        name="mla_proj",
    )(h, win, qn, kvn, wqa, wqb, wkv, cos_p, sin_p)


def _mla_attn_kernel(q_ref, k_ref, v_ref, o_ref):
    v = v_ref[...]
    lane = lax.broadcasted_iota(jnp.int32, o_ref.shape, 1)
    halves = []
    for half in range(2):
        cols = slice(half * LANES, (half + 1) * LANES)
        s = _dot_nt(q_ref[:, cols], k_ref[:, cols])
        m = jnp.max(s, axis=-1, keepdims=True)
        e = jnp.exp(s - m)
        l = jnp.sum(e, axis=-1, keepdims=True)
        halves.append(_dot(e.astype(BF16), v) / l)
    o_ref[...] = jnp.where(lane < MLA_V, halves[0], halves[1]).astype(o_ref.dtype)


def mla_attention(q, k, v, batch, seq, tq=256):
    hd = MLA_HEADS * LANES
    q3 = q.reshape(batch, seq, hd)
    k3 = k.reshape(batch, seq, hd)
    v3 = v.reshape(batch, seq, MLA_HEADS * MLA_V)
    out = pl.pallas_call(
        _mla_attn_kernel,
        grid=(batch, MLA_HEADS // 2, seq // tq),
        in_specs=[pl.BlockSpec((None, tq, 2 * LANES), lambda b, p, i: (b, i, p)),
                  pl.BlockSpec((None, seq, 2 * LANES), lambda b, p, i: (b, 0, p)),
                  pl.BlockSpec((None, seq, LANES), lambda b, p, i: (b, 0, p))],
        out_specs=pl.BlockSpec((None, tq, LANES), lambda b, p, i: (b, i, p)),
        out_shape=jax.ShapeDtypeStruct((batch, seq, MLA_HEADS * MLA_V), BF16),
        compiler_params=_params("parallel", "parallel", "arbitrary"),
        name="mla_attn",
    )(q3, k3, v3)
    return out.reshape(batch * seq, MLA_HEADS * MLA_V)


HG_LEVELS = int(math.log2(HG_CHUNK))


def _hg_constants(reverse):
    c = HG_CHUNK
    t = np.arange(c)[:, None]
    u = np.arange(c)[None, :]
    if not reverse:
        incl = u <= t
        rest = u > t
    else:
        incl = u >= t
        rest = u < t
    mats = [incl, rest]
    masks = []
    roles = []
    for lvl in range(1, HG_LEVELS + 1):
        size = 1 << lvl
        start = (t // size) * size
        mid = start + size // 2
        upper = t >= mid
        if not reverse:
            q_side = (u >= mid) & (u <= t)
            k_side = (u > t) & (u <= mid - 1)
            is_query = upper
        else:
            q_side = (u >= t) & (u < mid)
            k_side = (u >= mid) & (u < t)
            is_query = ~upper
        mats.append(np.where(is_query, q_side, k_side))
        same = (t // size) == (u // size)
        key_row = (~is_query).T
        masks.append(same & is_query & np.broadcast_to(key_row, (c, c)))
        roles.append(np.broadcast_to(is_query, (c, LANES)))
    w = np.concatenate(mats, axis=0).astype(np.float32)
    return (jnp.asarray(w, BF16), jnp.asarray(np.stack(masks).astype(np.float32)),
            jnp.asarray(np.stack(roles).astype(np.float32)))


def _hg_chunk(q, zf, v_b, lb, w_ref, m_ref, r_ref, st, total_row):
    c = HG_CHUNK
    sg = jax.nn.sigmoid(zf)
    k = (1.0 - lb) * jax.nn.sigmoid(-zf)
    lf = jnp.log(lb + (1.0 - lb) * sg)
    hi = lf.astype(BF16)
    lo = (lf - hi.astype(F32)).astype(BF16)
    f = lf.shape[1]
    ex2 = _dot(w_ref[...], jnp.concatenate([hi, lo], axis=1))
    ex = ex2[:, :f] + ex2[:, f:]
    b_incl = ex[0:c]
    total = ex[total_row:total_row + 1]
    qd = q * jnp.exp(b_incl)
    kd = k * jnp.exp(ex[c:2 * c])
    row = lax.broadcasted_iota(jnp.int32, (c, c), 0)
    col = lax.broadcasted_iota(jnp.int32, (c, c), 1)
    a = jnp.where(row == col, jnp.sum(q * k, axis=-1, keepdims=True), 0.0)
    xs = []
    for lvl in range(HG_LEVELS):
        x = jnp.where(r_ref[lvl] > 0.5, q, k) * jnp.exp(ex[(2 + lvl) * c:(3 + lvl) * c])
        xs.append(x.astype(BF16))
    for lvl in range(0, HG_LEVELS, 2):
        pair = xs[lvl:lvl + 2]
        xb = jnp.concatenate(pair, axis=0)
        g = _dot_nt(xb, xb)
        for j in range(len(pair)):
            a = a + m_ref[lvl + j] * g[j * c:(j + 1) * c, j * c:(j + 1) * c]
    o = _dot(a.astype(BF16), v_b) + _dot_nt(qd.astype(BF16), st.astype(BF16))
    return o, st * jnp.exp(total) + _dot_tn(v_b, kd.astype(BF16))


def _hg_kernel(zq_ref, zff_ref, zfb_ref, zi_ref, zg_ref, lb_ref, g_ref,
               wf_ref, mf_ref, rf_ref, wb_ref, mb_ref, rb_ref,
               o_ref, accf_ref, accb_ref):
    c = HG_CHUNK
    seq = zq_ref.shape[0]
    n = seq // c
    lb_f = lb_ref[0:1, :]
    lb_b = lb_ref[1:2, :]

    def gated(ref, rows):
        z = ref[rows, :]
        return z * jax.nn.sigmoid(z)

    def body(i, carry):
        st_f, st_b = carry
        rf = pl.ds(pl.multiple_of(i * c, c), c)
        rb = pl.ds(pl.multiple_of((n - 1 - i) * c, c), c)
        of, st_f = _hg_chunk(gated(zq_ref, rf), zff_ref[rf, :], zi_ref[rf, :].astype(BF16), lb_f,
                             wf_ref, mf_ref, rf_ref, st_f, c - 1)
        accf_ref[rf, :] = of
        ob, st_b = _hg_chunk(gated(zq_ref, rb), zfb_ref[rb, :], zi_ref[rb, :].astype(BF16), lb_b,
                             wb_ref, mb_ref, rb_ref, st_b, 0)
        accb_ref[rb, :] = ob
        return st_f, st_b

    zero = jnp.zeros((HG_F, HG_F), F32)
    lax.fori_loop(0, n, body, (zero, zero))

    def finish(i, carry):
        rows = pl.ds(pl.multiple_of(i * c, c), c)
        o = accf_ref[rows, :] + accb_ref[rows, :]
        o = o * lax.rsqrt(jnp.mean(o * o, axis=-1, keepdims=True) + NORM_EPS)
        o_ref[rows, :] = (o * g_ref[...] * gated(zg_ref, rows)).astype(o_ref.dtype)
        return carry

    lax.fori_loop(0, n, finish, 0)


def hgrn2_scan(z, lb, norm_g, batch, seq):
    d = D_MODEL
    z3 = z.reshape(batch, seq, 5 * d)
    wf, mf, rf = _hg_constants(False)
    wb, mb, rb = _hg_constants(True)
    zspec = lambda j: pl.BlockSpec((None, seq, HG_F), lambda b, h: (b, 0, j * HG_HEADS + h))
    full = lambda a: pl.BlockSpec(a.shape, lambda b, h: (0,) * a.ndim)
    out = pl.pallas_call(
        _hg_kernel,
        grid=(batch, HG_HEADS),
        in_specs=[zspec(0), zspec(1), zspec(2), zspec(3), zspec(4),
                  pl.BlockSpec((2, HG_F), lambda b, h: (0, h)),
                  pl.BlockSpec((1, HG_F), lambda b, h: (0, h)),
                  full(wf), full(mf), full(rf), full(wb), full(mb), full(rb)],
        out_specs=pl.BlockSpec((None, seq, HG_F), lambda b, h: (b, 0, h)),
        out_shape=jax.ShapeDtypeStruct((batch, seq, d), BF16),
        scratch_shapes=[pltpu.VMEM((seq, HG_F), F32), pltpu.VMEM((seq, HG_F), F32)],
        compiler_params=_params("parallel", "parallel"),
        name="hgrn2_scan",
    )(z3, z3, z3, z3, z3, lb.astype(F32), norm_g.reshape(1, d).astype(F32),
      wf, mf, rf, wb, mb, rb)
    return out.reshape(batch * seq, d)


def _merge_sort_network(n):
    pairs = []
    p = 1
    while p < n:
        k = p
        while k >= 1:
            for j in range(k % p, n - k, 2 * k):
                for i in range(min(k, n - j - k)):
                    if (i + j) // (2 * p) == (i + j + k) // (2 * p):
                        pairs.append((i + j, i + j + k))
            k //= 2
        p *= 2
    return pairs


def _top_values(s, count):
    rows = s.shape[0]
    n = rows // F32_SUBLANES
    size = 1 << (n - 1).bit_length()
    neg = jnp.full((F32_SUBLANES, s.shape[1]), -jnp.inf, F32)
    lists = [s[i * F32_SUBLANES:(i + 1) * F32_SUBLANES] for i in range(n)] + [neg] * (size - n)
    for i, j in _merge_sort_network(size):
        hi = jnp.maximum(lists[i], lists[j])
        lists[j] = jnp.minimum(lists[i], lists[j])
        lists[i] = hi
    lists = lists[:min(size, count)]
    vals = []
    for it in range(count):
        head = lists[0]
        m = jnp.max(head, axis=0, keepdims=True)
        vals.append(m)
        hit = head >= m
        depth = min(len(lists), count - it - 1)
        for k in range(depth):
            nxt = lists[k + 1] if k + 1 < len(lists) else neg
            lists[k] = jnp.where(hit, nxt, lists[k])
        lists = lists[:max(depth, 1)]
    return jnp.concatenate(vals, axis=0)


def _peer_route_kernel(ht_ref, wq_ref, keys_ref, cnt_ref, a0_ref, r1_ref, b1_ref):
    kk = PEER_TOPK
    ht = ht_ref[...]
    for h in range(PEER_HEADS):
        scores = []
        for c in range(2):
            g = 2 * h + c
            qt = _dot(wq_ref[g * LANES:(g + 1) * LANES, :], ht)
            scores.append(_dot(keys_ref[c], qt.astype(BF16)))
        s0, s1 = scores
        sv0 = _top_values(s0, kk)
        sv1 = _top_values(s1, kk)
        r1 = jnp.full(s1.shape, float(kk), F32)
        for b in range(kk):
            r1 = jnp.where(s1 == sv1[b:b + 1], float(b), r1)
        cands = [sv0[a:a + 1] + sv1[0:kk // (a + 1)] for a in range(kk)]
        n_cand = sum(kk // (a + 1) for a in range(kk))
        pad = (-n_cand) % 8
        if pad:
            cands.append(jnp.full((pad, ht.shape[1]), -jnp.inf, F32))
        tau = _top_values(jnp.concatenate(cands, axis=0), kk)[kk - 1:kk]
        e0 = jnp.exp(sv0 - sv0[0:1])
        e1 = jnp.exp(sv1 - sv1[0:1])
        z = jnp.zeros_like(tau)
        for a in range(kk):
            nb = kk // (a + 1)
            sel = (sv0[a:a + 1] + sv1[0:nb]) >= tau
            z = z + e0[a:a + 1] * jnp.sum(jnp.where(sel, e1[0:nb], 0.0), axis=0, keepdims=True)
        cnt = jnp.zeros(s0.shape, F32)
        for a in range(kk):
            pair_ok = (sv0[a:a + 1] + sv1) >= tau
            n_a = jnp.sum(jnp.where(pair_ok, 1.0, 0.0), axis=0, keepdims=True)
            cnt = jnp.where(s0 == sv0[a:a + 1], n_a, cnt)
        cnt_ref[h] = cnt
        a0_ref[h] = jnp.exp(s0 - sv0[0:1])
        r1_ref[h] = r1.astype(BF16)
        b1_ref[h] = (jnp.exp(s1 - sv1[0:1]) * (0.5 / z)).astype(BF16)


def peer_route(ht, wq_t, keys, tt=512):
    d, n = ht.shape
    shape = (PEER_HEADS, PEER_NKEYS, n)
    ospec = pl.BlockSpec((PEER_HEADS, PEER_NKEYS, tt), lambda i: (0, 0, i))
    return pl.pallas_call(
        _peer_route_kernel,
        grid=(n // tt,),
        in_specs=[pl.BlockSpec((d, tt), lambda i: (0, i)),
                  pl.BlockSpec(wq_t.shape, lambda i: (0, 0)),
                  pl.BlockSpec(keys.shape, lambda i: (0, 0, 0))],
        out_specs=[ospec, ospec, ospec, ospec],
        out_shape=[jax.ShapeDtypeStruct(shape, F32), jax.ShapeDtypeStruct(shape, F32),
                   jax.ShapeDtypeStruct(shape, BF16), jax.ShapeDtypeStruct(shape, BF16)],
        compiler_params=_params("parallel"),
        name="peer_route",
    )(ht, wq_t, keys)


def _peer_gate_chunk(hid, i0, cnt_ref, a0_ref, r1_ref, b1_ref):
    tt = hid.shape[1]
    zero = jnp.zeros((PEER_NKEYS, LANES), BF16)

    def row_tile(ref, h, ii, cols):
        row = jnp.broadcast_to(ref[h, ii:ii + 1, cols], (BF16_SUBLANES, LANES)).astype(BF16)
        return jnp.tile(row, (PEER_NKEYS // BF16_SUBLANES, 1))

    out_rows = []
    for k in range(PEER_CHUNK_ROWS):
        tiles = []
        for tc in range(tt // LANES):
            cols = slice(tc * LANES, (tc + 1) * LANES)
            gate = zero
            for h in range(PEER_HEADS):
                cnt = row_tile(cnt_ref, h, i0 + k, cols)
                a0 = row_tile(a0_ref, h, i0 + k, cols)
                gate = gate + jnp.where(r1_ref[h, :, cols] < cnt, a0 * b1_ref[h, :, cols], zero)
            x = hid[k * PEER_NKEYS:(k + 1) * PEER_NKEYS, cols]
            act = x.astype(BF16) * (1.0 + lax.erf(x * RSQRT2)).astype(BF16)
            tiles.append(gate * act)
        out_rows.append(jnp.concatenate(tiles, axis=1))
    return jnp.concatenate(out_rows, axis=0)


def _pack_rows_kernel(x_ref, o_ref):
    o_ref[...] = pltpu.bitcast(x_ref[...].astype(BF16), jnp.uint32)


def _pack_rows_t_kernel(x_ref, o_ref):
    o_ref[...] = pltpu.bitcast(x_ref[...].T.astype(BF16), jnp.uint32)


def pack_expert_tables(u, v):
    layers, n_exp, d = u.shape
    eb = PEER_EXPERT_BLOCK
    nb = n_exp // eb
    u_packed = pl.pallas_call(
        _pack_rows_kernel,
        grid=(layers, nb),
        in_specs=[pl.BlockSpec((None, eb, d), lambda l, e: (l, e, 0))],
        out_specs=pl.BlockSpec((None, eb // 2, d), lambda l, e: (l, e, 0)),
        out_shape=jax.ShapeDtypeStruct((layers, n_exp // 2, d), jnp.uint32),
        compiler_params=_params("parallel", "parallel"),
        name="pack_u",
    )(u)
    v_packed = pl.pallas_call(
        _pack_rows_t_kernel,
        grid=(layers, nb),
        in_specs=[pl.BlockSpec((None, eb, d), lambda l, e: (l, e, 0))],
        out_specs=pl.BlockSpec((None, None, d // 2, eb), lambda l, e: (l, e, 0, 0)),
        out_shape=jax.ShapeDtypeStruct((layers, nb, d // 2, eb), jnp.uint32),
        compiler_params=_params("parallel", "parallel"),
        name="pack_vt",
    )(v)
    return u_packed, v_packed


def _unpack_rows(words):
    return pltpu.bitcast(words, BF16)


def _ordered_after(x, dep):
    z = pltpu.bitcast(dep[:BF16_SUBLANES, :LANES], jnp.uint32)
    z = (z >> 16) >> 16
    zero = pltpu.bitcast(z, BF16)
    zero = jnp.tile(zero, (x.shape[0] // BF16_SUBLANES, x.shape[1] // LANES))
    return x + zero


def _peer_ffn_kernel(ht_ref, u0_ref, ub_ref, un_ref, vta_ref, vtb_ref, cnta_ref, a0a_ref, cntb_ref, a0b_ref,
                     r1_in_ref, b1_in_ref, res_ref, g_ref, b_ref,
                     o_ref, acc_ref, hida_ref, hidb_ref, r1_ref, b1_ref):
    s = pl.program_id(1)
    half = ht_ref.shape[1] // 2

    @pl.when(s == 0)
    def _():
        acc_ref[...] = jnp.zeros_like(acc_ref)
        r1_ref[...] = r1_in_ref[...]
        b1_ref[...] = b1_in_ref[...]
        hida_ref[...] = _dot(_unpack_rows(u0_ref[...]), ht_ref[...])

    chunk = PEER_CHUNK_ROWS * PEER_NKEYS
    n_chunks = PEER_EXPERT_BLOCK // chunk

    def run_block(acc, hid_ref, vt_ref, cnt_ref, a0_ref, next_u_ref, next_hid_ref):
        for c in range(n_chunks):
            span = slice(c * chunk, (c + 1) * chunk)
            w = _peer_gate_chunk(hid_ref[span, :], c * PEER_CHUNK_ROWS, cnt_ref, a0_ref, r1_ref, b1_ref)
            acc = acc + _dot(_unpack_rows(vt_ref[:, span]), w)
            if c % (n_chunks // 2) == n_chunks // 2 - 1:
                hh = c // (n_chunks // 2)
                tcols = slice(hh * half, (hh + 1) * half)
                next_hid_ref[:, tcols] = _dot(_unpack_rows(next_u_ref[...]),
                                              _ordered_after(ht_ref[:, tcols], w))
        return acc

    acc = run_block(acc_ref[...], hida_ref, vta_ref, cnta_ref, a0a_ref, ub_ref, hidb_ref)
    acc_ref[...] = run_block(acc, hidb_ref, vtb_ref, cntb_ref, a0b_ref, un_ref, hida_ref)

    @pl.when(s == pl.num_programs(1) - 1)
    def _():
        y = acc_ref[...].T
        o_ref[...] = _layer_norm_rows(DN_ALPHA * res_ref[...] + y, g_ref[...], b_ref[...])


def peer_ffn(h, ht, route, u, v_blocks, layer, g, b, tt=512):
    n, d = h.shape
    eb = PEER_EXPERT_BLOCK
    last = 2 * u.shape[1] // eb - 1
    rspec = pl.BlockSpec((PEER_HEADS, PEER_NKEYS, tt), lambda i, s: (0, 0, i))
    rowspec = lambda off: pl.BlockSpec((PEER_HEADS, eb // PEER_NKEYS, tt), lambda i, s: (0, 2 * s + off, i))
    cnt, a0, r1, b1 = route
    return pl.pallas_call(
        _peer_ffn_kernel,
        grid=(n // tt, (last + 1) // 2),
        in_specs=[pl.BlockSpec((d, tt), lambda i, s: (0, i)),
                  pl.BlockSpec((None, eb // 2, d), lambda i, s: (layer, 0, 0)),
                  pl.BlockSpec((None, eb // 2, d), lambda i, s: (layer, 2 * s + 1, 0)),
                  pl.BlockSpec((None, eb // 2, d), lambda i, s: (layer, jnp.minimum(2 * s + 2, last), 0)),
                  pl.BlockSpec((None, None, d // 2, eb), lambda i, s: (layer, 2 * s, 0, 0)),
                  pl.BlockSpec((None, None, d // 2, eb), lambda i, s: (layer, 2 * s + 1, 0, 0)),
                  rowspec(0), rowspec(0), rowspec(1), rowspec(1),
                  rspec, rspec,
                  pl.BlockSpec((tt, d), lambda i, s: (i, 0)),
                  pl.BlockSpec((1, d), lambda i, s: (0, 0)),
                  pl.BlockSpec((1, d), lambda i, s: (0, 0))],
        out_specs=pl.BlockSpec((tt, d), lambda i, s: (i, 0)),
        out_shape=jax.ShapeDtypeStruct((n, d), F32),
        scratch_shapes=[pltpu.VMEM((d, tt), F32), pltpu.VMEM((eb, tt), F32), pltpu.VMEM((eb, tt), F32),
                        pltpu.VMEM((PEER_HEADS, PEER_NKEYS, tt), BF16),
                        pltpu.VMEM((PEER_HEADS, PEER_NKEYS, tt), BF16)],
        compiler_params=_params("parallel", "arbitrary"),
        name="peer_ffn",
    )(ht, u, u, u, v_blocks, v_blocks, cnt, a0, cnt, a0, r1, b1, h, g.reshape(1, d), b.reshape(1, d))


def kernel(x, na_w_in, na_rel_bias, na_w_out, mla_w_in, mla_q_norm, mla_kv_norm, mla_w_q_up, mla_w_kv_up, mla_w_out, hg_w_in, hg_lower_bound, hg_norm, hg_w_out, peer_w_q, peer_sub_keys, peer_u, peer_v, ln_mix_g, ln_mix_b, ln_ffn_g, ln_ffn_b):
    batch, seq, d = x.shape
    lb_w = jax.nn.softmax(hg_lower_bound.astype(F32), axis=0)
    lb_all = jnp.cumsum(lb_w, axis=0) - lb_w[0:1]
    h = x.reshape(batch * seq, d)
    u_all, v_all = pack_expert_tables(peer_u, peer_v)
    for layer in range(DEPTH):
        kind = layer % N_MIXERS
        j = layer // N_MIXERS
        if kind == 0:
            qk, vt = na_project(h, na_w_in[j], batch)
            mix_in = neighborhood_attention(qk, vt, na_rel_bias[j])
            w_out = na_w_out[j]
        elif kind == 1:
            q, k, v = mla_projections(h, mla_w_in[j], mla_q_norm[j], mla_kv_norm[j],
                                      mla_w_q_up[j], mla_w_kv_up[j], seq)
            mix_in = mla_attention(q, k, v, batch, seq)
            w_out = mla_w_out[j]
        else:
            z = matmul(h, hg_w_in[j].astype(BF16), F32)
            mix_in = hgrn2_scan(z, lb_all[layer], hg_norm[j], batch, seq)
            w_out = hg_w_out[j]
        h, ht = matmul_res_ln(mix_in, w_out.astype(BF16), h, ln_mix_g[layer], ln_mix_b[layer])
        route = peer_route(ht, peer_w_q[layer].T.astype(BF16), peer_sub_keys[layer].astype(BF16))
        h = peer_ffn(h, ht, route, u_all, v_all, layer, ln_ffn_g[layer], ln_ffn_b[layer])
    return h.reshape(batch, seq, d)
```

```python
import math

import jax
import jax.numpy as jnp
import numpy as np
from jax import lax
from jax.experimental import pallas as pl
from jax.experimental.pallas import tpu as pltpu

F32 = jnp.float32
BF16 = jnp.bfloat16

D_MODEL = 1024
DEPTH = 4
GRID_W = 64
N_MIXERS = 3

NA_HEADS = 16
NA_HEAD_DIM = 64
NA_KH = 8
NA_KW = 16

MLA_HEADS = 16
MLA_NOPE = 64
MLA_ROPE = 32
MLA_V = 64
MLA_Q_RANK = 256
MLA_KV_RANK = 256
ROPE_THETA = 10000.0

HG_HEADS = 8
HG_F = 128
HG_CHUNK = 128

PEER_HEADS = 8
PEER_NKEYS = 128
PEER_TOPK = 16
PEER_EXPERT_BLOCK = 1024
PEER_CHUNK_ROWS = 2
NORM_EPS = 1e-5
DN_ALPHA = (2.0 * DEPTH) ** 0.25

LANES = 128
F32_SUBLANES = 8
BF16_SUBLANES = 16
VMEM_LIMIT = 48 * 1024 * 1024
MASK_NEG = -1e30
RSQRT2 = 0.7071067811865476


def _params(*sem):
    return pltpu.CompilerParams(dimension_semantics=sem, vmem_limit_bytes=VMEM_LIMIT)


def _layer_norm_rows(z, g, b):
    mu = jnp.mean(z, axis=-1, keepdims=True)
    zc = z - mu
    var = jnp.mean(zc * zc, axis=-1, keepdims=True)
    return zc * lax.rsqrt(var + NORM_EPS) * g + b


def _dot(a, b):
    return jnp.dot(a, b, preferred_element_type=F32)


def _dot_nt(a, b):
    return lax.dot_general(a, b, (((1,), (1,)), ((), ())), preferred_element_type=F32)


def _dot_tn(a, b):
    return lax.dot_general(a, b, (((0,), (0,)), ((), ())), preferred_element_type=F32)


def _mm_kernel(a_ref, w_ref, o_ref):
    o_ref[...] = _dot(a_ref[...].astype(BF16), w_ref[...]).astype(o_ref.dtype)


def matmul(a, w, out_dtype, tm=512, tn=1024):
    m, k = a.shape
    n = w.shape[1]
    tn = min(tn, n)
    return pl.pallas_call(
        _mm_kernel,
        grid=(m // tm, n // tn),
        in_specs=[pl.BlockSpec((tm, k), lambda i, j: (i, 0)),
                  pl.BlockSpec((k, tn), lambda i, j: (0, j))],
        out_specs=pl.BlockSpec((tm, tn), lambda i, j: (i, j)),
        out_shape=jax.ShapeDtypeStruct((m, n), out_dtype),
        compiler_params=_params("parallel", "parallel"),
        name="proj",
    )(a, w)


def _mm_res_ln_kernel(a_ref, w_ref, res_ref, g_ref, b_ref, o_ref, ot_ref):
    y = _dot(a_ref[...], w_ref[...])
    out = _layer_norm_rows(DN_ALPHA * res_ref[...] + y, g_ref[...], b_ref[...])
    o_ref[...] = out
    ot_ref[...] = out.T.astype(BF16)


def matmul_res_ln(a, w, res, g, b, tm=512):
    m, k = a.shape
    n = w.shape[1]
    return pl.pallas_call(
        _mm_res_ln_kernel,
        grid=(m // tm,),
        in_specs=[pl.BlockSpec((tm, k), lambda i: (i, 0)),
                  pl.BlockSpec((k, n), lambda i: (0, 0)),
                  pl.BlockSpec((tm, n), lambda i: (i, 0)),
                  pl.BlockSpec((1, n), lambda i: (0, 0)),
                  pl.BlockSpec((1, n), lambda i: (0, 0))],
        out_specs=[pl.BlockSpec((tm, n), lambda i: (i, 0)),
                   pl.BlockSpec((n, tm), lambda i: (0, i))],
        out_shape=[jax.ShapeDtypeStruct((m, n), F32),
                   jax.ShapeDtypeStruct((n, m), BF16)],
        compiler_params=_params("parallel"),
        name="out_proj_ln",
    )(a, w, res, g.reshape(1, n), b.reshape(1, n))


def _na_proj_kernel(h_ref, wqk_ref, wvt_ref, qk_ref, vt_ref):
    hb = h_ref[...].astype(BF16)
    qk_ref[...] = _dot(hb, wqk_ref[...]).astype(qk_ref.dtype).reshape(qk_ref.shape)
    vt_ref[...] = _dot_nt(wvt_ref[...], hb).astype(vt_ref.dtype)


def na_project(h, w_in, batch, tm=512):
    n, d = h.shape
    rows = n // (batch * GRID_W)
    tile_rows = tm // GRID_W
    tiles = rows // tile_rows
    wqk = w_in[:, :2 * d].astype(BF16)
    wvt = w_in[:, 2 * d:].T.astype(BF16)
    return pl.pallas_call(
        _na_proj_kernel,
        grid=(n // tm,),
        in_specs=[pl.BlockSpec((tm, d), lambda i: (i, 0)),
                  pl.BlockSpec((d, 2 * d), lambda i: (0, 0)),
                  pl.BlockSpec((d, d), lambda i: (0, 0))],
        out_specs=[pl.BlockSpec((None, tile_rows, GRID_W, 2 * d), lambda i: (i // tiles, i % tiles, 0, 0)),
                   pl.BlockSpec((d, tm), lambda i: (0, i))],
        out_shape=[jax.ShapeDtypeStruct((batch, rows, GRID_W, 2 * d), BF16),
                   jax.ShapeDtypeStruct((d, n), BF16)],
        compiler_params=_params("parallel"),
        name="na_proj",
    )(h, wqk, wvt)


NA_QROWS = 2
NA_KROWS = NA_KH + NA_QROWS
NA_EDGE = NA_KH // 4


def _na_key_start(i, rows):
    return jnp.clip(i - NA_KH // 4, 0, (rows - NA_KROWS) // 2)


def _na_kernel(q_ref, k_ref, vt_ref, bias_ref, o_ref):
    n_keys = NA_KROWS * GRID_W
    n_q = NA_QROWS * GRID_W
    lane = lax.broadcasted_iota(jnp.int32, (n_q, LANES), 1)
    low = lane < NA_HEAD_DIM
    row_low = lax.broadcasted_iota(jnp.int32, (LANES, n_q), 0) < NA_HEAD_DIM
    for p in range(NA_HEADS // 2):
        cols = slice(p * LANES, (p + 1) * LANES)
        qp = q_ref[0, :, :, cols].reshape(n_q, LANES)
        zq = jnp.zeros_like(qp)
        kp = k_ref[0, :, :, cols].reshape(n_keys, LANES)
        vtp = vt_ref[cols, :]
        outs = []
        for half in range(2):
            qh = jnp.where(low, qp, zq) if half == 0 else jnp.where(low, zq, qp)
            s = _dot_nt(kp, qh) * (NA_HEAD_DIM ** -0.5) + bias_ref[2 * p + half]
            m = jnp.max(s, axis=0, keepdims=True)
            e = jnp.exp(s - m)
            l = jnp.sum(e, axis=0, keepdims=True)
            outs.append(_dot(vtp, e.astype(BF16)) / l)
        o2 = jnp.where(row_low, outs[0], outs[1]).T.astype(o_ref.dtype)
        o_ref[:, cols] = o2


def _na_bias_table(rel_bias, rows):
    cols = np.arange(GRID_W)
    c0 = np.clip(cols - NA_KW // 2, 0, GRID_W - NA_KW)
    kc = np.arange(GRID_W)
    inside = (kc[None, :] >= c0[:, None]) & (kc[None, :] < c0[:, None] + NA_KW)
    dc = np.clip(kc[None, :] - cols[:, None] + (NA_KW - 1), 0, 2 * NA_KW - 2)
    b2 = jnp.where(inside[None, None], rel_bias[:, :, dc].astype(F32), MASK_NEG)
    steps = rows // NA_QROWS
    variant_steps = list(range(NA_EDGE)) + [NA_EDGE] + list(range(steps - NA_EDGE, steps))
    dr = np.zeros((len(variant_steps), NA_QROWS, NA_KROWS), np.int32)
    ok = np.zeros(dr.shape, bool)
    for v, i in enumerate(variant_steps):
        key0 = 2 * int(np.clip(i - NA_KH // 4, 0, (rows - NA_KROWS) // 2))
        for a in range(NA_QROWS):
            r = NA_QROWS * i + a
            r0 = int(np.clip(r - NA_KH // 2, 0, rows - NA_KH))
            key_rows = key0 + np.arange(NA_KROWS)
            ok[v, a] = (key_rows >= r0) & (key_rows < r0 + NA_KH)
            dr[v, a] = np.clip(key_rows - r + NA_KH - 1, 0, 2 * NA_KH - 2)
    b2t = jnp.swapaxes(b2, -1, -2)
    per_row = []
    for a in range(NA_QROWS):
        t = b2t[:, dr[:, a]]
        per_row.append(jnp.where(ok[None, :, a, :, None, None], t, MASK_NEG))
    t = jnp.concatenate(per_row, axis=-1)
    return t.reshape(NA_HEADS, len(variant_steps), NA_KROWS * GRID_W, NA_QROWS * GRID_W)


def neighborhood_attention(qk4, vt, rel_bias):
    d = D_MODEL
    batch, rows = qk4.shape[:2]
    steps = rows // NA_QROWS
    n_keys = NA_KROWS * GRID_W
    bias = _na_bias_table(rel_bias, rows)
    el = pl.Element
    key0 = lambda i: _na_key_start(i, rows)

    def variant(i):
        return jnp.where(i < NA_EDGE, i, jnp.where(i >= steps - NA_EDGE, i - (steps - 2 * NA_EDGE - 1), NA_EDGE))

    return pl.pallas_call(
        _na_kernel,
        grid=(batch, steps),
        in_specs=[pl.BlockSpec((el(1), el(NA_QROWS), el(GRID_W), el(d)),
                               lambda b, i: (b, NA_QROWS * i, 0, 0)),
                  pl.BlockSpec((el(1), el(NA_KROWS), el(GRID_W), el(d)),
                               lambda b, i: (b, 2 * key0(i), 0, d)),
                  pl.BlockSpec((el(d), el(n_keys)),
                               lambda b, i: (0, (b * (rows // 2) + key0(i)) * (2 * GRID_W))),
                  pl.BlockSpec((NA_HEADS, None, n_keys, NA_QROWS * GRID_W),
                               lambda b, i: (0, variant(i), 0, 0))],
        out_specs=pl.BlockSpec((NA_QROWS * GRID_W, d), lambda b, i: (b * steps + i, 0)),
        out_shape=jax.ShapeDtypeStruct((batch * rows * GRID_W, d), BF16),
        compiler_params=_params("parallel", "arbitrary"),
        name="na_attn",
    )(qk4, qk4, vt, bias)


def _rms_rows(x, g):
    return x * lax.rsqrt(jnp.mean(x * x, axis=-1, keepdims=True) + NORM_EPS) * g


def _mla_proj_kernel(x_ref, win_ref, qn_ref, kvn_ref, wqa_ref, wqb_ref, wkv_ref, cos_ref, sin_ref,
                     q_ref, k_ref, v_ref):
    hd = MLA_HEADS * LANES
    hin = _dot(x_ref[...].astype(BF16), win_ref[...])
    cq = _rms_rows(hin[:, :MLA_Q_RANK], qn_ref[...]).astype(BF16)
    ckv = _rms_rows(hin[:, MLA_Q_RANK:MLA_Q_RANK + MLA_KV_RANK], kvn_ref[...]).astype(BF16)
    cos = cos_ref[...]
    sin = sin_ref[...]
    cos_t = jnp.tile(cos, (1, MLA_HEADS))
    sin_t = jnp.tile(sin, (1, MLA_HEADS))
    q = _dot(cq, wqa_ref[...]) * cos_t + _dot(cq, wqb_ref[...]) * sin_t
    q_ref[...] = (q * ((MLA_NOPE + MLA_ROPE) ** -0.5)).astype(q_ref.dtype)
    kv = _dot(ckv, wkv_ref[...])
    base = MLA_Q_RANK + MLA_KV_RANK
    kpe = hin[:, base:base + LANES] * cos + hin[:, base + LANES:base + 2 * LANES] * sin
    k_ref[...] = (kv[:, :hd] + jnp.tile(kpe, (1, MLA_HEADS))).astype(k_ref.dtype)
    v_ref[...] = kv[:, hd:].astype(v_ref.dtype)


def _mla_weights(w_in, w_q_up, w_kv_up):
    r = MLA_ROPE
    half = r // 2
    dq = MLA_NOPE + r
    nh = MLA_HEADS
    base = MLA_Q_RANK + MLA_KV_RANK
    kpe = w_in[:, base:base + r]
    zpad = lambda w, lo, hi: jnp.pad(w, ((0, 0), (lo, hi)))
    swap = lambda w: jnp.concatenate([-w[..., half:], w[..., :half]], axis=-1)
    kpe_a = zpad(kpe, MLA_NOPE, LANES - MLA_NOPE - r)
    kpe_b = zpad(swap(kpe), MLA_NOPE, LANES - MLA_NOPE - r)
    win = jnp.concatenate([w_in[:, :base], kpe_a, kpe_b], axis=1)
    wq = w_q_up.reshape(MLA_Q_RANK, nh, dq)
    pad3 = lambda w, lo, hi: jnp.pad(w, ((0, 0), (0, 0), (lo, hi)))
    wqa = pad3(wq, 0, LANES - dq).reshape(MLA_Q_RANK, nh * LANES)
    wqb = pad3(swap(wq[:, :, MLA_NOPE:]), MLA_NOPE, LANES - dq).reshape(MLA_Q_RANK, nh * LANES)
    wkv = w_kv_up.reshape(MLA_KV_RANK, nh, MLA_NOPE + MLA_V)
    wk = pad3(wkv[:, :, :MLA_NOPE], 0, LANES - MLA_NOPE).reshape(MLA_KV_RANK, nh * LANES)
    wv = wkv[:, :, MLA_NOPE:].reshape(MLA_KV_RANK, nh * MLA_V)
    return (win.astype(BF16), wqa.astype(BF16), wqb.astype(BF16),
            jnp.concatenate([wk, wv], axis=1).astype(BF16))


def _rope_tables(seq):
    half = MLA_ROPE // 2
    inv_freq = ROPE_THETA ** (-jnp.arange(half, dtype=F32) * 2.0 / MLA_ROPE)
    ang = jnp.arange(seq, dtype=F32)[:, None] * inv_freq[None, :]
    cos = jnp.cos(ang)
    sin = jnp.sin(ang)
    tail = LANES - MLA_NOPE - MLA_ROPE
    cos_p = jnp.concatenate([jnp.ones((seq, MLA_NOPE), F32), cos, cos, jnp.zeros((seq, tail), F32)], axis=1)
    sin_p = jnp.concatenate([jnp.zeros((seq, MLA_NOPE), F32), sin, sin, jnp.zeros((seq, tail), F32)], axis=1)
    return cos_p, sin_p


def mla_projections(h, w_in, q_norm, kv_norm, w_q_up, w_kv_up, seq, tm=256):
    n, d = h.shape
    win, wqa, wqb, wkv = _mla_weights(w_in, w_q_up, w_kv_up)
    cos_p, sin_p = _rope_tables(seq)
    hd = MLA_HEADS * LANES
    per_seq = seq // tm
    full = lambda a: pl.BlockSpec(a.shape, lambda i: (0,) * a.ndim)
    qn = q_norm.reshape(1, -1).astype(F32)
    kvn = kv_norm.reshape(1, -1).astype(F32)
    return pl.pallas_call(
        _mla_proj_kernel,
        grid=(n // tm,),
        in_specs=[pl.BlockSpec((tm, d), lambda i: (i, 0)), full(win), full(qn), full(kvn),
                  full(wqa), full(wqb), full(wkv),
                  pl.BlockSpec((tm, LANES), lambda i: (i % per_seq, 0)),
                  pl.BlockSpec((tm, LANES), lambda i: (i % per_seq, 0))],
        out_specs=[pl.BlockSpec((tm, hd), lambda i: (i, 0)),
                   pl.BlockSpec((tm, hd), lambda i: (i, 0)),
                   pl.BlockSpec((tm, MLA_HEADS * MLA_V), lambda i: (i, 0))],
        out_shape=[jax.ShapeDtypeStruct((n, hd), BF16),
                   jax.ShapeDtypeStruct((n, hd), BF16),
                   jax.ShapeDtypeStruct((n, MLA_HEADS * MLA_V), BF16)],
        compiler_params=_params("parallel"),
        name="mla_proj",
    )(h, win, qn, kvn, wqa, wqb, wkv, cos_p, sin_p)


def _mla_attn_kernel(q_ref, k_ref, v_ref, o_ref):
    v = v_ref[...]
    lane = lax.broadcasted_iota(jnp.int32, o_ref.shape, 1)
    halves = []
    for half in range(2):
        cols = slice(half * LANES, (half + 1) * LANES)
        s = _dot_nt(q_ref[:, cols], k_ref[:, cols])
        m = jnp.max(s, axis=-1, keepdims=True)
        e = jnp.exp(s - m)
        l = jnp.sum(e, axis=-1, keepdims=True)
        halves.append(_dot(e.astype(BF16), v) / l)
    o_ref[...] = jnp.where(lane < MLA_V, halves[0], halves[1]).astype(o_ref.dtype)


def mla_attention(q, k, v, batch, seq, tq=256):
    hd = MLA_HEADS * LANES
    q3 = q.reshape(batch, seq, hd)
    k3 = k.reshape(batch, seq, hd)
    v3 = v.reshape(batch, seq, MLA_HEADS * MLA_V)
    out = pl.pallas_call(
        _mla_attn_kernel,
        grid=(batch, MLA_HEADS // 2, seq // tq),
        in_specs=[pl.BlockSpec((None, tq, 2 * LANES), lambda b, p, i: (b, i, p)),
                  pl.BlockSpec((None, seq, 2 * LANES), lambda b, p, i: (b, 0, p)),
                  pl.BlockSpec((None, seq, LANES), lambda b, p, i: (b, 0, p))],
        out_specs=pl.BlockSpec((None, tq, LANES), lambda b, p, i: (b, i, p)),
        out_shape=jax.ShapeDtypeStruct((batch, seq, MLA_HEADS * MLA_V), BF16),
        compiler_params=_params("parallel", "parallel", "arbitrary"),
        name="mla_attn",
    )(q3, k3, v3)
    return out.reshape(batch * seq, MLA_HEADS * MLA_V)


HG_LEVELS = int(math.log2(HG_CHUNK))


def _hg_constants(reverse):
    c = HG_CHUNK
    t = np.arange(c)[:, None]
    u = np.arange(c)[None, :]
    if not reverse:
        incl = u <= t
        rest = u > t
    else:
        incl = u >= t
        rest = u < t
    mats = [incl, rest]
    masks = []
    roles = []
    for lvl in range(1, HG_LEVELS + 1):
        size = 1 << lvl
        start = (t // size) * size
        mid = start + size // 2
        upper = t >= mid
        if not reverse:
            q_side = (u >= mid) & (u <= t)
            k_side = (u > t) & (u <= mid - 1)
            is_query = upper
        else:
            q_side = (u >= t) & (u < mid)
            k_side = (u >= mid) & (u < t)
            is_query = ~upper
        mats.append(np.where(is_query, q_side, k_side))
        same = (t // size) == (u // size)
        key_row = (~is_query).T
        masks.append(same & is_query & np.broadcast_to(key_row, (c, c)))
        roles.append(np.broadcast_to(is_query, (c, LANES)))
    w = np.concatenate(mats, axis=0).astype(np.float32)
    return (jnp.asarray(w, BF16), jnp.asarray(np.stack(masks).astype(np.float32)),
            jnp.asarray(np.stack(roles).astype(np.float32)))


def _hg_chunk(q, zf, v_b, lb, w_ref, m_ref, r_ref, st, total_row):
    c = HG_CHUNK
    sg = jax.nn.sigmoid(zf)
    k = (1.0 - lb) * jax.nn.sigmoid(-zf)
    lf = jnp.log(lb + (1.0 - lb) * sg)
    hi = lf.astype(BF16)
    lo = (lf - hi.astype(F32)).astype(BF16)
    f = lf.shape[1]
    ex2 = _dot(w_ref[...], jnp.concatenate([hi, lo], axis=1))
    ex = ex2[:, :f] + ex2[:, f:]
    b_incl = ex[0:c]
    total = ex[total_row:total_row + 1]
    qd = q * jnp.exp(b_incl)
    kd = k * jnp.exp(ex[c:2 * c])
    row = lax.broadcasted_iota(jnp.int32, (c, c), 0)
    col = lax.broadcasted_iota(jnp.int32, (c, c), 1)
    a = jnp.where(row == col, jnp.sum(q * k, axis=-1, keepdims=True), 0.0)
    xs = []
    for lvl in range(HG_LEVELS):
        x = jnp.where(r_ref[lvl] > 0.5, q, k) * jnp.exp(ex[(2 + lvl) * c:(3 + lvl) * c])
        xs.append(x.astype(BF16))
    for lvl in range(0, HG_LEVELS, 2):
        pair = xs[lvl:lvl + 2]
        xb = jnp.concatenate(pair, axis=0)
        g = _dot_nt(xb, xb)
        for j in range(len(pair)):
            a = a + m_ref[lvl + j] * g[j * c:(j + 1) * c, j * c:(j + 1) * c]
    o = _dot(a.astype(BF16), v_b) + _dot_nt(qd.astype(BF16), st.astype(BF16))
    return o, st * jnp.exp(total) + _dot_tn(v_b, kd.astype(BF16))


def _hg_kernel(zq_ref, zff_ref, zfb_ref, zi_ref, zg_ref, lb_ref, g_ref,
               wf_ref, mf_ref, rf_ref, wb_ref, mb_ref, rb_ref,
               o_ref, accf_ref, accb_ref):
    c = HG_CHUNK
    seq = zq_ref.shape[0]
    n = seq // c
    lb_f = lb_ref[0:1, :]
    lb_b = lb_ref[1:2, :]

    def gated(ref, rows):
        z = ref[rows, :]
        return z * jax.nn.sigmoid(z)

    def body(i, carry):
        st_f, st_b = carry
        rf = pl.ds(pl.multiple_of(i * c, c), c)
        rb = pl.ds(pl.multiple_of((n - 1 - i) * c, c), c)
        of, st_f = _hg_chunk(gated(zq_ref, rf), zff_ref[rf, :], zi_ref[rf, :].astype(BF16), lb_f,
                             wf_ref, mf_ref, rf_ref, st_f, c - 1)
        accf_ref[rf, :] = of
        ob, st_b = _hg_chunk(gated(zq_ref, rb), zfb_ref[rb, :], zi_ref[rb, :].astype(BF16), lb_b,
                             wb_ref, mb_ref, rb_ref, st_b, 0)
        accb_ref[rb, :] = ob
        return st_f, st_b

    zero = jnp.zeros((HG_F, HG_F), F32)
    lax.fori_loop(0, n, body, (zero, zero))

    def finish(i, carry):
        rows = pl.ds(pl.multiple_of(i * c, c), c)
        o = accf_ref[rows, :] + accb_ref[rows, :]
        o = o * lax.rsqrt(jnp.mean(o * o, axis=-1, keepdims=True) + NORM_EPS)
        o_ref[rows, :] = (o * g_ref[...] * gated(zg_ref, rows)).astype(o_ref.dtype)
        return carry

    lax.fori_loop(0, n, finish, 0)


def hgrn2_scan(z, lb, norm_g, batch, seq):
    d = D_MODEL
    z3 = z.reshape(batch, seq, 5 * d)
    wf, mf, rf = _hg_constants(False)
    wb, mb, rb = _hg_constants(True)
    zspec = lambda j: pl.BlockSpec((None, seq, HG_F), lambda b, h: (b, 0, j * HG_HEADS + h))
    full = lambda a: pl.BlockSpec(a.shape, lambda b, h: (0,) * a.ndim)
    out = pl.pallas_call(
        _hg_kernel,
        grid=(batch, HG_HEADS),
        in_specs=[zspec(0), zspec(1), zspec(2), zspec(3), zspec(4),
                  pl.BlockSpec((2, HG_F), lambda b, h: (0, h)),
                  pl.BlockSpec((1, HG_F), lambda b, h: (0, h)),
                  full(wf), full(mf), full(rf), full(wb), full(mb), full(rb)],
        out_specs=pl.BlockSpec((None, seq, HG_F), lambda b, h: (b, 0, h)),
        out_shape=jax.ShapeDtypeStruct((batch, seq, d), BF16),
        scratch_shapes=[pltpu.VMEM((seq, HG_F), F32), pltpu.VMEM((seq, HG_F), F32)],
        compiler_params=_params("parallel", "parallel"),
        name="hgrn2_scan",
    )(z3, z3, z3, z3, z3, lb.astype(F32), norm_g.reshape(1, d).astype(F32),
      wf, mf, rf, wb, mb, rb)
    return out.reshape(batch * seq, d)


def _merge_sort_network(n):
    pairs = []
    p = 1
    while p < n:
        k = p
        while k >= 1:
            for j in range(k % p, n - k, 2 * k):
                for i in range(min(k, n - j - k)):
                    if (i + j) // (2 * p) == (i + j + k) // (2 * p):
                        pairs.append((i + j, i + j + k))
            k //= 2
        p *= 2
    return pairs


def _top_values(s, count):
    rows = s.shape[0]
    n = rows // F32_SUBLANES
    size = 1 << (n - 1).bit_length()
    neg = jnp.full((F32_SUBLANES, s.shape[1]), -jnp.inf, F32)
    lists = [s[i * F32_SUBLANES:(i + 1) * F32_SUBLANES] for i in range(n)] + [neg] * (size - n)
    for i, j in _merge_sort_network(size):
        hi = jnp.maximum(lists[i], lists[j])
        lists[j] = jnp.minimum(lists[i], lists[j])
        lists[i] = hi
    lists = lists[:min(size, count)]
    vals = []
    for it in range(count):
        head = lists[0]
        m = jnp.max(head, axis=0, keepdims=True)
        vals.append(m)
        hit = head >= m
        depth = min(len(lists), count - it - 1)
        for k in range(depth):
            nxt = lists[k + 1] if k + 1 < len(lists) else neg
            lists[k] = jnp.where(hit, nxt, lists[k])
        lists = lists[:max(depth, 1)]
    return jnp.concatenate(vals, axis=0)


def _peer_route_kernel(ht_ref, wq_ref, keys_ref, cnt_ref, a0_ref, r1_ref, b1_ref):
    kk = PEER_TOPK
    ht = ht_ref[...]
    for h in range(PEER_HEADS):
        scores = []
        for c in range(2):
            g = 2 * h + c
            qt = _dot(wq_ref[g * LANES:(g + 1) * LANES, :], ht)
            scores.append(_dot(keys_ref[c], qt.astype(BF16)))
        s0, s1 = scores
        sv0 = _top_values(s0, kk)
        sv1 = _top_values(s1, kk)
        r1 = jnp.full(s1.shape, float(kk), F32)
        for b in range(kk):
            r1 = jnp.where(s1 == sv1[b:b + 1], float(b), r1)
        cands = [sv0[a:a + 1] + sv1[0:kk // (a + 1)] for a in range(kk)]
        n_cand = sum(kk // (a + 1) for a in range(kk))
        pad = (-n_cand) % 8
        if pad:
            cands.append(jnp.full((pad, ht.shape[1]), -jnp.inf, F32))
        tau = _top_values(jnp.concatenate(cands, axis=0), kk)[kk - 1:kk]
        e0 = jnp.exp(sv0 - sv0[0:1])
        e1 = jnp.exp(sv1 - sv1[0:1])
        z = jnp.zeros_like(tau)
        for a in range(kk):
            nb = kk // (a + 1)
            sel = (sv0[a:a + 1] + sv1[0:nb]) >= tau
            z = z + e0[a:a + 1] * jnp.sum(jnp.where(sel, e1[0:nb], 0.0), axis=0, keepdims=True)
        cnt = jnp.zeros(s0.shape, F32)
        for a in range(kk):
            pair_ok = (sv0[a:a + 1] + sv1) >= tau
            n_a = jnp.sum(jnp.where(pair_ok, 1.0, 0.0), axis=0, keepdims=True)
            cnt = jnp.where(s0 == sv0[a:a + 1], n_a, cnt)
        cnt_ref[h] = cnt
        a0_ref[h] = jnp.exp(s0 - sv0[0:1])
        r1_ref[h] = r1.astype(BF16)
        b1_ref[h] = (jnp.exp(s1 - sv1[0:1]) * (0.5 / z)).astype(BF16)


def peer_route(ht, wq_t, keys, tt=1024):
    d, n = ht.shape
    shape = (PEER_HEADS, PEER_NKEYS, n)
    ospec = pl.BlockSpec((PEER_HEADS, PEER_NKEYS, tt), lambda i: (0, 0, i))
    return pl.pallas_call(
        _peer_route_kernel,
        grid=(n // tt,),
        in_specs=[pl.BlockSpec((d, tt), lambda i: (0, i)),
                  pl.BlockSpec(wq_t.shape, lambda i: (0, 0)),
                  pl.BlockSpec(keys.shape, lambda i: (0, 0, 0))],
        out_specs=[ospec, ospec, ospec, ospec],
        out_shape=[jax.ShapeDtypeStruct(shape, F32), jax.ShapeDtypeStruct(shape, F32),
                   jax.ShapeDtypeStruct(shape, BF16), jax.ShapeDtypeStruct(shape, BF16)],
        compiler_params=_params("parallel"),
        name="peer_route",
    )(ht, wq_t, keys)


def _peer_gate_chunk(hid, i0, cnt_ref, a0_ref, r1_ref, b1_ref):
    tt = hid.shape[1]
    zero = jnp.zeros((PEER_NKEYS, LANES), BF16)

    def row_tile(ref, h, ii, cols):
        row = jnp.broadcast_to(ref[h, ii:ii + 1, cols], (BF16_SUBLANES, LANES)).astype(BF16)
        return jnp.tile(row, (PEER_NKEYS // BF16_SUBLANES, 1))

    out_rows = []
    for k in range(PEER_CHUNK_ROWS):
        tiles = []
        for tc in range(tt // LANES):
            cols = slice(tc * LANES, (tc + 1) * LANES)
            gate = zero
            for h in range(PEER_HEADS):
                cnt = row_tile(cnt_ref, h, i0 + k, cols)
                a0 = row_tile(a0_ref, h, i0 + k, cols)
                gate = gate + jnp.where(r1_ref[h, :, cols] < cnt, a0 * b1_ref[h, :, cols], zero)
            x = hid[k * PEER_NKEYS:(k + 1) * PEER_NKEYS, cols]
            act = x.astype(BF16) * (1.0 + lax.erf(x * RSQRT2)).astype(BF16)
            tiles.append(gate * act)
        out_rows.append(jnp.concatenate(tiles, axis=1))
    return jnp.concatenate(out_rows, axis=0)


def _pack_rows_kernel(x_ref, o_ref):
    o_ref[...] = pltpu.bitcast(x_ref[...].astype(BF16), jnp.uint32)


def _pack_rows_t_kernel(x_ref, o_ref):
    o_ref[...] = pltpu.bitcast(x_ref[...].T.astype(BF16), jnp.uint32)


def pack_expert_tables(u, v):
    layers, n_exp, d = u.shape
    eb = PEER_EXPERT_BLOCK
    nb = n_exp // eb
    u_packed = pl.pallas_call(
        _pack_rows_kernel,
        grid=(layers, nb),
        in_specs=[pl.BlockSpec((None, eb, d), lambda l, e: (l, e, 0))],
        out_specs=pl.BlockSpec((None, eb // 2, d), lambda l, e: (l, e, 0)),
        out_shape=jax.ShapeDtypeStruct((layers, n_exp // 2, d), jnp.uint32),
        compiler_params=_params("parallel", "parallel"),
        name="pack_u",
    )(u)
    v_packed = pl.pallas_call(
        _pack_rows_t_kernel,
        grid=(layers, nb),
        in_specs=[pl.BlockSpec((None, eb, d), lambda l, e: (l, e, 0))],
        out_specs=pl.BlockSpec((None, None, d // 2, eb), lambda l, e: (l, e, 0, 0)),
        out_shape=jax.ShapeDtypeStruct((layers, nb, d // 2, eb), jnp.uint32),
        compiler_params=_params("parallel", "parallel"),
        name="pack_vt",
    )(v)
    return u_packed, v_packed


def _unpack_rows(words):
    return pltpu.bitcast(words, BF16)


def _ordered_after(x, dep):
    z = pltpu.bitcast(dep[:BF16_SUBLANES, :LANES], jnp.uint32)
    z = (z >> 16) >> 16
    zero = pltpu.bitcast(z, BF16)
    zero = jnp.tile(zero, (x.shape[0] // BF16_SUBLANES, x.shape[1] // LANES))
    return x + zero


def _peer_ffn_kernel(ht_ref, u0_ref, ub_ref, un_ref, vta_ref, vtb_ref, cnta_ref, a0a_ref, cntb_ref, a0b_ref,
                     r1_in_ref, b1_in_ref, res_ref, g_ref, b_ref,
                     o_ref, acc_ref, hida_ref, hidb_ref, r1_ref, b1_ref):
    s = pl.program_id(1)
    half = ht_ref.shape[1] // 2

    @pl.when(s == 0)
    def _():
        acc_ref[...] = jnp.zeros_like(acc_ref)
        r1_ref[...] = r1_in_ref[...]
        b1_ref[...] = b1_in_ref[...]
        hida_ref[...] = _dot(_unpack_rows(u0_ref[...]), ht_ref[...])

    chunk = PEER_CHUNK_ROWS * PEER_NKEYS
    n_chunks = PEER_EXPERT_BLOCK // chunk

    def run_block(acc, hid_ref, vt_ref, cnt_ref, a0_ref, next_u_ref, next_hid_ref):
        for c in range(n_chunks):
            span = slice(c * chunk, (c + 1) * chunk)
            w = _peer_gate_chunk(hid_ref[span, :], c * PEER_CHUNK_ROWS, cnt_ref, a0_ref, r1_ref, b1_ref)
            acc = acc + _dot(_unpack_rows(vt_ref[:, span]), w)
            if c % (n_chunks // 2) == n_chunks // 2 - 1:
                hh = c // (n_chunks // 2)
                tcols = slice(hh * half, (hh + 1) * half)
                next_hid_ref[:, tcols] = _dot(_unpack_rows(next_u_ref[...]),
                                              _ordered_after(ht_ref[:, tcols], w))
        return acc

    acc = run_block(acc_ref[...], hida_ref, vta_ref, cnta_ref, a0a_ref, ub_ref, hidb_ref)
    acc_ref[...] = run_block(acc, hidb_ref, vtb_ref, cntb_ref, a0b_ref, un_ref, hida_ref)

    @pl.when(s == pl.num_programs(1) - 1)
    def _():
        y = acc_ref[...].T
        o_ref[...] = _layer_norm_rows(DN_ALPHA * res_ref[...] + y, g_ref[...], b_ref[...])


def peer_ffn(h, ht, route, u, v_blocks, layer, g, b, tt=512):
    n, d = h.shape
    eb = PEER_EXPERT_BLOCK
    last = 2 * u.shape[1] // eb - 1
    rspec = pl.BlockSpec((PEER_HEADS, PEER_NKEYS, tt), lambda i, s: (0, 0, i))
    rowspec = lambda off: pl.BlockSpec((PEER_HEADS, eb // PEER_NKEYS, tt), lambda i, s: (0, 2 * s + off, i))
    cnt, a0, r1, b1 = route
    return pl.pallas_call(
        _peer_ffn_kernel,
        grid=(n // tt, (last + 1) // 2),
        in_specs=[pl.BlockSpec((d, tt), lambda i, s: (0, i)),
                  pl.BlockSpec((None, eb // 2, d), lambda i, s: (layer, 0, 0)),
                  pl.BlockSpec((None, eb // 2, d), lambda i, s: (layer, 2 * s + 1, 0)),
                  pl.BlockSpec((None, eb // 2, d), lambda i, s: (layer, jnp.minimum(2 * s + 2, last), 0)),
                  pl.BlockSpec((None, None, d // 2, eb), lambda i, s: (layer, 2 * s, 0, 0)),
                  pl.BlockSpec((None, None, d // 2, eb), lambda i, s: (layer, 2 * s + 1, 0, 0)),
                  rowspec(0), rowspec(0), rowspec(1), rowspec(1),
                  rspec, rspec,
                  pl.BlockSpec((tt, d), lambda i, s: (i, 0)),
                  pl.BlockSpec((1, d), lambda i, s: (0, 0)),
                  pl.BlockSpec((1, d), lambda i, s: (0, 0))],
        out_specs=pl.BlockSpec((tt, d), lambda i, s: (i, 0)),
        out_shape=jax.ShapeDtypeStruct((n, d), F32),
        scratch_shapes=[pltpu.VMEM((d, tt), F32), pltpu.VMEM((eb, tt), F32), pltpu.VMEM((eb, tt), F32),
                        pltpu.VMEM((PEER_HEADS, PEER_NKEYS, tt), BF16),
                        pltpu.VMEM((PEER_HEADS, PEER_NKEYS, tt), BF16)],
        compiler_params=_params("parallel", "arbitrary"),
        name="peer_ffn",
    )(ht, u, u, u, v_blocks, v_blocks, cnt, a0, cnt, a0, r1, b1, h, g.reshape(1, d), b.reshape(1, d))


def kernel(x, na_w_in, na_rel_bias, na_w_out, mla_w_in, mla_q_norm, mla_kv_norm, mla_w_q_up, mla_w_kv_up, mla_w_out, hg_w_in, hg_lower_bound, hg_norm, hg_w_out, peer_w_q, peer_sub_keys, peer_u, peer_v, ln_mix_g, ln_mix_b, ln_ffn_g, ln_ffn_b):
    batch, seq, d = x.shape
    lb_w = jax.nn.softmax(hg_lower_bound.astype(F32), axis=0)
    lb_all = jnp.cumsum(lb_w, axis=0) - lb_w[0:1]
    h = x.reshape(batch * seq, d)
    u_all, v_all = pack_expert_tables(peer_u, peer_v)
    for layer in range(DEPTH):
        kind = layer % N_MIXERS
        j = layer // N_MIXERS
        if kind == 0:
            qk, vt = na_project(h, na_w_in[j], batch)
            mix_in = neighborhood_attention(qk, vt, na_rel_bias[j])
            w_out = na_w_out[j]
        elif kind == 1:
            q, k, v = mla_projections(h, mla_w_in[j], mla_q_norm[j], mla_kv_norm[j],
                                      mla_w_q_up[j], mla_w_kv_up[j], seq)
            mix_in = mla_attention(q, k, v, batch, seq)
            w_out = mla_w_out[j]
        else:
            z = matmul(h, hg_w_in[j].astype(BF16), F32)
            mix_in = hgrn2_scan(z, lb_all[layer], hg_norm[j], batch, seq)
            w_out = hg_w_out[j]
        h, ht = matmul_res_ln(mix_in, w_out.astype(BF16), h, ln_mix_g[layer], ln_mix_b[layer])
        route = peer_route(ht, peer_w_q[layer].T.astype(BF16), peer_sub_keys[layer].astype(BF16))
        h = peer_ffn(h, ht, route, u_all, v_all, layer, ln_ffn_g[layer], ln_ffn_b[layer])
    return h.reshape(batch, seq, d)
```

```python
import math

import jax
import jax.numpy as jnp
import numpy as np
from jax import lax
from jax.experimental import pallas as pl
from jax.experimental.pallas import tpu as pltpu

F32 = jnp.float32
BF16 = jnp.bfloat16

D_MODEL = 1024
DEPTH = 4
GRID_W = 64
N_MIXERS = 3

NA_HEADS = 16
NA_HEAD_DIM = 64
NA_KH = 8
NA_KW = 16

MLA_HEADS = 16
MLA_NOPE = 64
MLA_ROPE = 32
MLA_V = 64
MLA_Q_RANK = 256
MLA_KV_RANK = 256
ROPE_THETA = 10000.0

HG_HEADS = 8
HG_F = 128
HG_CHUNK = 128

PEER_HEADS = 8
PEER_NKEYS = 128
PEER_TOPK = 16
PEER_EXPERT_BLOCK = 1024
PEER_CHUNK_ROWS = 2
NORM_EPS = 1e-5
DN_ALPHA = (2.0 * DEPTH) ** 0.25

LANES = 128
F32_SUBLANES = 8
BF16_SUBLANES = 16
VMEM_LIMIT = 48 * 1024 * 1024
MASK_NEG = -1e30
RSQRT2 = 0.7071067811865476


def _params(*sem):
    return pltpu.CompilerParams(dimension_semantics=sem, vmem_limit_bytes=VMEM_LIMIT)


def _layer_norm_rows(z, g, b):
    mu = jnp.mean(z, axis=-1, keepdims=True)
    zc = z - mu
    var = jnp.mean(zc * zc, axis=-1, keepdims=True)
    return zc * lax.rsqrt(var + NORM_EPS) * g + b


def _dot(a, b):
    return jnp.dot(a, b, preferred_element_type=F32)


def _dot_nt(a, b):
    return lax.dot_general(a, b, (((1,), (1,)), ((), ())), preferred_element_type=F32)


def _dot_tn(a, b):
    return lax.dot_general(a, b, (((0,), (0,)), ((), ())), preferred_element_type=F32)


def _mm_kernel(a_ref, w_ref, o_ref):
    o_ref[...] = _dot(a_ref[...].astype(BF16), w_ref[...]).astype(o_ref.dtype)


def matmul(a, w, out_dtype, tm=512, tn=1024):
    m, k = a.shape
    n = w.shape[1]
    tn = min(tn, n)
    return pl.pallas_call(
        _mm_kernel,
        grid=(m // tm, n // tn),
        in_specs=[pl.BlockSpec((tm, k), lambda i, j: (i, 0)),
                  pl.BlockSpec((k, tn), lambda i, j: (0, j))],
        out_specs=pl.BlockSpec((tm, tn), lambda i, j: (i, j)),
        out_shape=jax.ShapeDtypeStruct((m, n), out_dtype),
        compiler_params=_params("parallel", "parallel"),
        name="proj",
    )(a, w)


def _mm_res_ln_kernel(a_ref, w_ref, res_ref, g_ref, b_ref, o_ref, ot_ref):
    y = _dot(a_ref[...], w_ref[...])
    out = _layer_norm_rows(DN_ALPHA * res_ref[...] + y, g_ref[...], b_ref[...])
    o_ref[...] = out
    ot_ref[...] = out.T.astype(BF16)


def matmul_res_ln(a, w, res, g, b, tm=512):
    m, k = a.shape
    n = w.shape[1]
    return pl.pallas_call(
        _mm_res_ln_kernel,
        grid=(m // tm,),
        in_specs=[pl.BlockSpec((tm, k), lambda i: (i, 0)),
                  pl.BlockSpec((k, n), lambda i: (0, 0)),
                  pl.BlockSpec((tm, n), lambda i: (i, 0)),
                  pl.BlockSpec((1, n), lambda i: (0, 0)),
                  pl.BlockSpec((1, n), lambda i: (0, 0))],
        out_specs=[pl.BlockSpec((tm, n), lambda i: (i, 0)),
                   pl.BlockSpec((n, tm), lambda i: (0, i))],
        out_shape=[jax.ShapeDtypeStruct((m, n), F32),
                   jax.ShapeDtypeStruct((n, m), BF16)],
        compiler_params=_params("parallel"),
        name="out_proj_ln",
    )(a, w, res, g.reshape(1, n), b.reshape(1, n))


def _na_proj_kernel(h_ref, wqk_ref, wvt_ref, qk_ref, vt_ref):
    hb = h_ref[...].astype(BF16)
    qk_ref[...] = _dot(hb, wqk_ref[...]).astype(qk_ref.dtype).reshape(qk_ref.shape)
    vt_ref[...] = _dot_nt(wvt_ref[...], hb).astype(vt_ref.dtype)


def na_project(h, w_in, batch, tm=512):
    n, d = h.shape
    rows = n // (batch * GRID_W)
    tile_rows = tm // GRID_W
    tiles = rows // tile_rows
    wqk = w_in[:, :2 * d].astype(BF16)
    wvt = w_in[:, 2 * d:].T.astype(BF16)
    return pl.pallas_call(
        _na_proj_kernel,
        grid=(n // tm,),
        in_specs=[pl.BlockSpec((tm, d), lambda i: (i, 0)),
                  pl.BlockSpec((d, 2 * d), lambda i: (0, 0)),
                  pl.BlockSpec((d, d), lambda i: (0, 0))],
        out_specs=[pl.BlockSpec((None, tile_rows, GRID_W, 2 * d), lambda i: (i // tiles, i % tiles, 0, 0)),
                   pl.BlockSpec((d, tm), lambda i: (0, i))],
        out_shape=[jax.ShapeDtypeStruct((batch, rows, GRID_W, 2 * d), BF16),
                   jax.ShapeDtypeStruct((d, n), BF16)],
        compiler_params=_params("parallel"),
        name="na_proj",
    )(h, wqk, wvt)


NA_QROWS = 2
NA_KROWS = NA_KH + NA_QROWS
NA_EDGE = NA_KH // 4


def _na_key_start(i, rows):
    return jnp.clip(i - NA_KH // 4, 0, (rows - NA_KROWS) // 2)


def _na_kernel(q_ref, k_ref, vt_ref, bias_ref, o_ref):
    n_keys = NA_KROWS * GRID_W
    n_q = NA_QROWS * GRID_W
    lane = lax.broadcasted_iota(jnp.int32, (n_q, LANES), 1)
    low = lane < NA_HEAD_DIM
    row_low = lax.broadcasted_iota(jnp.int32, (LANES, n_q), 0) < NA_HEAD_DIM
    for p in range(NA_HEADS // 2):
        cols = slice(p * LANES, (p + 1) * LANES)
        qp = q_ref[0, :, :, cols].reshape(n_q, LANES)
        zq = jnp.zeros_like(qp)
        kp = k_ref[0, :, :, cols].reshape(n_keys, LANES)
        vtp = vt_ref[cols, :]
        outs = []
        for half in range(2):
            qh = jnp.where(low, qp, zq) if half == 0 else jnp.where(low, zq, qp)
            s = _dot_nt(kp, qh) * (NA_HEAD_DIM ** -0.5) + bias_ref[2 * p + half]
            m = jnp.max(s, axis=0, keepdims=True)
            e = jnp.exp(s - m)
            l = jnp.sum(e, axis=0, keepdims=True)
            outs.append(_dot(vtp, e.astype(BF16)) / l)
        o2 = jnp.where(row_low, outs[0], outs[1]).T.astype(o_ref.dtype)
        o_ref[:, cols] = o2


def _na_bias_table(rel_bias, rows):
    cols = np.arange(GRID_W)
    c0 = np.clip(cols - NA_KW // 2, 0, GRID_W - NA_KW)
    kc = np.arange(GRID_W)
    inside = (kc[None, :] >= c0[:, None]) & (kc[None, :] < c0[:, None] + NA_KW)
    dc = np.clip(kc[None, :] - cols[:, None] + (NA_KW - 1), 0, 2 * NA_KW - 2)
    b2 = jnp.where(inside[None, None], rel_bias[:, :, dc].astype(F32), MASK_NEG)
    steps = rows // NA_QROWS
    variant_steps = list(range(NA_EDGE)) + [NA_EDGE] + list(range(steps - NA_EDGE, steps))
    dr = np.zeros((len(variant_steps), NA_QROWS, NA_KROWS), np.int32)
    ok = np.zeros(dr.shape, bool)
    for v, i in enumerate(variant_steps):
        key0 = 2 * int(np.clip(i - NA_KH // 4, 0, (rows - NA_KROWS) // 2))
        for a in range(NA_QROWS):
            r = NA_QROWS * i + a
            r0 = int(np.clip(r - NA_KH // 2, 0, rows - NA_KH))
            key_rows = key0 + np.arange(NA_KROWS)
            ok[v, a] = (key_rows >= r0) & (key_rows < r0 + NA_KH)
            dr[v, a] = np.clip(key_rows - r + NA_KH - 1, 0, 2 * NA_KH - 2)
    b2t = jnp.swapaxes(b2, -1, -2)
    per_row = []
    for a in range(NA_QROWS):
        t = b2t[:, dr[:, a]]
        per_row.append(jnp.where(ok[None, :, a, :, None, None], t, MASK_NEG))
    t = jnp.concatenate(per_row, axis=-1)
    return t.reshape(NA_HEADS, len(variant_steps), NA_KROWS * GRID_W, NA_QROWS * GRID_W)


def neighborhood_attention(qk4, vt, rel_bias):
    d = D_MODEL
    batch, rows = qk4.shape[:2]
    steps = rows // NA_QROWS
    n_keys = NA_KROWS * GRID_W
    bias = _na_bias_table(rel_bias, rows)
    el = pl.Element
    key0 = lambda i: _na_key_start(i, rows)

    def variant(i):
        return jnp.where(i < NA_EDGE, i, jnp.where(i >= steps - NA_EDGE, i - (steps - 2 * NA_EDGE - 1), NA_EDGE))

    return pl.pallas_call(
        _na_kernel,
        grid=(batch, steps),
        in_specs=[pl.BlockSpec((el(1), el(NA_QROWS), el(GRID_W), el(d)),
                               lambda b, i: (b, NA_QROWS * i, 0, 0)),
                  pl.BlockSpec((el(1), el(NA_KROWS), el(GRID_W), el(d)),
                               lambda b, i: (b, 2 * key0(i), 0, d)),
                  pl.BlockSpec((el(d), el(n_keys)),
                               lambda b, i: (0, (b * (rows // 2) + key0(i)) * (2 * GRID_W))),
                  pl.BlockSpec((NA_HEADS, None, n_keys, NA_QROWS * GRID_W),
                               lambda b, i: (0, variant(i), 0, 0))],
        out_specs=pl.BlockSpec((NA_QROWS * GRID_W, d), lambda b, i: (b * steps + i, 0)),
        out_shape=jax.ShapeDtypeStruct((batch * rows * GRID_W, d), BF16),
        compiler_params=_params("parallel", "arbitrary"),
        name="na_attn",
    )(qk4, qk4, vt, bias)


def _rms_rows(x, g):
    return x * lax.rsqrt(jnp.mean(x * x, axis=-1, keepdims=True) + NORM_EPS) * g


def _mla_proj_kernel(x_ref, win_ref, qn_ref, kvn_ref, wqa_ref, wqb_ref, wkv_ref, cos_ref, sin_ref,
                     q_ref, k_ref, v_ref):
    hd = MLA_HEADS * LANES
    hin = _dot(x_ref[...].astype(BF16), win_ref[...])
    cq = _rms_rows(hin[:, :MLA_Q_RANK], qn_ref[...]).astype(BF16)
    ckv = _rms_rows(hin[:, MLA_Q_RANK:MLA_Q_RANK + MLA_KV_RANK], kvn_ref[...]).astype(BF16)
    cos = cos_ref[...]
    sin = sin_ref[...]
    cos_t = jnp.tile(cos, (1, MLA_HEADS))
    sin_t = jnp.tile(sin, (1, MLA_HEADS))
    q = _dot(cq, wqa_ref[...]) * cos_t + _dot(cq, wqb_ref[...]) * sin_t
    q_ref[...] = (q * ((MLA_NOPE + MLA_ROPE) ** -0.5)).astype(q_ref.dtype)
    kv = _dot(ckv, wkv_ref[...])
    base = MLA_Q_RANK + MLA_KV_RANK
    kpe = hin[:, base:base + LANES] * cos + hin[:, base + LANES:base + 2 * LANES] * sin
    k_ref[...] = (kv[:, :hd] + jnp.tile(kpe, (1, MLA_HEADS))).astype(k_ref.dtype)
    v_ref[...] = kv[:, hd:].astype(v_ref.dtype)


def _mla_weights(w_in, w_q_up, w_kv_up):
    r = MLA_ROPE
    half = r // 2
    dq = MLA_NOPE + r
    nh = MLA_HEADS
    base = MLA_Q_RANK + MLA_KV_RANK
    kpe = w_in[:, base:base + r]
    zpad = lambda w, lo, hi: jnp.pad(w, ((0, 0), (lo, hi)))
    swap = lambda w: jnp.concatenate([-w[..., half:], w[..., :half]], axis=-1)
    kpe_a = zpad(kpe, MLA_NOPE, LANES - MLA_NOPE - r)
    kpe_b = zpad(swap(kpe), MLA_NOPE, LANES - MLA_NOPE - r)
    win = jnp.concatenate([w_in[:, :base], kpe_a, kpe_b], axis=1)
    wq = w_q_up.reshape(MLA_Q_RANK, nh, dq)
    pad3 = lambda w, lo, hi: jnp.pad(w, ((0, 0), (0, 0), (lo, hi)))
    wqa = pad3(wq, 0, LANES - dq).reshape(MLA_Q_RANK, nh * LANES)
    wqb = pad3(swap(wq[:, :, MLA_NOPE:]), MLA_NOPE, LANES - dq).reshape(MLA_Q_RANK, nh * LANES)
    wkv = w_kv_up.reshape(MLA_KV_RANK, nh, MLA_NOPE + MLA_V)
    wk = pad3(wkv[:, :, :MLA_NOPE], 0, LANES - MLA_NOPE).reshape(MLA_KV_RANK, nh * LANES)
    wv = wkv[:, :, MLA_NOPE:].reshape(MLA_KV_RANK, nh * MLA_V)
    return (win.astype(BF16), wqa.astype(BF16), wqb.astype(BF16),
            jnp.concatenate([wk, wv], axis=1).astype(BF16))


def _rope_tables(seq):
    half = MLA_ROPE // 2
    inv_freq = ROPE_THETA ** (-jnp.arange(half, dtype=F32) * 2.0 / MLA_ROPE)
    ang = jnp.arange(seq, dtype=F32)[:, None] * inv_freq[None, :]
    cos = jnp.cos(ang)
    sin = jnp.sin(ang)
    tail = LANES - MLA_NOPE - MLA_ROPE
    cos_p = jnp.concatenate([jnp.ones((seq, MLA_NOPE), F32), cos, cos, jnp.zeros((seq, tail), F32)], axis=1)
    sin_p = jnp.concatenate([jnp.zeros((seq, MLA_NOPE), F32), sin, sin, jnp.zeros((seq, tail), F32)], axis=1)
    return cos_p, sin_p


def mla_projections(h, w_in, q_norm, kv_norm, w_q_up, w_kv_up, seq, tm=256):
    n, d = h.shape
    win, wqa, wqb, wkv = _mla_weights(w_in, w_q_up, w_kv_up)
    cos_p, sin_p = _rope_tables(seq)
    hd = MLA_HEADS * LANES
    per_seq = seq // tm
    full = lambda a: pl.BlockSpec(a.shape, lambda i: (0,) * a.ndim)
    qn = q_norm.reshape(1, -1).astype(F32)
    kvn = kv_norm.reshape(1, -1).astype(F32)
    return pl.pallas_call(
        _mla_proj_kernel,
        grid=(n // tm,),
        in_specs=[pl.BlockSpec((tm, d), lambda i: (i, 0)), full(win), full(qn), full(kvn),
                  full(wqa), full(wqb), full(wkv),
                  pl.BlockSpec((tm, LANES), lambda i: (i % per_seq, 0)),
                  pl.BlockSpec((tm, LANES), lambda i: (i % per_seq, 0))],
        out_specs=[pl.BlockSpec((tm, hd), lambda i: (i, 0)),
                   pl.BlockSpec((tm, hd), lambda i: (i, 0)),
                   pl.BlockSpec((tm, MLA_HEADS * MLA_V), lambda i: (i, 0))],
        out_shape=[jax.ShapeDtypeStruct((n, hd), BF16),
                   jax.ShapeDtypeStruct((n, hd), BF16),
                   jax.ShapeDtypeStruct((n, MLA_HEADS * MLA_V), BF16)],
        compiler_params=_params("parallel"),
        name="mla_proj",
    )(h, win, qn, kvn, wqa, wqb, wkv, cos_p, sin_p)


def _mla_attn_kernel(q_ref, k_ref, v_ref, o_ref):
    v = v_ref[...]
    lane = lax.broadcasted_iota(jnp.int32, o_ref.shape, 1)
    halves = []
    for half in range(2):
        cols = slice(half * LANES, (half + 1) * LANES)
        s = _dot_nt(q_ref[:, cols], k_ref[:, cols])
        m = jnp.max(s, axis=-1, keepdims=True)
        e = jnp.exp(s - m)
        l = jnp.sum(e, axis=-1, keepdims=True)
        halves.append(_dot(e.astype(BF16), v) / l)
    o_ref[...] = jnp.where(lane < MLA_V, halves[0], halves[1]).astype(o_ref.dtype)


def mla_attention(q, k, v, batch, seq, tq=256):
    hd = MLA_HEADS * LANES
    q3 = q.reshape(batch, seq, hd)
    k3 = k.reshape(batch, seq, hd)
    v3 = v.reshape(batch, seq, MLA_HEADS * MLA_V)
    out = pl.pallas_call(
        _mla_attn_kernel,
        grid=(batch, MLA_HEADS // 2, seq // tq),
        in_specs=[pl.BlockSpec((None, tq, 2 * LANES), lambda b, p, i: (b, i, p)),
                  pl.BlockSpec((None, seq, 2 * LANES), lambda b, p, i: (b, 0, p)),
                  pl.BlockSpec((None, seq, LANES), lambda b, p, i: (b, 0, p))],
        out_specs=pl.BlockSpec((None, tq, LANES), lambda b, p, i: (b, i, p)),
        out_shape=jax.ShapeDtypeStruct((batch, seq, MLA_HEADS * MLA_V), BF16),
        compiler_params=_params("parallel", "parallel", "arbitrary"),
        name="mla_attn",
    )(q3, k3, v3)
    return out.reshape(batch * seq, MLA_HEADS * MLA_V)


HG_LEVELS = int(math.log2(HG_CHUNK))


def _hg_constants(reverse):
    c = HG_CHUNK
    t = np.arange(c)[:, None]
    u = np.arange(c)[None, :]
    if not reverse:
        incl = u <= t
        rest = u > t
    else:
        incl = u >= t
        rest = u < t
    mats = [incl, rest]
    masks = []
    roles = []
    for lvl in range(1, HG_LEVELS + 1):
        size = 1 << lvl
        start = (t // size) * size
        mid = start + size // 2
        upper = t >= mid
        if not reverse:
            q_side = (u >= mid) & (u <= t)
            k_side = (u > t) & (u <= mid - 1)
            is_query = upper
        else:
            q_side = (u >= t) & (u < mid)
            k_side = (u >= mid) & (u < t)
            is_query = ~upper
        mats.append(np.where(is_query, q_side, k_side))
        same = (t // size) == (u // size)
        key_row = (~is_query).T
        masks.append(same & is_query & np.broadcast_to(key_row, (c, c)))
        roles.append(np.broadcast_to(is_query, (c, LANES)))
    w = np.concatenate(mats, axis=0).astype(np.float32)
    return (jnp.asarray(w, BF16), jnp.asarray(np.stack(masks).astype(np.float32)),
            jnp.asarray(np.stack(roles).astype(np.float32)))


def _hg_chunk(q, zf, v_b, lb, w_ref, m_ref, r_ref, st, total_row):
    c = HG_CHUNK
    sg = jax.nn.sigmoid(zf)
    k = (1.0 - lb) * jax.nn.sigmoid(-zf)
    lf = jnp.log(lb + (1.0 - lb) * sg)
    hi = lf.astype(BF16)
    lo = (lf - hi.astype(F32)).astype(BF16)
    f = lf.shape[1]
    ex2 = _dot(w_ref[...], jnp.concatenate([hi, lo], axis=1))
    ex = ex2[:, :f] + ex2[:, f:]
    b_incl = ex[0:c]
    total = ex[total_row:total_row + 1]
    qd = q * jnp.exp(b_incl)
    kd = k * jnp.exp(ex[c:2 * c])
    row = lax.broadcasted_iota(jnp.int32, (c, c), 0)
    col = lax.broadcasted_iota(jnp.int32, (c, c), 1)
    a = jnp.where(row == col, jnp.sum(q * k, axis=-1, keepdims=True), 0.0)
    xs = []
    for lvl in range(HG_LEVELS):
        x = jnp.where(r_ref[lvl] > 0.5, q, k) * jnp.exp(ex[(2 + lvl) * c:(3 + lvl) * c])
        xs.append(x.astype(BF16))
    for lvl in range(0, HG_LEVELS, 2):
        pair = xs[lvl:lvl + 2]
        xb = jnp.concatenate(pair, axis=0)
        g = _dot_nt(xb, xb)
        for j in range(len(pair)):
            a = a + m_ref[lvl + j] * g[j * c:(j + 1) * c, j * c:(j + 1) * c]
    o = _dot(a.astype(BF16), v_b) + _dot_nt(qd.astype(BF16), st.astype(BF16))
    return o, st * jnp.exp(total) + _dot_tn(v_b, kd.astype(BF16))


def _hg_kernel(zq_ref, zff_ref, zfb_ref, zi_ref, zg_ref, lb_ref, g_ref,
               wf_ref, mf_ref, rf_ref, wb_ref, mb_ref, rb_ref,
               o_ref, accf_ref, accb_ref):
    c = HG_CHUNK
    seq = zq_ref.shape[0]
    n = seq // c
    lb_f = lb_ref[0:1, :]
    lb_b = lb_ref[1:2, :]

    def gated(ref, rows):
        z = ref[rows, :]
        return z * jax.nn.sigmoid(z)

    def body(i, carry):
        st_f, st_b = carry
        rf = pl.ds(pl.multiple_of(i * c, c), c)
        rb = pl.ds(pl.multiple_of((n - 1 - i) * c, c), c)
        of, st_f = _hg_chunk(gated(zq_ref, rf), zff_ref[rf, :], zi_ref[rf, :].astype(BF16), lb_f,
                             wf_ref, mf_ref, rf_ref, st_f, c - 1)
        accf_ref[rf, :] = of
        ob, st_b = _hg_chunk(gated(zq_ref, rb), zfb_ref[rb, :], zi_ref[rb, :].astype(BF16), lb_b,
                             wb_ref, mb_ref, rb_ref, st_b, 0)
        accb_ref[rb, :] = ob
        return st_f, st_b

    zero = jnp.zeros((HG_F, HG_F), F32)
    lax.fori_loop(0, n, body, (zero, zero))

    def finish(i, carry):
        rows = pl.ds(pl.multiple_of(i * c, c), c)
        o = accf_ref[rows, :] + accb_ref[rows, :]
        o = o * lax.rsqrt(jnp.mean(o * o, axis=-1, keepdims=True) + NORM_EPS)
        o_ref[rows, :] = (o * g_ref[...] * gated(zg_ref, rows)).astype(o_ref.dtype)
        return carry

    lax.fori_loop(0, n, finish, 0)


def hgrn2_scan(z, lb, norm_g, batch, seq):
    d = D_MODEL
    z3 = z.reshape(batch, seq, 5 * d)
    wf, mf, rf = _hg_constants(False)
    wb, mb, rb = _hg_constants(True)
    zspec = lambda j: pl.BlockSpec((None, seq, HG_F), lambda b, h: (b, 0, j * HG_HEADS + h))
    full = lambda a: pl.BlockSpec(a.shape, lambda b, h: (0,) * a.ndim)
    out = pl.pallas_call(
        _hg_kernel,
        grid=(batch, HG_HEADS),
        in_specs=[zspec(0), zspec(1), zspec(2), zspec(3), zspec(4),
                  pl.BlockSpec((2, HG_F), lambda b, h: (0, h)),
                  pl.BlockSpec((1, HG_F), lambda b, h: (0, h)),
                  full(wf), full(mf), full(rf), full(wb), full(mb), full(rb)],
        out_specs=pl.BlockSpec((None, seq, HG_F), lambda b, h: (b, 0, h)),
        out_shape=jax.ShapeDtypeStruct((batch, seq, d), BF16),
        scratch_shapes=[pltpu.VMEM((seq, HG_F), F32), pltpu.VMEM((seq, HG_F), F32)],
        compiler_params=_params("parallel", "parallel"),
        name="hgrn2_scan",
    )(z3, z3, z3, z3, z3, lb.astype(F32), norm_g.reshape(1, d).astype(F32),
      wf, mf, rf, wb, mb, rb)
    return out.reshape(batch * seq, d)


def _merge_sort_network(n):
    pairs = []
    p = 1
    while p < n:
        k = p
        while k >= 1:
            for j in range(k % p, n - k, 2 * k):
                for i in range(min(k, n - j - k)):
                    if (i + j) // (2 * p) == (i + j + k) // (2 * p):
                        pairs.append((i + j, i + j + k))
            k //= 2
        p *= 2
    return pairs


def _top_values(s, count):
    rows = s.shape[0]
    n = rows // F32_SUBLANES
    size = 1 << (n - 1).bit_length()
    neg = jnp.full((F32_SUBLANES, s.shape[1]), -jnp.inf, F32)
    lists = [s[i * F32_SUBLANES:(i + 1) * F32_SUBLANES] for i in range(n)] + [neg] * (size - n)
    for i, j in _merge_sort_network(size):
        hi = jnp.maximum(lists[i], lists[j])
        lists[j] = jnp.minimum(lists[i], lists[j])
        lists[i] = hi
    lists = lists[:min(size, count)]
    vals = []
    for it in range(count):
        head = lists[0]
        m = jnp.max(head, axis=0, keepdims=True)
        vals.append(m)
        hit = head >= m
        depth = min(len(lists), count - it - 1)
        for k in range(depth):
            nxt = lists[k + 1] if k + 1 < len(lists) else neg
            lists[k] = jnp.where(hit, nxt, lists[k])
        lists = lists[:max(depth, 1)]
    return jnp.concatenate(vals, axis=0)


def _peer_route_kernel(ht_ref, wq_ref, keys_ref, cnt_ref, a0_ref, r1_ref, b1_ref):
    kk = PEER_TOPK
    ht = ht_ref[...]
    for h in range(PEER_HEADS):
        scores = []
        for c in range(2):
            g = 2 * h + c
            qt = _dot(wq_ref[g * LANES:(g + 1) * LANES, :], ht)
            scores.append(_dot(keys_ref[c], qt.astype(BF16)))
        s0, s1 = scores
        sv0 = _top_values(s0, kk)
        sv1 = _top_values(s1, kk)
        r1 = jnp.full(s1.shape, float(kk), F32)
        for b in range(kk):
            r1 = jnp.where(s1 == sv1[b:b + 1], float(b), r1)
        cands = [sv0[a:a + 1] + sv1[0:kk // (a + 1)] for a in range(kk)]
        n_cand = sum(kk // (a + 1) for a in range(kk))
        pad = (-n_cand) % 8
        if pad:
            cands.append(jnp.full((pad, ht.shape[1]), -jnp.inf, F32))
        tau = _top_values(jnp.concatenate(cands, axis=0), kk)[kk - 1:kk]
        e0 = jnp.exp(sv0 - sv0[0:1])
        e1 = jnp.exp(sv1 - sv1[0:1])
        z = jnp.zeros_like(tau)
        for a in range(kk):
            nb = kk // (a + 1)
            sel = (sv0[a:a + 1] + sv1[0:nb]) >= tau
            z = z + e0[a:a + 1] * jnp.sum(jnp.where(sel, e1[0:nb], 0.0), axis=0, keepdims=True)
        cnt = jnp.zeros(s0.shape, F32)
        for a in range(kk):
            pair_ok = (sv0[a:a + 1] + sv1) >= tau
            n_a = jnp.sum(jnp.where(pair_ok, 1.0, 0.0), axis=0, keepdims=True)
            cnt = jnp.where(s0 == sv0[a:a + 1], n_a, cnt)
        cnt_ref[h] = cnt
        a0_ref[h] = jnp.exp(s0 - sv0[0:1])
        r1_ref[h] = r1.astype(BF16)
        b1_ref[h] = (jnp.exp(s1 - sv1[0:1]) * (0.5 / z)).astype(BF16)


def peer_route(ht, wq_t, keys, tt=1024):
    d, n = ht.shape
    shape = (PEER_HEADS, PEER_NKEYS, n)
    ospec = pl.BlockSpec((PEER_HEADS, PEER_NKEYS, tt), lambda i: (0, 0, i))
    return pl.pallas_call(
        _peer_route_kernel,
        grid=(n // tt,),
        in_specs=[pl.BlockSpec((d, tt), lambda i: (0, i)),
                  pl.BlockSpec(wq_t.shape, lambda i: (0, 0)),
                  pl.BlockSpec(keys.shape, lambda i: (0, 0, 0))],
        out_specs=[ospec, ospec, ospec, ospec],
        out_shape=[jax.ShapeDtypeStruct(shape, F32), jax.ShapeDtypeStruct(shape, F32),
                   jax.ShapeDtypeStruct(shape, BF16), jax.ShapeDtypeStruct(shape, BF16)],
        compiler_params=_params("parallel"),
        name="peer_route",
    )(ht, wq_t, keys)


def _peer_gate_chunk(hid, i0, cnt_ref, a0_ref, r1_ref, b1_ref):
    tt = hid.shape[1]
    zero = jnp.zeros((PEER_NKEYS, LANES), BF16)

    def row_tile(ref, h, ii, cols):
        row = jnp.broadcast_to(ref[h, ii:ii + 1, cols], (BF16_SUBLANES, LANES)).astype(BF16)
        return jnp.tile(row, (PEER_NKEYS // BF16_SUBLANES, 1))

    out_rows = []
    for k in range(PEER_CHUNK_ROWS):
        tiles = []
        for tc in range(tt // LANES):
            cols = slice(tc * LANES, (tc + 1) * LANES)
            gate = zero
            for h in range(PEER_HEADS):
                cnt = row_tile(cnt_ref, h, i0 + k, cols)
                a0 = row_tile(a0_ref, h, i0 + k, cols)
                gate = gate + jnp.where(r1_ref[h, :, cols] < cnt, a0 * b1_ref[h, :, cols], zero)
            x = hid[k * PEER_NKEYS:(k + 1) * PEER_NKEYS, cols]
            act = x * (1.0 + lax.erf(x.astype(F32) * RSQRT2)).astype(BF16)
            tiles.append(gate * act)
        out_rows.append(jnp.concatenate(tiles, axis=1))
    return jnp.concatenate(out_rows, axis=0)


def _pack_rows_kernel(x_ref, o_ref):
    o_ref[...] = pltpu.bitcast(x_ref[...].astype(BF16), jnp.uint32)


def _pack_rows_t_kernel(x_ref, o_ref):
    o_ref[...] = pltpu.bitcast(x_ref[...].T.astype(BF16), jnp.uint32)


def pack_expert_tables(u, v):
    layers, n_exp, d = u.shape
    eb = PEER_EXPERT_BLOCK
    nb = n_exp // eb
    u_packed = pl.pallas_call(
        _pack_rows_kernel,
        grid=(layers, nb),
        in_specs=[pl.BlockSpec((None, eb, d), lambda l, e: (l, e, 0))],
        out_specs=pl.BlockSpec((None, eb // 2, d), lambda l, e: (l, e, 0)),
        out_shape=jax.ShapeDtypeStruct((layers, n_exp // 2, d), jnp.uint32),
        compiler_params=_params("parallel", "parallel"),
        name="pack_u",
    )(u)
    v_packed = pl.pallas_call(
        _pack_rows_t_kernel,
        grid=(layers, nb),
        in_specs=[pl.BlockSpec((None, eb, d), lambda l, e: (l, e, 0))],
        out_specs=pl.BlockSpec((None, None, d // 2, eb), lambda l, e: (l, e, 0, 0)),
        out_shape=jax.ShapeDtypeStruct((layers, nb, d // 2, eb), jnp.uint32),
        compiler_params=_params("parallel", "parallel"),
        name="pack_vt",
    )(v)
    return u_packed, v_packed


def _unpack_rows(words):
    return pltpu.bitcast(words, BF16)


def _ordered_after(x, dep):
    z = pltpu.bitcast(dep[:BF16_SUBLANES, :LANES], jnp.uint32)
    z = (z >> 16) >> 16
    zero = pltpu.bitcast(z, BF16)
    zero = jnp.tile(zero, (x.shape[0] // BF16_SUBLANES, x.shape[1] // LANES))
    return x + zero


def _peer_ffn_kernel(ht_ref, u0_ref, ub_ref, un_ref, vta_ref, vtb_ref, cnta_ref, a0a_ref, cntb_ref, a0b_ref,
                     r1_in_ref, b1_in_ref, res_ref, g_ref, b_ref,
                     o_ref, acc_ref, hida_ref, hidb_ref, r1_ref, b1_ref):
    s = pl.program_id(1)
    half = ht_ref.shape[1] // 2

    @pl.when(s == 0)
    def _():
        acc_ref[...] = jnp.zeros_like(acc_ref)
        r1_ref[...] = r1_in_ref[...]
        b1_ref[...] = b1_in_ref[...]
        hida_ref[...] = _dot(_unpack_rows(u0_ref[...]), ht_ref[...]).astype(BF16)

    chunk = PEER_CHUNK_ROWS * PEER_NKEYS
    n_chunks = PEER_EXPERT_BLOCK // chunk

    def run_block(acc, hid_ref, vt_ref, cnt_ref, a0_ref, next_u_ref, next_hid_ref):
        for c in range(n_chunks):
            span = slice(c * chunk, (c + 1) * chunk)
            w = _peer_gate_chunk(hid_ref[span, :], c * PEER_CHUNK_ROWS, cnt_ref, a0_ref, r1_ref, b1_ref)
            acc = acc + _dot(_unpack_rows(vt_ref[:, span]), w)
            if c % (n_chunks // 2) == n_chunks // 2 - 1:
                hh = c // (n_chunks // 2)
                tcols = slice(hh * half, (hh + 1) * half)
                next_hid_ref[:, tcols] = _dot(_unpack_rows(next_u_ref[...]),
                                              _ordered_after(ht_ref[:, tcols], w)).astype(BF16)
        return acc

    acc = run_block(acc_ref[...], hida_ref, vta_ref, cnta_ref, a0a_ref, ub_ref, hidb_ref)
    acc_ref[...] = run_block(acc, hidb_ref, vtb_ref, cntb_ref, a0b_ref, un_ref, hida_ref)

    @pl.when(s == pl.num_programs(1) - 1)
    def _():
        y = acc_ref[...].T
        o_ref[...] = _layer_norm_rows(DN_ALPHA * res_ref[...] + y, g_ref[...], b_ref[...])


def peer_ffn(h, ht, route, u, v_blocks, layer, g, b, tt=512):
    n, d = h.shape
    eb = PEER_EXPERT_BLOCK
    last = 2 * u.shape[1] // eb - 1
    rspec = pl.BlockSpec((PEER_HEADS, PEER_NKEYS, tt), lambda i, s: (0, 0, i))
    rowspec = lambda off: pl.BlockSpec((PEER_HEADS, eb // PEER_NKEYS, tt), lambda i, s: (0, 2 * s + off, i))
    cnt, a0, r1, b1 = route
    return pl.pallas_call(
        _peer_ffn_kernel,
        grid=(n // tt, (last + 1) // 2),
        in_specs=[pl.BlockSpec((d, tt), lambda i, s: (0, i)),
                  pl.BlockSpec((None, eb // 2, d), lambda i, s: (layer, 0, 0)),
                  pl.BlockSpec((None, eb // 2, d), lambda i, s: (layer, 2 * s + 1, 0)),
                  pl.BlockSpec((None, eb // 2, d), lambda i, s: (layer, jnp.minimum(2 * s + 2, last), 0)),
                  pl.BlockSpec((None, None, d // 2, eb), lambda i, s: (layer, 2 * s, 0, 0)),
                  pl.BlockSpec((None, None, d // 2, eb), lambda i, s: (layer, 2 * s + 1, 0, 0)),
                  rowspec(0), rowspec(0), rowspec(1), rowspec(1),
                  rspec, rspec,
                  pl.BlockSpec((tt, d), lambda i, s: (i, 0)),
                  pl.BlockSpec((1, d), lambda i, s: (0, 0)),
                  pl.BlockSpec((1, d), lambda i, s: (0, 0))],
        out_specs=pl.BlockSpec((tt, d), lambda i, s: (i, 0)),
        out_shape=jax.ShapeDtypeStruct((n, d), F32),
        scratch_shapes=[pltpu.VMEM((d, tt), F32), pltpu.VMEM((eb, tt), BF16), pltpu.VMEM((eb, tt), BF16),
                        pltpu.VMEM((PEER_HEADS, PEER_NKEYS, tt), BF16),
                        pltpu.VMEM((PEER_HEADS, PEER_NKEYS, tt), BF16)],
        compiler_params=_params("parallel", "arbitrary"),
        name="peer_ffn",
    )(ht, u, u, u, v_blocks, v_blocks, cnt, a0, cnt, a0, r1, b1, h, g.reshape(1, d), b.reshape(1, d))


def kernel(x, na_w_in, na_rel_bias, na_w_out, mla_w_in, mla_q_norm, mla_kv_norm, mla_w_q_up, mla_w_kv_up, mla_w_out, hg_w_in, hg_lower_bound, hg_norm, hg_w_out, peer_w_q, peer_sub_keys, peer_u, peer_v, ln_mix_g, ln_mix_b, ln_ffn_g, ln_ffn_b):
    batch, seq, d = x.shape
    lb_w = jax.nn.softmax(hg_lower_bound.astype(F32), axis=0)
    lb_all = jnp.cumsum(lb_w, axis=0) - lb_w[0:1]
    h = x.reshape(batch * seq, d)
    u_all, v_all = pack_expert_tables(peer_u, peer_v)
    for layer in range(DEPTH):
        kind = layer % N_MIXERS
        j = layer // N_MIXERS
        if kind == 0:
            qk, vt = na_project(h, na_w_in[j], batch)
            mix_in = neighborhood_attention(qk, vt, na_rel_bias[j])
            w_out = na_w_out[j]
        elif kind == 1:
            q, k, v = mla_projections(h, mla_w_in[j], mla_q_norm[j], mla_kv_norm[j],
                                      mla_w_q_up[j], mla_w_kv_up[j], seq)
            mix_in = mla_attention(q, k, v, batch, seq)
            w_out = mla_w_out[j]
        else:
            z = matmul(h, hg_w_in[j].astype(BF16), F32)
            mix_in = hgrn2_scan(z, lb_all[layer], hg_norm[j], batch, seq)
            w_out = hg_w_out[j]
        h, ht = matmul_res_ln(mix_in, w_out.astype(BF16), h, ln_mix_g[layer], ln_mix_b[layer])
        route = peer_route(ht, peer_w_q[layer].T.astype(BF16), peer_sub_keys[layer].astype(BF16))
        h = peer_ffn(h, ht, route, u_all, v_all, layer, ln_ffn_g[layer], ln_ffn_b[layer])
    return h.reshape(batch, seq, d)
```
